```python
import jax, jax.numpy as jnp
from jax import lax
import numpy as np

D_MODEL = 1024
BATCH = 4
SEQ = 8192
DEPTH = 1

HEAD_DIM = 64
N_Q_HEADS = 8
N_KV_HEADS = 2
Q_PER_KV = N_Q_HEADS // N_KV_HEADS
ATTN_WIDTH = N_Q_HEADS * HEAD_DIM
KV_WIDTH = N_KV_HEADS * HEAD_DIM
WINDOW = 128
BLOCK = 128
N_BUCKETS = 32
MAX_DISTANCE = 128

SSM_HEAD_DIM = 64
SSM_HEADS = 8
SSM_GROUPS = 2
HEADS_PER_GROUP = SSM_HEADS // SSM_GROUPS
SSM_WIDTH = SSM_HEADS * SSM_HEAD_DIM
D_STATE = 128
CONV_K = 4
CHUNK = 128
XBC_WIDTH = SSM_WIDTH + 2 * SSM_GROUPS * D_STATE

MIX_WIDTH = ATTN_WIDTH + SSM_WIDTH
IN_WIDTH = ATTN_WIDTH + 2 * KV_WIDTH + SSM_WIDTH + XBC_WIDTH + SSM_HEADS
D_FF = -(-8 * D_MODEL // (3 * 256)) * 256
EPS = 1e-6

kernel_name = "hymba_swa_sink_ssd_adaln_block"


def rmsnorm(x, g):
    xf = x.astype(jnp.float32)
    y = xf * lax.rsqrt(jnp.mean(xf * xf, axis=-1, keepdims=True) + EPS)
    return (y * g.astype(jnp.float32)).astype(x.dtype)


def t5_buckets(dist):
    n = np.maximum(dist, 0)
    max_exact = N_BUCKETS // 2
    large = max_exact + (np.log(np.maximum(n, 1) / max_exact) / np.log(MAX_DISTANCE / max_exact)
                         * (N_BUCKETS - max_exact)).astype(np.int32)
    large = np.minimum(large, N_BUCKETS - 1)
    return np.where(n < max_exact, n, large).astype(np.int32)


def sliding_window_attention(q, k, v, sinks, rel_bias):
    b, s, _ = q.shape
    nb = s // BLOCK
    qb = q.reshape(b, nb, BLOCK, N_KV_HEADS, Q_PER_KV, HEAD_DIM)

    def band(t):
        t = t.reshape(b, s, N_KV_HEADS, HEAD_DIM)
        t = jnp.pad(t, ((0, 0), (BLOCK, 0), (0, 0), (0, 0)))
        t = t.reshape(b, nb + 1, BLOCK, N_KV_HEADS, HEAD_DIM)
        return jnp.concatenate([t[:, :-1], t[:, 1:]], axis=2)

    kb, vb = band(k), band(v)
    dist = np.arange(BLOCK)[:, None] + BLOCK - np.arange(2 * BLOCK)[None, :]
    key_pos = np.arange(nb)[:, None] * BLOCK - BLOCK + np.arange(2 * BLOCK)[None, :]
    mask = ((dist >= 0) & (dist < WINDOW))[None] & (key_pos >= 0)[:, None, :]
    mask = mask.reshape(nb, 1, 1, BLOCK, 2 * BLOCK)
    bias = rel_bias.astype(jnp.float32)[t5_buckets(dist)]
    bias = jnp.transpose(bias, (2, 0, 1)).reshape(N_KV_HEADS, Q_PER_KV, BLOCK, 2 * BLOCK)

    scores = jnp.einsum("bnqkgd,bnskd->bnkgqs", qb, kb).astype(jnp.float32)
    scores = scores * (HEAD_DIM ** -0.5) + bias
    scores = jnp.where(mask, scores, -jnp.inf)
    sink = sinks.astype(jnp.float32).reshape(N_KV_HEADS, Q_PER_KV, 1, 1)
    m = jnp.maximum(jnp.max(scores, axis=-1, keepdims=True), sink)
    p = jnp.exp(scores - m)
    denom = jnp.sum(p, axis=-1, keepdims=True) + jnp.exp(sink - m)
    out = jnp.einsum("bnkgqs,bnskd->bnkgqd", p, vb.astype(jnp.float32)) / denom
    out = jnp.transpose(out, (0, 1, 4, 2, 3, 5)).reshape(b, s, ATTN_WIDTH)
    return out.astype(q.dtype)


def ssd_scan(xs, dt, A, Bm, Cm, D_skip):
    b, s = xs.shape[:2]
    nc = s // CHUNK
    xs = xs.astype(jnp.float32)
    xdt = xs * dt[..., None]
    xc = xdt.reshape(b, nc, CHUNK, SSM_GROUPS, HEADS_PER_GROUP, SSM_HEAD_DIM)
    Bc = Bm.astype(jnp.float32).reshape(b, nc, CHUNK, SSM_GROUPS, D_STATE)
    Cc = Cm.astype(jnp.float32).reshape(b, nc, CHUNK, SSM_GROUPS, D_STATE)
    dtA = (dt * A).reshape(b, nc, CHUNK, SSM_GROUPS, HEADS_PER_GROUP)
    Acs = jnp.cumsum(jnp.moveaxis(dtA, 2, -1), axis=-1)

    causal = np.tril(np.ones((CHUNK, CHUNK), dtype=bool))
    seg = Acs[..., :, None] - Acs[..., None, :]
    Lmat = jnp.exp(jnp.where(causal, seg, -jnp.inf))
    CB = jnp.einsum("bclgn,bcsgn->bcgls", Cc, Bc)
    W = CB[:, :, :, None] * Lmat
    y_diag = jnp.einsum("bcgrls,bcsgrp->bclgrp", W, xc)

    decay_states = jnp.exp(Acs[..., -1:] - Acs)
    states = jnp.einsum("bclgn,bcgrl,bclgrp->bcgrpn", Bc, decay_states, xc)
    chunk_decay = jnp.exp(Acs[..., -1])

    def step(h, inp):
        s_c, d_c = inp
        return h * d_c[..., None, None] + s_c, h

    h0 = jnp.zeros((b, SSM_GROUPS, HEADS_PER_GROUP, SSM_HEAD_DIM, D_STATE), jnp.float32)
    _, prev = lax.scan(step, h0, (jnp.moveaxis(states, 1, 0), jnp.moveaxis(chunk_decay, 1, 0)))
    prev = jnp.moveaxis(prev, 0, 1)
    y_off = jnp.einsum("bclgn,bcgrpn,bcgrl->bclgrp", Cc, prev, jnp.exp(Acs))

    y = (y_diag + y_off).reshape(b, s, SSM_GROUPS, HEADS_PER_GROUP, SSM_HEAD_DIM)
    y = y + D_skip.astype(jnp.float32)[:, :, None] * xs
    return y


def hybrid_mixer(h, w_in, conv_w, conv_b, dt_bias, A_log, D_skip, sinks,
                 attn_out_norm, ssm_out_norm, w_o, rel_bias):
    b, s, _ = h.shape
    proj = h @ w_in
    o1 = ATTN_WIDTH
    o2 = o1 + KV_WIDTH
    o3 = o2 + KV_WIDTH
    o4 = o3 + SSM_WIDTH
    o5 = o4 + XBC_WIDTH
    q, k, v = proj[..., :o1], proj[..., o1:o2], proj[..., o2:o3]
    z, xbc, dt_raw = proj[..., o3:o4], proj[..., o4:o5], proj[..., o5:]

    y_attn = sliding_window_attention(q, k, v, sinks, rel_bias)
    y_attn = rmsnorm(y_attn, attn_out_norm)

    xbc = lax.conv_general_dilated(xbc, conv_w[:, None, :], window_strides=(1,),
                                   padding=[(CONV_K - 1, 0)],
                                   dimension_numbers=("NWC", "WIO", "NWC"),
                                   feature_group_count=XBC_WIDTH)
    xbc = jax.nn.silu(xbc + conv_b)
    xs = xbc[..., :SSM_WIDTH].reshape(b, s, SSM_GROUPS, HEADS_PER_GROUP, SSM_HEAD_DIM)
    Bm = xbc[..., SSM_WIDTH:SSM_WIDTH + SSM_GROUPS * D_STATE].reshape(b, s, SSM_GROUPS, D_STATE)
    Cm = xbc[..., SSM_WIDTH + SSM_GROUPS * D_STATE:].reshape(b, s, SSM_GROUPS, D_STATE)
    dt = jax.nn.softplus(dt_raw.astype(jnp.float32) + dt_bias.astype(jnp.float32))
    dt = dt.reshape(b, s, SSM_GROUPS, HEADS_PER_GROUP)
    A = -jnp.exp(A_log.astype(jnp.float32)).reshape(SSM_GROUPS, HEADS_PER_GROUP)
    y_ssm = ssd_scan(xs, dt, A, Bm, Cm, D_skip.reshape(SSM_GROUPS, HEADS_PER_GROUP))
    y_ssm = y_ssm.reshape(b, s, SSM_GROUPS, SSM_WIDTH // SSM_GROUPS)
    gz = jax.nn.silu(z.astype(jnp.float32)).reshape(b, s, SSM_GROUPS, SSM_WIDTH // SSM_GROUPS)
    y_ssm = rmsnorm(y_ssm * gz, ssm_out_norm.reshape(SSM_GROUPS, SSM_WIDTH // SSM_GROUPS))
    y_ssm = y_ssm.reshape(b, s, SSM_WIDTH).astype(h.dtype)

    return jnp.concatenate([y_attn, y_ssm], axis=-1) @ w_o


def swiglu(h, w_gate_up, w_down):
    gu = h @ w_gate_up
    g, u = gu[..., :D_FF], gu[..., D_FF:]
    return (jax.nn.silu(g) * u) @ w_down


def setup_inputs(seed: int = 0) -> dict:
    key = jax.random.key(seed)
    ks = jax.random.split(key, 24)
    f32 = jnp.float32
    nrm = lambda k, shape, sc: jax.random.normal(k, shape, f32) * sc
    dt = jnp.exp(jax.random.uniform(ks[8], (DEPTH, SSM_HEADS), f32)
                 * (jnp.log(0.1) - jnp.log(0.001)) + jnp.log(0.001))
    return {
        "x": nrm(ks[0], (BATCH, SEQ, D_MODEL), 1.0),
        "c": nrm(ks[1], (BATCH, D_MODEL), 1.0),
        "ada_w": nrm(ks[2], (DEPTH, D_MODEL, 6 * D_MODEL), D_MODEL ** -0.5),
        "ada_b": nrm(ks[3], (DEPTH, 6 * D_MODEL), 0.01),
        "norm1": 1.0 + nrm(ks[4], (DEPTH, D_MODEL), 0.01),
        "w_in": nrm(ks[5], (DEPTH, D_MODEL, IN_WIDTH), D_MODEL ** -0.5),
        "conv_w": nrm(ks[6], (DEPTH, CONV_K, XBC_WIDTH), CONV_K ** -0.5),
        "conv_b": nrm(ks[7], (DEPTH, XBC_WIDTH), 0.01),
        "dt_bias": dt + jnp.log(-jnp.expm1(-dt)),
        "A_log": jnp.log(jax.random.uniform(ks[9], (DEPTH, SSM_HEADS), f32, 1.0, 16.0)),
        "D_skip": 1.0 + nrm(ks[10], (DEPTH, SSM_HEADS), 0.1),
        "sinks": nrm(ks[11], (DEPTH, N_Q_HEADS), 0.5),
        "attn_out_norm": 1.0 + nrm(ks[12], (DEPTH, ATTN_WIDTH), 0.01),
        "ssm_out_norm": 1.0 + nrm(ks[13], (DEPTH, SSM_WIDTH), 0.01),
        "w_o": nrm(ks[14], (DEPTH, MIX_WIDTH, D_MODEL), MIX_WIDTH ** -0.5),
        "norm2": 1.0 + nrm(ks[15], (DEPTH, D_MODEL), 0.01),
        "w_gate_up": nrm(ks[16], (DEPTH, D_MODEL, 2 * D_FF), D_MODEL ** -0.5),
        "w_down": nrm(ks[17], (DEPTH, D_FF, D_MODEL), D_FF ** -0.5),
        "rel_bias": nrm(ks[18], (N_BUCKETS, N_Q_HEADS), 0.5),
        "final_norm": 1.0 + nrm(ks[19], (D_MODEL,), 0.01),
    }


def reference(x, c, ada_w, ada_b, norm1, w_in, conv_w, conv_b, dt_bias, A_log, D_skip,
              sinks, attn_out_norm, ssm_out_norm, w_o, norm2, w_gate_up, w_down,
              rel_bias, final_norm):
    cond = jax.nn.silu(c)
    for l in range(DEPTH):
        mod = (cond @ ada_w[l] + ada_b[l])[:, None, :]
        shift1, scale1, gate1, shift2, scale2, gate2 = jnp.split(mod, 6, axis=-1)
        h = rmsnorm(x, norm1[l]) * (1.0 + scale1) + shift1
        x = x + gate1 * hybrid_mixer(h, w_in[l], conv_w[l], conv_b[l], dt_bias[l], A_log[l],
                                     D_skip[l], sinks[l], attn_out_norm[l], ssm_out_norm[l],
                                     w_o[l], rel_bias)
        h = rmsnorm(x, norm2[l]) * (1.0 + scale2) + shift2
        x = x + gate2 * swiglu(h, w_gate_up[l], w_down[l])
    return rmsnorm(x, final_norm)
```

```python
import functools

import numpy as np
import jax
import jax.numpy as jnp
from jax import lax
from jax.experimental import pallas as pl
from jax.experimental.pallas import tpu as pltpu

F32 = jnp.float32
BF16 = jnp.bfloat16

D_MODEL = 1024
HEAD_DIM = 64
N_Q_HEADS = 8
N_KV_HEADS = 2
Q_PER_KV = N_Q_HEADS // N_KV_HEADS
ATTN_WIDTH = N_Q_HEADS * HEAD_DIM
KV_WIDTH = N_KV_HEADS * HEAD_DIM
WINDOW = 128
BLOCK = 128
N_BUCKETS = 32
MAX_DISTANCE = 128

SSM_HEAD_DIM = 64
SSM_HEADS = 8
SSM_GROUPS = 2
HEADS_PER_GROUP = SSM_HEADS // SSM_GROUPS
SSM_WIDTH = SSM_HEADS * SSM_HEAD_DIM
GROUP_WIDTH = SSM_WIDTH // SSM_GROUPS
D_STATE = 128
CONV_K = 4
CHUNK = 128
XBC_WIDTH = SSM_WIDTH + 2 * SSM_GROUPS * D_STATE
D_FF = 2816
EPS = 1e-6

LANES = 128
SUBLANES = 8
VMEM_LIMIT_BYTES = 56 * 1024 * 1024

ADA_TN = 1536
INPROJ_TM = 512
FFN_TM = 512
FFN_FC = 256

Q_END = ATTN_WIDTH
KV_END = Q_END + 2 * KV_WIDTH
Z_END = KV_END + SSM_WIDTH
XBC_END = Z_END + XBC_WIDTH


def _silu(x):
    return x * jax.nn.sigmoid(x)


def _rms(x):
    return x * lax.rsqrt(jnp.mean(x * x, axis=-1, keepdims=True) + EPS)


def _split3(x):
    hi = x.astype(BF16)
    r1 = x - hi.astype(F32)
    mid = r1.astype(BF16)
    lo = (r1 - mid.astype(F32)).astype(BF16)
    return hi, mid, lo


def _dot(a, b):
    return jnp.dot(a, b, preferred_element_type=F32)


def _dot_nt(a, b):
    return lax.dot_general(a, b, (((1,), (1,)), ((), ())), preferred_element_type=F32)


def _dot_tn(a, b):
    return lax.dot_general(a, b, (((0,), (0,)), ((), ())), preferred_element_type=F32)


def _ada_kernel(c_ref, w_ref, b_ref, o_ref):
    cond = _silu(c_ref[...])
    o_ref[...] = _dot(cond.astype(BF16), w_ref[...].astype(BF16)) + b_ref[...]


def _ada_call(c_pad, ada_w, ada_b):
    n = ada_w.shape[1]
    return pl.pallas_call(
        _ada_kernel,
        grid=(n // ADA_TN,),
        in_specs=[
            pl.BlockSpec((SUBLANES, D_MODEL), lambda j: (0, 0)),
            pl.BlockSpec((D_MODEL, ADA_TN), lambda j: (0, j)),
            pl.BlockSpec((1, ADA_TN), lambda j: (0, j)),
        ],
        out_specs=pl.BlockSpec((SUBLANES, ADA_TN), lambda j: (0, j)),
        out_shape=jax.ShapeDtypeStruct((SUBLANES, n), F32),
        compiler_params=pltpu.CompilerParams(dimension_semantics=("parallel",)),
        name="ada_mod",
    )(c_pad, ada_w, ada_b)


def _t5_bucket_table():
    i = np.arange(BLOCK)[:, None]
    j = np.arange(BLOCK)[None, :]
    n = np.where(j <= i, i - j, i + BLOCK - j)
    assert WINDOW == BLOCK and n.min() >= 0 and n.max() < WINDOW
    max_exact = N_BUCKETS // 2
    large = max_exact + (np.log(np.maximum(n, 1) / max_exact) / np.log(MAX_DISTANCE / max_exact)
                         * (N_BUCKETS - max_exact)).astype(np.int32)
    large = np.minimum(large, N_BUCKETS - 1)
    bucket = np.where(n < max_exact, n, large).astype(np.int32)
    first = np.where(j <= i, bucket, -1)
    return np.stack([first, bucket]).astype(np.int32)


def _bias_kernel(idx_ref, rb_ref, o_ref):
    for v in range(2):
        idx = idx_ref[v]
        hits = [idx == b for b in range(N_BUCKETS)]
        for h in range(N_Q_HEADS):
            acc = jnp.full((BLOCK, BLOCK), -jnp.inf, F32)
            for b in range(N_BUCKETS):
                acc = jnp.where(hits[b], rb_ref[b, h], acc)
            o_ref[v, h] = acc


def _bias_call(rel_bias):
    idx = jnp.asarray(_t5_bucket_table())
    return pl.pallas_call(
        _bias_kernel,
        in_specs=[
            pl.BlockSpec(memory_space=pltpu.VMEM),
            pl.BlockSpec(memory_space=pltpu.SMEM),
        ],
        out_specs=pl.BlockSpec(memory_space=pltpu.VMEM),
        out_shape=jax.ShapeDtypeStruct((2, N_Q_HEADS, BLOCK, BLOCK), F32),
        name="t5_bias",
    )(idx, rel_bias)


def _inproj_kernel(x_ref, mod_ref, n1_ref, w_ref, wdt_ref, q_ref, kv_ref, z_ref, xbc_ref, dt_ref):
    h = _rms(x_ref[...]) * n1_ref[...]
    h = h * (1.0 + mod_ref[1:2, :]) + mod_ref[0:1, :]
    hb = h.astype(BF16)
    q_ref[...] = _dot(hb, w_ref[:, 0:Q_END]).astype(BF16)
    kv_ref[...] = _dot(hb, w_ref[:, Q_END:KV_END]).astype(BF16)
    z_ref[...] = _dot(hb, w_ref[:, KV_END:Z_END]).astype(BF16)
    xbc_ref[...] = _dot(hb, w_ref[:, Z_END:XBC_END]).astype(BF16)
    dt_ref[...] = _dot(hb, wdt_ref[...])


def _inproj_call(x, mod, norm1, w_main, w_dt):
    b, s, _ = x.shape
    tm = INPROJ_TM
    row = lambda width: pl.BlockSpec((None, tm, width), lambda i, j: (i, j, 0))
    const = lambda shape: pl.BlockSpec(shape, lambda i, j: (0,) * len(shape))
    return pl.pallas_call(
        _inproj_kernel,
        grid=(b, s // tm),
        in_specs=[
            row(D_MODEL),
            pl.BlockSpec((None, 6, D_MODEL), lambda i, j: (i, 0, 0)),
            const((1, D_MODEL)),
            const((D_MODEL, XBC_END)),
            const((D_MODEL, LANES)),
        ],
        out_specs=[row(ATTN_WIDTH), row(2 * KV_WIDTH), row(SSM_WIDTH), row(XBC_WIDTH), row(LANES)],
        out_shape=[
            jax.ShapeDtypeStruct((b, s, ATTN_WIDTH), BF16),
            jax.ShapeDtypeStruct((b, s, 2 * KV_WIDTH), BF16),
            jax.ShapeDtypeStruct((b, s, SSM_WIDTH), BF16),
            jax.ShapeDtypeStruct((b, s, XBC_WIDTH), BF16),
            jax.ShapeDtypeStruct((b, s, LANES), F32),
        ],
        compiler_params=pltpu.CompilerParams(
            dimension_semantics=("parallel", "parallel"), vmem_limit_bytes=VMEM_LIMIT_BYTES),
        name="in_proj",
    )(x, mod, norm1, w_main, w_dt)


def _attention(q, kv, kv_prev, bias_ref, sinks_ref):
    row_i = lax.broadcasted_iota(jnp.int32, (BLOCK, BLOCK), 0)
    col_i = lax.broadcasted_iota(jnp.int32, (BLOCK, BLOCK), 1)
    own = col_i <= row_i
    outs = []
    for g in range(N_KV_HEADS):
        k_lo, v_lo = g * HEAD_DIM, KV_WIDTH + g * HEAD_DIM
        k_own, k_prev = kv[:, k_lo:k_lo + HEAD_DIM], kv_prev[:, k_lo:k_lo + HEAD_DIM]
        v_own, v_prev = kv[:, v_lo:v_lo + HEAD_DIM], kv_prev[:, v_lo:v_lo + HEAD_DIM]
        for r in range(Q_PER_KV):
            h = g * Q_PER_KV + r
            qh = q[:, h * HEAD_DIM:(h + 1) * HEAD_DIM]
            s = jnp.where(own, _dot_nt(qh, k_own), _dot_nt(qh, k_prev)) + bias_ref[h]
            sink = sinks_ref[h]
            m = jnp.maximum(jnp.max(s, axis=-1, keepdims=True), sink)
            p = jnp.exp(s - m)
            denom = jnp.sum(p, axis=-1, keepdims=True) + jnp.exp(sink - m)
            p_own = jnp.where(own, p, 0.0).astype(BF16)
            p_prev = jnp.where(own, 0.0, p).astype(BF16)
            outs.append((_dot(p_own, v_own) + _dot(p_prev, v_prev)) / denom)
    return jnp.concatenate(outs, axis=-1)


def _expand_heads(vals, expand):
    hi, mid, _ = _split3(vals)
    return _dot(hi, expand) + _dot(mid, expand)


def _ssd(xbc, z, dt_raw, tail_ref, state_ref, cw_ref, cb_ref, dtb_ref, alog_ref, dskip_ref, expand_ref):
    xe = jnp.concatenate([tail_ref[...], xbc], axis=0)
    conv = cw_ref[3:4, :] * xbc + cb_ref[...]
    for k in range(CONV_K - 1):
        shift = CONV_K - 1 - k
        conv = conv + cw_ref[k:k + 1, :] * pltpu.roll(xe, shift, axis=0)[SUBLANES:, :]
    tail_ref[...] = xbc[CHUNK - SUBLANES:, :]
    act = _silu(conv)
    xs = act[:, :SSM_WIDTH]

    dt = jax.nn.softplus(dt_raw + dtb_ref[...])
    a = -jnp.exp(alog_ref[...])
    dta = dt * a
    row_i = lax.broadcasted_iota(jnp.int32, (CHUNK, CHUNK), 0)
    col_i = lax.broadcasted_iota(jnp.int32, (CHUNK, CHUNK), 1)
    causal = row_i >= col_i
    tri = causal.astype(BF16)
    hi, mid, lo = _split3(dta)
    acs = _dot(tri, hi) + _dot(tri, mid) + _dot(tri, lo)
    acs_t = acs.T
    acs_last = acs[CHUNK - 1:CHUNK, :]

    expand = expand_ref[...]
    dt_x = _expand_heads(dt, expand)
    decay_in_x = _expand_heads(jnp.exp(acs), expand)
    decay_out_x = _expand_heads(jnp.exp(acs_last - acs), expand)

    xdt = xs * dt_x
    xdt_b = xdt.astype(BF16)
    xdec_b = (xdt * decay_out_x).astype(BF16)
    lane = lax.broadcasted_iota(jnp.int32, (CHUNK, GROUP_WIDTH), 1)

    ys = []
    for g in range(SSM_GROUPS):
        lo_l, hi_l = g * GROUP_WIDTH, (g + 1) * GROUP_WIDTH
        bm = act[:, SSM_WIDTH + g * D_STATE:SSM_WIDTH + (g + 1) * D_STATE].astype(BF16)
        cm = act[:, SSM_WIDTH + (SSM_GROUPS + g) * D_STATE:SSM_WIDTH + (SSM_GROUPS + g + 1) * D_STATE].astype(BF16)
        cb = _dot_nt(cm, bm)
        xg = xdt_b[:, lo_l:hi_l]
        y_g = jnp.zeros((CHUNK, GROUP_WIDTH), F32)
        for r in range(HEADS_PER_GROUP):
            h = g * HEADS_PER_GROUP + r
            seg = acs[:, h:h + 1] - acs_t[h:h + 1, :]
            lmat = jnp.exp(jnp.where(causal, seg, -jnp.inf))
            y_r = _dot((cb * lmat).astype(BF16), xg)
            in_head = (lane >= r * SSM_HEAD_DIM) & (lane < (r + 1) * SSM_HEAD_DIM)
            y_g = jnp.where(in_head, y_r, y_g)
        prev = state_ref[g]
        y_off = _dot(cm, prev.astype(BF16)) * decay_in_x[:, lo_l:hi_l]
        chunk_decay = decay_in_x[CHUNK - 1:CHUNK, lo_l:hi_l]
        state_ref[g] = prev * chunk_decay + _dot_tn(bm, xdec_b[:, lo_l:hi_l])
        ys.append(y_g + y_off)
    y = jnp.concatenate(ys, axis=-1) + dskip_ref[...] * xs
    return y * _silu(z)


def _mixer_kernel(q_ref, kv_ref, kvp_ref, z_ref, xbc_ref, dt_ref, bias_ref, sinks_ref, cw_ref, cb_ref, dtb_ref,
                  alog_ref, dskip_ref, an_ref, sn_ref, expand_ref, o_ref, tail_ref, state_ref):
    @pl.when(pl.program_id(1) == 0)
    def _():
        tail_ref[...] = jnp.zeros_like(tail_ref)
        state_ref[...] = jnp.zeros_like(state_ref)

    y_attn = _attention(q_ref[...], kv_ref[...], kvp_ref[...], bias_ref, sinks_ref)
    o_ref[:, :ATTN_WIDTH] = (_rms(y_attn) * an_ref[...]).astype(BF16)

    y = _ssd(xbc_ref[...].astype(F32), z_ref[...].astype(F32), dt_ref[...], tail_ref, state_ref,
             cw_ref, cb_ref, dtb_ref, alog_ref, dskip_ref, expand_ref)
    y = jnp.concatenate([_rms(y[:, :GROUP_WIDTH]), _rms(y[:, GROUP_WIDTH:])], axis=-1)
    o_ref[:, ATTN_WIDTH:] = (y * sn_ref[...]).astype(BF16)


def _mixer_call(q, kv, z, xbc, dt, bias, sinks, conv_w, conv_b, dt_bias, a_log, d_skip_x, attn_norm,
                ssm_norm, expand):
    b, s, _ = q.shape
    row = lambda width: pl.BlockSpec((None, CHUNK, width), lambda i, c: (i, c, 0))
    const = lambda shape: pl.BlockSpec(shape, lambda i, c: (0,) * len(shape))
    return pl.pallas_call(
        _mixer_kernel,
        grid=(b, s // CHUNK),
        in_specs=[
            row(ATTN_WIDTH), row(2 * KV_WIDTH),
            pl.BlockSpec((None, BLOCK, 2 * KV_WIDTH), lambda i, c: (i, jnp.maximum(c - 1, 0), 0)),
            row(SSM_WIDTH), row(XBC_WIDTH), row(LANES),
            pl.BlockSpec((None, N_Q_HEADS, BLOCK, BLOCK), lambda i, c: (jnp.minimum(c, 1), 0, 0, 0)),
            pl.BlockSpec(memory_space=pltpu.SMEM),
            const((CONV_K, XBC_WIDTH)), const((1, XBC_WIDTH)), const((1, LANES)), const((1, LANES)),
            const((1, SSM_WIDTH)), const((1, ATTN_WIDTH)), const((1, SSM_WIDTH)),
            const((LANES, SSM_WIDTH)),
        ],
        out_specs=row(ATTN_WIDTH + SSM_WIDTH),
        out_shape=jax.ShapeDtypeStruct((b, s, ATTN_WIDTH + SSM_WIDTH), BF16),
        scratch_shapes=[
            pltpu.VMEM((SUBLANES, XBC_WIDTH), F32),
            pltpu.VMEM((SSM_GROUPS, D_STATE, GROUP_WIDTH), F32),
        ],
        compiler_params=pltpu.CompilerParams(
            dimension_semantics=("parallel", "arbitrary"), vmem_limit_bytes=VMEM_LIMIT_BYTES),
        name="mixer",
    )(q, kv, kv, z, xbc, dt, bias, sinks, conv_w, conv_b, dt_bias, a_log, d_skip_x, attn_norm, ssm_norm, expand)


def _ffn_kernel(mix_ref, x_ref, mod_ref, wo_ref, n2_ref, wgu_ref, wd_ref, fn_ref, o_ref, h_ref, acc_ref):
    x1 = x_ref[...] + mod_ref[2:3, :] * _dot(mix_ref[...], wo_ref[...])
    h = _rms(x1) * n2_ref[...]
    h_ref[...] = (h * (1.0 + mod_ref[4:5, :]) + mod_ref[3:4, :]).astype(BF16)
    o_ref[...] = x1
    acc_ref[...] = jnp.zeros_like(acc_ref)

    def body(j, carry):
        c0 = pl.multiple_of(j * FFN_FC, FFN_FC)
        hb = h_ref[...]
        g = _dot(hb, wgu_ref[:, pl.ds(c0, FFN_FC)])
        u = _dot(hb, wgu_ref[:, pl.ds(D_FF + c0, FFN_FC)])
        a = (_silu(g) * u).astype(BF16)
        acc_ref[...] += _dot(a, wd_ref[pl.ds(c0, FFN_FC), :])
        return carry

    lax.fori_loop(0, D_FF // FFN_FC, body, 0)
    o_ref[...] = _rms(o_ref[...] + mod_ref[5:6, :] * acc_ref[...]) * fn_ref[...]


def _ffn_call(mix, x, mod, w_o, norm2, w_gu, w_down, final_norm):
    b, s, _ = x.shape
    tm = FFN_TM
    row = lambda: pl.BlockSpec((None, tm, D_MODEL), lambda i, j: (i, j, 0))
    const = lambda shape: pl.BlockSpec(shape, lambda i, j: (0,) * len(shape), pipeline_mode=pl.Buffered(1))
    return pl.pallas_call(
        _ffn_kernel,
        grid=(b, s // tm),
        in_specs=[
            row(), row(),
            pl.BlockSpec((None, 6, D_MODEL), lambda i, j: (i, 0, 0)),
            const((D_MODEL, D_MODEL)), const((1, D_MODEL)),
            const((D_MODEL, 2 * D_FF)), const((D_FF, D_MODEL)), const((1, D_MODEL)),
        ],
        out_specs=row(),
        out_shape=jax.ShapeDtypeStruct((b, s, D_MODEL), F32),
        scratch_shapes=[pltpu.VMEM((tm, D_MODEL), BF16), pltpu.VMEM((tm, D_MODEL), F32)],
        compiler_params=pltpu.CompilerParams(
            dimension_semantics=("parallel", "parallel"), vmem_limit_bytes=VMEM_LIMIT_BYTES),
        name="ffn",
    )(mix, x, mod, w_o, norm2, w_gu, w_down, final_norm)


def _head_expand_matrix():
    e = np.zeros((LANES, SSM_WIDTH), np.float32)
    for h in range(SSM_HEADS):
        e[h, h * SSM_HEAD_DIM:(h + 1) * SSM_HEAD_DIM] = 1.0
    return e


def kernel(x, c, ada_w, ada_b, norm1, w_in, conv_w, conv_b, dt_bias, A_log, D_skip, sinks, attn_out_norm,
           ssm_out_norm, w_o, norm2, w_gate_up, w_down, rel_bias, final_norm):
    assert ada_w.shape[0] == 1, "the final RMSNorm is fused into the (single) layer's FFN call"
    b = x.shape[0]
    bias = _bias_call(rel_bias)
    expand = jnp.asarray(_head_expand_matrix(), BF16)
    pad_heads = lambda v: jnp.pad(v, (0, LANES - SSM_HEADS))[None, :]
    c_pad = jnp.pad(c, ((0, SUBLANES - b), (0, 0)))
    mod = _ada_call(c_pad, ada_w[0], ada_b[0][None, :])[:b].reshape(b, 6, D_MODEL)
    w_q = w_in[0][:, :Q_END] * (HEAD_DIM ** -0.5)
    w_main = jnp.concatenate([w_q, w_in[0][:, Q_END:XBC_END]], axis=1).astype(BF16)
    w_dt = jnp.pad(w_in[0][:, XBC_END:], ((0, 0), (0, LANES - SSM_HEADS))).astype(BF16)
    q, kv, z, xbc, dt = _inproj_call(x, mod, norm1[0][None, :], w_main, w_dt)
    mix = _mixer_call(
        q, kv, z, xbc, dt, bias, sinks[0], conv_w[0], conv_b[0][None, :], pad_heads(dt_bias[0]),
        pad_heads(A_log[0]), jnp.repeat(D_skip[0], SSM_HEAD_DIM)[None, :], attn_out_norm[0][None, :],
        ssm_out_norm[0][None, :], expand)
    return _ffn_call(mix, x, mod, w_o[0].astype(BF16), norm2[0][None, :], w_gate_up[0].astype(BF16),
                     w_down[0].astype(BF16), final_norm[None, :])
```

```python
import numpy as np
import jax
import jax.numpy as jnp
from jax import lax
from jax.experimental import pallas as pl
from jax.experimental.pallas import tpu as pltpu

F32 = jnp.float32
BF16 = jnp.bfloat16

D_MODEL = 1024
HEAD_DIM = 64
N_Q_HEADS = 8
N_KV_HEADS = 2
Q_PER_KV = N_Q_HEADS // N_KV_HEADS
ATTN_WIDTH = N_Q_HEADS * HEAD_DIM
KV_WIDTH = N_KV_HEADS * HEAD_DIM
WINDOW = 128
BLOCK = 128
N_BUCKETS = 32
MAX_DISTANCE = 128

SSM_HEAD_DIM = 64
SSM_HEADS = 8
SSM_GROUPS = 2
HEADS_PER_GROUP = SSM_HEADS // SSM_GROUPS
SSM_WIDTH = SSM_HEADS * SSM_HEAD_DIM
GROUP_WIDTH = SSM_WIDTH // SSM_GROUPS
D_STATE = 128
CONV_K = 4
CHUNK = 128
XBC_WIDTH = SSM_WIDTH + 2 * SSM_GROUPS * D_STATE
D_FF = 2816
EPS = 1e-6

LANES = 128
SUBLANES = 8
VMEM_LIMIT_BYTES = 56 * 1024 * 1024

ADA_TN = 1536
INPROJ_TM = 512
FFN_TM = 512
FFN_FC = 256
MIX_CHUNKS = 4
MIX_ROWS = MIX_CHUNKS * CHUNK

Q_END = ATTN_WIDTH
K_END = Q_END + KV_WIDTH
V_END = K_END + KV_WIDTH
Z_END = V_END + SSM_WIDTH
XBC_END = Z_END + XBC_WIDTH

assert BLOCK == CHUNK == LANES and KV_WIDTH == LANES and 2 * HEAD_DIM == LANES


def _silu(x):
    return x * jax.nn.sigmoid(x)


def _rms(x):
    return x * lax.rsqrt(jnp.mean(x * x, axis=-1, keepdims=True) + EPS)


def _split2(x):
    hi = x.astype(BF16)
    lo = (x - hi.astype(F32)).astype(BF16)
    return hi, lo


def _dot(a, b):
    return jnp.dot(a, b, preferred_element_type=F32)


def _dot_nt(a, b):
    return lax.dot_general(a, b, (((1,), (1,)), ((), ())), preferred_element_type=F32)


def _dot_tn(a, b):
    return lax.dot_general(a, b, (((0,), (0,)), ((), ())), preferred_element_type=F32)


def _ada_kernel(c_ref, w_ref, b_ref, o_ref):
    cond = _silu(c_ref[...])
    o_ref[...] = _dot(cond.astype(BF16), w_ref[...].astype(BF16)) + b_ref[...]


def _ada_call(c_pad, ada_w, ada_b):
    n = ada_w.shape[1]
    return pl.pallas_call(
        _ada_kernel,
        grid=(n // ADA_TN,),
        in_specs=[
            pl.BlockSpec((SUBLANES, D_MODEL), lambda j: (0, 0)),
            pl.BlockSpec((D_MODEL, ADA_TN), lambda j: (0, j)),
            pl.BlockSpec((1, ADA_TN), lambda j: (0, j)),
        ],
        out_specs=pl.BlockSpec((SUBLANES, ADA_TN), lambda j: (0, j)),
        out_shape=jax.ShapeDtypeStruct((SUBLANES, n), F32),
        compiler_params=pltpu.CompilerParams(dimension_semantics=("parallel",)),
        name="ada_mod",
    )(c_pad, ada_w, ada_b)


def _t5_bucket_table():
    i = np.arange(BLOCK)[:, None]
    j = np.arange(BLOCK)[None, :]
    n = np.where(j <= i, i - j, i + BLOCK - j)
    assert WINDOW == BLOCK and n.min() >= 0 and n.max() < WINDOW
    max_exact = N_BUCKETS // 2
    large = max_exact + (np.log(np.maximum(n, 1) / max_exact) / np.log(MAX_DISTANCE / max_exact)
                         * (N_BUCKETS - max_exact)).astype(np.int32)
    large = np.minimum(large, N_BUCKETS - 1)
    bucket = np.where(n < max_exact, n, large).astype(np.int32)
    first = np.where(j <= i, bucket, -1)
    return np.stack([first, bucket]).astype(np.int32)


def _bias_kernel(idx_ref, rb_ref, o_ref):
    for v in range(2):
        idx = idx_ref[v]
        hits = [idx == b for b in range(N_BUCKETS)]
        for h in range(N_Q_HEADS):
            acc = jnp.full((BLOCK, BLOCK), -jnp.inf, F32)
            for b in range(N_BUCKETS):
                acc = jnp.where(hits[b], rb_ref[b, h], acc)
            o_ref[v, h] = acc


def _bias_call(rel_bias):
    idx = jnp.asarray(_t5_bucket_table())
    return pl.pallas_call(
        _bias_kernel,
        in_specs=[
            pl.BlockSpec(memory_space=pltpu.VMEM),
            pl.BlockSpec(memory_space=pltpu.SMEM),
        ],
        out_specs=pl.BlockSpec(memory_space=pltpu.VMEM),
        out_shape=jax.ShapeDtypeStruct((2, N_Q_HEADS, BLOCK, BLOCK), F32),
        name="t5_bias",
    )(idx, rel_bias)


def _inproj_kernel(x_ref, mod_ref, n1_ref, w_ref, wdt_ref, q_ref, kt_ref, v_ref, z_ref, xbc_ref, dt_ref):
    h = _rms(x_ref[...]) * n1_ref[...]
    h = h * (1.0 + mod_ref[1:2, :]) + mod_ref[0:1, :]
    hb = h.astype(BF16)
    q_ref[...] = _dot(hb, w_ref[:, 0:Q_END]).astype(BF16)
    kt_ref[...] = _dot(hb, w_ref[:, Q_END:K_END]).T.astype(BF16)
    v_ref[...] = _dot(hb, w_ref[:, K_END:V_END]).astype(BF16)
    z_ref[...] = _dot(hb, w_ref[:, V_END:Z_END]).astype(BF16)
    xbc_ref[...] = _dot(hb, w_ref[:, Z_END:XBC_END]).astype(BF16)
    dt_ref[...] = _dot(hb, wdt_ref[...])


def _inproj_call(x, mod, norm1, w_main, w_dt):
    b, s, _ = x.shape
    tm = INPROJ_TM
    row = lambda width: pl.BlockSpec((None, tm, width), lambda i, j: (i, j, 0))
    const = lambda shape: pl.BlockSpec(shape, lambda i, j: (0,) * len(shape))
    return pl.pallas_call(
        _inproj_kernel,
        grid=(b, s // tm),
        in_specs=[
            row(D_MODEL),
            pl.BlockSpec((None, 6, D_MODEL), lambda i, j: (i, 0, 0)),
            const((1, D_MODEL)),
            const((D_MODEL, XBC_END)),
            const((D_MODEL, LANES)),
        ],
        out_specs=[row(ATTN_WIDTH), pl.BlockSpec((None, KV_WIDTH, tm), lambda i, j: (i, 0, j)), row(KV_WIDTH),
                   row(SSM_WIDTH), row(XBC_WIDTH), row(LANES)],
        out_shape=[
            jax.ShapeDtypeStruct((b, s, ATTN_WIDTH), BF16),
            jax.ShapeDtypeStruct((b, KV_WIDTH, s), BF16),
            jax.ShapeDtypeStruct((b, s, KV_WIDTH), BF16),
            jax.ShapeDtypeStruct((b, s, SSM_WIDTH), BF16),
            jax.ShapeDtypeStruct((b, s, XBC_WIDTH), BF16),
            jax.ShapeDtypeStruct((b, s, LANES), F32),
        ],
        compiler_params=pltpu.CompilerParams(
            dimension_semantics=("parallel", "parallel"), vmem_limit_bytes=VMEM_LIMIT_BYTES),
        name="in_proj",
    )(x, mod, norm1, w_main, w_dt)


def _stage_keys_values(kt_ref, ktp_ref, v_ref, vp_ref, ka_ref, vv_ref):
    zeros = jnp.zeros((HEAD_DIM, BLOCK + MIX_ROWS), BF16)
    for g in range(N_KV_HEADS):
        rows = slice(g * HEAD_DIM, (g + 1) * HEAD_DIM)
        for par in range(2):
            live = slice(par * HEAD_DIM, (par + 1) * HEAD_DIM)
            dead = slice((1 - par) * HEAD_DIM, (2 - par) * HEAD_DIM)
            ka_ref[g, par, live, 0:BLOCK] = ktp_ref[rows, :]
            ka_ref[g, par, live, BLOCK:] = kt_ref[rows, :]
            ka_ref[g, par, dead, :] = zeros

    def duplicate_halves(v):
        v32 = pltpu.bitcast(v, jnp.uint32)
        swapped = pltpu.roll(v32, HEAD_DIM, axis=1)
        lane = lax.broadcasted_iota(jnp.int32, v32.shape, 1)
        first = lane < HEAD_DIM
        return (pltpu.bitcast(jnp.where(first, v32, swapped), BF16),
                pltpu.bitcast(jnp.where(first, swapped, v32), BF16))

    for src, r0, r1 in ((vp_ref, 0, BLOCK), (v_ref, BLOCK, BLOCK + MIX_ROWS)):
        for g, dup in enumerate(duplicate_halves(src[...])):
            vv_ref[g, r0:r1, 0:LANES] = dup


def _scores(q_ref, ka_ref, j):
    rows = pl.ds(j * BLOCK, BLOCK)
    win = pl.ds(j * BLOCK, 2 * BLOCK)
    out = []
    for h in range(N_Q_HEADS):
        t, par = h // 2, h % 2
        out.append(_dot(q_ref[rows, t * LANES:(t + 1) * LANES], ka_ref[t // 2, par, :, win]))
    return out


def _softmax(scores, bias_ref, sinks_ref, own):
    s = [jnp.where(own, sc[:, BLOCK:], sc[:, :BLOCK]) + bias_ref[h] for h, sc in enumerate(scores)]
    m = [jnp.maximum(jnp.max(s[h], axis=-1, keepdims=True), sinks_ref[h]) for h in range(N_Q_HEADS)]
    p = [jnp.exp(s[h] - m[h]) for h in range(N_Q_HEADS)]
    pcat = [jnp.concatenate([jnp.where(own, 0.0, ph), jnp.where(own, ph, 0.0)], axis=1).astype(BF16) for ph in p]
    sink_term = [jnp.exp(sinks_ref[h] - m[h]) for h in range(N_Q_HEADS)]
    return pcat, sink_term


def _attend(pcat, sink_term, vv_ref, an_ref, o_ref, j):
    win = pl.ds(j * BLOCK, 2 * BLOCK)
    rows = pl.ds(j * BLOCK, BLOCK)
    o2 = [_dot(pcat[h], vv_ref[h // Q_PER_KV, win, :]) for h in range(N_Q_HEADS)]
    heads = [o2[h][:, :LANES] / (o2[h][:, LANES:] + sink_term[h]) for h in range(N_Q_HEADS)]
    lane = lax.broadcasted_iota(jnp.int32, (BLOCK, LANES), 1)
    tiles = [jnp.where(lane < HEAD_DIM, heads[2 * t], heads[2 * t + 1]) for t in range(N_Q_HEADS // 2)]
    ssq = sum(jnp.sum(tl * tl, axis=-1, keepdims=True) for tl in tiles)
    inv = lax.rsqrt(ssq * (1.0 / ATTN_WIDTH) + EPS)
    for t, tl in enumerate(tiles):
        cols = slice(t * LANES, (t + 1) * LANES)
        o_ref[rows, cols] = (tl * inv * an_ref[:, cols]).astype(BF16)


def _expand_heads(vals, expand):
    hi, lo = _split2(vals)
    return _dot(hi, expand) + _dot(lo, expand)


def _mixer_kernel(q_ref, kt_ref, ktp_ref, v_ref, vp_ref, z_ref, xbc_ref, dt_ref, bias_ref, sinks_ref, cw_ref,
                  cb_ref, dtb_ref, alog_ref, dskip_ref, an_ref, sn_ref, expand_ref, o_ref,
                  xe_ref, state_ref, ka_ref, vv_ref):
    step = pl.program_id(1)
    nch = MIX_CHUNKS

    @pl.when(step == 0)
    def _():
        xe_ref[0:SUBLANES, :] = jnp.zeros((SUBLANES, XBC_WIDTH), F32)
        state_ref[...] = jnp.zeros_like(state_ref)
        vv_ref[:, :, LANES:] = jnp.ones((N_KV_HEADS, BLOCK + MIX_ROWS, LANES), BF16)

    row_i = lax.broadcasted_iota(jnp.int32, (BLOCK, BLOCK), 0)
    col_i = lax.broadcasted_iota(jnp.int32, (BLOCK, BLOCK), 1)
    own = col_i <= row_i
    bias_of = lambda j: bias_ref.at[jnp.minimum(step, 1)] if j == 0 else bias_ref.at[1]

    _stage_keys_values(kt_ref, ktp_ref, v_ref, vp_ref, ka_ref, vv_ref)
    scores = {0: _scores(q_ref, ka_ref, 0), 1: _scores(q_ref, ka_ref, 1)}

    xbc = xbc_ref[...].astype(F32)
    xe_ref[SUBLANES:, :] = xbc
    conv = cw_ref[3:4, :] * xbc + cb_ref[...]
    for k in range(CONV_K - 1):
        conv = conv + cw_ref[k:k + 1, :] * xe_ref[pl.ds(SUBLANES - (CONV_K - 1) + k, MIX_ROWS), :]
    xe_ref[0:SUBLANES, :] = xbc[MIX_ROWS - SUBLANES:, :]
    act = _silu(conv)
    xs = act[:, :SSM_WIDTH]
    bc_b = act[:, SSM_WIDTH:].astype(BF16)

    dt = jax.nn.softplus(dt_ref[...] + dtb_ref[...])
    dta = dt * (-jnp.exp(alog_ref[...]))
    tri = own.astype(BF16)
    dta_hi, dta_lo = _split2(dta)
    acs = [_dot(tri, dta_hi[j * CHUNK:(j + 1) * CHUNK]) + _dot(tri, dta_lo[j * CHUNK:(j + 1) * CHUNK])
           for j in range(nch)]

    pcat, sink_term = _softmax(scores.pop(0), bias_of(0), sinks_ref, own)
    _attend(pcat, sink_term, vv_ref, an_ref, o_ref, 0)

    acs_t = [a.T for a in acs]
    dec_in = jnp.concatenate([jnp.exp(a) for a in acs], axis=0)
    dec_out = jnp.concatenate([jnp.exp(a[CHUNK - 1:CHUNK, :] - a) for a in acs], axis=0)
    expand = expand_ref[...]
    dt_x = _expand_heads(dt, expand)
    decay_in_x = _expand_heads(dec_in, expand)
    decay_out_x = _expand_heads(dec_out, expand)

    scores[2] = _scores(q_ref, ka_ref, 2)
    pcat, sink_term = _softmax(scores.pop(1), bias_of(1), sinks_ref, own)
    _attend(pcat, sink_term, vv_ref, an_ref, o_ref, 1)

    xdt = xs * dt_x
    xdt_b = xdt.astype(BF16)
    xdec_b = (xdt * decay_out_x).astype(BF16)
    bm = [[bc_b[j * CHUNK:(j + 1) * CHUNK, g * D_STATE:(g + 1) * D_STATE] for g in range(SSM_GROUPS)]
          for j in range(nch)]
    cm = [[bc_b[j * CHUNK:(j + 1) * CHUNK, (SSM_GROUPS + g) * D_STATE:(SSM_GROUPS + g + 1) * D_STATE]
           for g in range(SSM_GROUPS)] for j in range(nch)]
    grp = lambda a, j, g: a[j * CHUNK:(j + 1) * CHUNK, g * GROUP_WIDTH:(g + 1) * GROUP_WIDTH]
    cbm = [[_dot_nt(cm[j][g], bm[j][g]) for g in range(SSM_GROUPS)] for j in range(nch)]
    d_state = [[_dot_tn(bm[j][g], grp(xdec_b, j, g)) for g in range(SSM_GROUPS)] for j in range(nch)]

    scores[3] = _scores(q_ref, ka_ref, 3)
    pcat, sink_term = _softmax(scores.pop(2), bias_of(2), sinks_ref, own)
    _attend(pcat, sink_term, vv_ref, an_ref, o_ref, 2)

    lane = lax.broadcasted_iota(jnp.int32, (CHUNK, GROUP_WIDTH), 1)
    y_diag = []
    for j in range(nch):
        per_group = []
        for g in range(SSM_GROUPS):
            y_g = None
            for r in range(HEADS_PER_GROUP):
                h = g * HEADS_PER_GROUP + r
                seg = acs[j][:, h:h + 1] - acs_t[j][h:h + 1, :]
                lmat = jnp.exp(jnp.where(own, seg, -jnp.inf))
                y_r = _dot((cbm[j][g] * lmat).astype(BF16), grp(xdt_b, j, g))
                if y_g is None:
                    y_g = y_r
                else:
                    y_g = jnp.where(lane >= r * SSM_HEAD_DIM, y_r, y_g)
            per_group.append(y_g)
        y_diag.append(per_group)

    pcat, sink_term = _softmax(scores.pop(3), bias_of(3), sinks_ref, own)
    _attend(pcat, sink_term, vv_ref, an_ref, o_ref, 3)

    y_rows = []
    for j in range(nch):
        per_group = []
        for g in range(SSM_GROUPS):
            prev = state_ref[g]
            y_off = _dot(cm[j][g], prev.astype(BF16)) * grp(decay_in_x, j, g)
            chunk_decay = decay_in_x[(j + 1) * CHUNK - 1:(j + 1) * CHUNK, g * GROUP_WIDTH:(g + 1) * GROUP_WIDTH]
            state_ref[g] = prev * chunk_decay + d_state[j][g]
            per_group.append(y_diag[j][g] + y_off)
        y_rows.append(jnp.concatenate(per_group, axis=-1))
    y = jnp.concatenate(y_rows, axis=0) + dskip_ref[...] * xs
    y = y * _silu(z_ref[...].astype(F32))
    y = jnp.concatenate([_rms(y[:, :GROUP_WIDTH]), _rms(y[:, GROUP_WIDTH:])], axis=-1)
    o_ref[:, ATTN_WIDTH:] = (y * sn_ref[...]).astype(BF16)


def _mixer_call(q, kt, v, z, xbc, dt, bias, sinks, conv_w, conv_b, dt_bias, a_log, d_skip_x, attn_norm,
                ssm_norm, expand):
    b, s, _ = q.shape
    rows = MIX_ROWS
    row = lambda width: pl.BlockSpec((None, rows, width), lambda i, c: (i, c, 0))
    const = lambda shape: pl.BlockSpec(shape, lambda i, c: (0,) * len(shape))
    prev_blk = lambda i, c: jnp.maximum(c * MIX_CHUNKS - 1, 0)
    return pl.pallas_call(
        _mixer_kernel,
        grid=(b, s // rows),
        in_specs=[
            row(ATTN_WIDTH),
            pl.BlockSpec((None, KV_WIDTH, rows), lambda i, c: (i, 0, c)),
            pl.BlockSpec((None, KV_WIDTH, BLOCK), lambda i, c: (i, 0, prev_blk(i, c))),
            row(KV_WIDTH),
            pl.BlockSpec((None, BLOCK, KV_WIDTH), lambda i, c: (i, prev_blk(i, c), 0)),
            row(SSM_WIDTH), row(XBC_WIDTH), row(LANES),
            const((2, N_Q_HEADS, BLOCK, BLOCK)),
            pl.BlockSpec(memory_space=pltpu.SMEM),
            const((CONV_K, XBC_WIDTH)), const((1, XBC_WIDTH)), const((1, LANES)), const((1, LANES)),
            const((1, SSM_WIDTH)), const((1, ATTN_WIDTH)), const((1, SSM_WIDTH)),
            const((LANES, SSM_WIDTH)),
        ],
        out_specs=row(ATTN_WIDTH + SSM_WIDTH),
        out_shape=jax.ShapeDtypeStruct((b, s, ATTN_WIDTH + SSM_WIDTH), BF16),
        scratch_shapes=[
            pltpu.VMEM((SUBLANES + rows, XBC_WIDTH), F32),
            pltpu.VMEM((SSM_GROUPS, D_STATE, GROUP_WIDTH), F32),
            pltpu.VMEM((N_KV_HEADS, 2, KV_WIDTH, BLOCK + rows), BF16),
            pltpu.VMEM((N_KV_HEADS, BLOCK + rows, 2 * LANES), BF16),
        ],
        compiler_params=pltpu.CompilerParams(
            dimension_semantics=("parallel", "arbitrary"), vmem_limit_bytes=VMEM_LIMIT_BYTES),
        name="mixer",
    )(q, kt, kt, v, v, z, xbc, dt, bias, sinks, conv_w, conv_b, dt_bias, a_log, d_skip_x, attn_norm, ssm_norm,
      expand)


def _ffn_kernel(mix_ref, x_ref, mod_ref, wo_ref, n2_ref, wgu_ref, wd_ref, fn_ref, o_ref, h_ref, acc_ref):
    x1 = x_ref[...] + mod_ref[2:3, :] * _dot(mix_ref[...], wo_ref[...])
    h = _rms(x1) * n2_ref[...]
    h_ref[...] = (h * (1.0 + mod_ref[4:5, :]) + mod_ref[3:4, :]).astype(BF16)
    o_ref[...] = x1
    acc_ref[...] = jnp.zeros_like(acc_ref)

    def body(j, carry):
        c0 = pl.multiple_of(j * FFN_FC, FFN_FC)
        hb = h_ref[...]
        g = _dot(hb, wgu_ref[:, pl.ds(c0, FFN_FC)])
        u = _dot(hb, wgu_ref[:, pl.ds(D_FF + c0, FFN_FC)])
        a = (_silu(g) * u).astype(BF16)
        acc_ref[...] += _dot(a, wd_ref[pl.ds(c0, FFN_FC), :])
        return carry

    lax.fori_loop(0, D_FF // FFN_FC, body, 0)
    o_ref[...] = _rms(o_ref[...] + mod_ref[5:6, :] * acc_ref[...]) * fn_ref[...]


def _ffn_call(mix, x, mod, w_o, norm2, w_gu, w_down, final_norm):
    b, s, _ = x.shape
    tm = FFN_TM
    row = lambda: pl.BlockSpec((None, tm, D_MODEL), lambda i, j: (i, j, 0))
    const = lambda shape: pl.BlockSpec(shape, lambda i, j: (0,) * len(shape), pipeline_mode=pl.Buffered(1))
    return pl.pallas_call(
        _ffn_kernel,
        grid=(b, s // tm),
        in_specs=[
            row(), row(),
            pl.BlockSpec((None, 6, D_MODEL), lambda i, j: (i, 0, 0)),
            const((D_MODEL, D_MODEL)), const((1, D_MODEL)),
            const((D_MODEL, 2 * D_FF)), const((D_FF, D_MODEL)), const((1, D_MODEL)),
        ],
        out_specs=row(),
        out_shape=jax.ShapeDtypeStruct((b, s, D_MODEL), F32),
        scratch_shapes=[pltpu.VMEM((tm, D_MODEL), BF16), pltpu.VMEM((tm, D_MODEL), F32)],
        compiler_params=pltpu.CompilerParams(
            dimension_semantics=("parallel", "parallel"), vmem_limit_bytes=VMEM_LIMIT_BYTES),
        name="ffn",
    )(mix, x, mod, w_o, norm2, w_gu, w_down, final_norm)


def _head_expand_matrix():
    e = np.zeros((LANES, SSM_WIDTH), np.float32)
    for h in range(SSM_HEADS):
        e[h, h * SSM_HEAD_DIM:(h + 1) * SSM_HEAD_DIM] = 1.0
    return e


def kernel(x, c, ada_w, ada_b, norm1, w_in, conv_w, conv_b, dt_bias, A_log, D_skip, sinks, attn_out_norm,
           ssm_out_norm, w_o, norm2, w_gate_up, w_down, rel_bias, final_norm):
    assert ada_w.shape[0] == 1, "the final RMSNorm is fused into the (single) layer's FFN call"
    b = x.shape[0]
    bias = _bias_call(rel_bias)
    expand = jnp.asarray(_head_expand_matrix(), BF16)
    pad_heads = lambda v: jnp.pad(v, (0, LANES - SSM_HEADS))[None, :]
    c_pad = jnp.pad(c, ((0, SUBLANES - b), (0, 0)))
    mod = _ada_call(c_pad, ada_w[0], ada_b[0][None, :])[:b].reshape(b, 6, D_MODEL)
    w_q = w_in[0][:, :Q_END] * (HEAD_DIM ** -0.5)
    w_main = jnp.concatenate([w_q, w_in[0][:, Q_END:XBC_END]], axis=1).astype(BF16)
    w_dt = jnp.pad(w_in[0][:, XBC_END:], ((0, 0), (0, LANES - SSM_HEADS))).astype(BF16)
    q, kt, v, z, xbc, dt = _inproj_call(x, mod, norm1[0][None, :], w_main, w_dt)
    mix = _mixer_call(
        q, kt, v, z, xbc, dt, bias, sinks[0], conv_w[0], conv_b[0][None, :], pad_heads(dt_bias[0]),
        pad_heads(A_log[0]), jnp.repeat(D_skip[0], SSM_HEAD_DIM)[None, :], attn_out_norm[0][None, :],
        ssm_out_norm[0][None, :], expand)
    return _ffn_call(mix, x, mod, w_o[0].astype(BF16), norm2[0][None, :], w_gate_up[0].astype(BF16),
                     w_down[0].astype(BF16), final_norm[None, :])
```

```python
import numpy as np
import jax
import jax.numpy as jnp
from jax import lax
from jax.experimental import pallas as pl
from jax.experimental.pallas import tpu as pltpu

F32 = jnp.float32
BF16 = jnp.bfloat16

D_MODEL = 1024
HEAD_DIM = 64
N_Q_HEADS = 8
N_KV_HEADS = 2
Q_PER_KV = N_Q_HEADS // N_KV_HEADS
ATTN_WIDTH = N_Q_HEADS * HEAD_DIM
KV_WIDTH = N_KV_HEADS * HEAD_DIM
WINDOW = 128
BLOCK = 128
N_BUCKETS = 32
MAX_DISTANCE = 128

SSM_HEAD_DIM = 64
SSM_HEADS = 8
SSM_GROUPS = 2
HEADS_PER_GROUP = SSM_HEADS // SSM_GROUPS
SSM_WIDTH = SSM_HEADS * SSM_HEAD_DIM
GROUP_WIDTH = SSM_WIDTH // SSM_GROUPS
D_STATE = 128
CONV_K = 4
CHUNK = 128
XBC_WIDTH = SSM_WIDTH + 2 * SSM_GROUPS * D_STATE
D_FF = 2816
EPS = 1e-6

LANES = 128
SUBLANES = 8
VMEM_LIMIT_BYTES = 56 * 1024 * 1024

ADA_TN = 1536
INPROJ_TM = 512
FFN_TM = 512
FFN_FC = 256
MIX_CHUNKS = 4
MIX_ROWS = MIX_CHUNKS * CHUNK

Q_END = ATTN_WIDTH
K_END = Q_END + KV_WIDTH
V_END = K_END + KV_WIDTH
Z_END = V_END + SSM_WIDTH
XBC_END = Z_END + XBC_WIDTH

assert BLOCK == CHUNK == LANES and KV_WIDTH == LANES and 2 * HEAD_DIM == LANES


def _silu(x):
    return x * jax.nn.sigmoid(x)


def _rms(x):
    return x * lax.rsqrt(jnp.mean(x * x, axis=-1, keepdims=True) + EPS)


def _split2(x):
    hi = x.astype(BF16)
    lo = (x - hi.astype(F32)).astype(BF16)
    return hi, lo


def _dot(a, b):
    return jnp.dot(a, b, preferred_element_type=F32)


def _dot_nt(a, b):
    return lax.dot_general(a, b, (((1,), (1,)), ((), ())), preferred_element_type=F32)


def _dot_tn(a, b):
    return lax.dot_general(a, b, (((0,), (0,)), ((), ())), preferred_element_type=F32)


def _ada_kernel(c_ref, w_ref, b_ref, o_ref):
    cond = _silu(c_ref[...])
    o_ref[...] = _dot(cond.astype(BF16), w_ref[...].astype(BF16)) + b_ref[...]


def _ada_call(c_pad, ada_w, ada_b):
    n = ada_w.shape[1]
    return pl.pallas_call(
        _ada_kernel,
        grid=(n // ADA_TN,),
        in_specs=[
            pl.BlockSpec((SUBLANES, D_MODEL), lambda j: (0, 0)),
            pl.BlockSpec((D_MODEL, ADA_TN), lambda j: (0, j)),
            pl.BlockSpec((1, ADA_TN), lambda j: (0, j)),
        ],
        out_specs=pl.BlockSpec((SUBLANES, ADA_TN), lambda j: (0, j)),
        out_shape=jax.ShapeDtypeStruct((SUBLANES, n), F32),
        compiler_params=pltpu.CompilerParams(dimension_semantics=("parallel",)),
        name="ada_mod",
    )(c_pad, ada_w, ada_b)


def _t5_bucket_table():
    i = np.arange(BLOCK)[:, None]
    j = np.arange(BLOCK)[None, :]
    n = np.where(j <= i, i - j, i + BLOCK - j)
    assert WINDOW == BLOCK and n.min() >= 0 and n.max() < WINDOW
    max_exact = N_BUCKETS // 2
    large = max_exact + (np.log(np.maximum(n, 1) / max_exact) / np.log(MAX_DISTANCE / max_exact)
                         * (N_BUCKETS - max_exact)).astype(np.int32)
    large = np.minimum(large, N_BUCKETS - 1)
    bucket = np.where(n < max_exact, n, large).astype(np.int32)
    first = np.where(j <= i, bucket, -1)
    return np.stack([first, bucket]).astype(np.int32)


def _bias_kernel(idx_ref, rb_ref, o_ref):
    for v in range(2):
        idx = idx_ref[v]
        hits = [idx == b for b in range(N_BUCKETS)]
        for h in range(N_Q_HEADS):
            acc = jnp.full((BLOCK, BLOCK), -jnp.inf, F32)
            for b in range(N_BUCKETS):
                acc = jnp.where(hits[b], rb_ref[b, h], acc)
            o_ref[v, h] = acc


def _bias_call(rel_bias):
    idx = jnp.asarray(_t5_bucket_table())
    return pl.pallas_call(
        _bias_kernel,
        in_specs=[
            pl.BlockSpec(memory_space=pltpu.VMEM),
            pl.BlockSpec(memory_space=pltpu.SMEM),
        ],
        out_specs=pl.BlockSpec(memory_space=pltpu.VMEM),
        out_shape=jax.ShapeDtypeStruct((2, N_Q_HEADS, BLOCK, BLOCK), F32),
        name="t5_bias",
    )(idx, rel_bias)


def _inproj_kernel(x_ref, mod_ref, n1_ref, w_ref, wdt_ref, q_ref, kt_ref, v_ref, z_ref, xbc_ref, dt_ref):
    h = _rms(x_ref[...]) * n1_ref[...]
    h = h * (1.0 + mod_ref[1:2, :]) + mod_ref[0:1, :]
    hb = h.astype(BF16)
    q_ref[...] = _dot(hb, w_ref[:, 0:Q_END]).astype(BF16)
    kt_ref[...] = _dot(hb, w_ref[:, Q_END:K_END]).T.astype(BF16)
    v_ref[...] = _dot(hb, w_ref[:, K_END:V_END]).astype(BF16)
    z_ref[...] = _dot(hb, w_ref[:, V_END:Z_END]).astype(BF16)
    xbc_ref[...] = _dot(hb, w_ref[:, Z_END:XBC_END]).astype(BF16)
    dt_ref[...] = _dot(hb, wdt_ref[...])


def _inproj_call(x, mod, norm1, w_main, w_dt):
    b, s, _ = x.shape
    tm = INPROJ_TM
    row = lambda width: pl.BlockSpec((None, tm, width), lambda i, j: (i, j, 0))
    const = lambda shape: pl.BlockSpec(shape, lambda i, j: (0,) * len(shape))
    return pl.pallas_call(
        _inproj_kernel,
        grid=(b, s // tm),
        in_specs=[
            row(D_MODEL),
            pl.BlockSpec((None, 6, D_MODEL), lambda i, j: (i, 0, 0)),
            const((1, D_MODEL)),
            const((D_MODEL, XBC_END)),
            const((D_MODEL, LANES)),
        ],
        out_specs=[row(ATTN_WIDTH), pl.BlockSpec((None, KV_WIDTH, tm), lambda i, j: (i, 0, j)), row(KV_WIDTH),
                   row(SSM_WIDTH), row(XBC_WIDTH), row(LANES)],
        out_shape=[
            jax.ShapeDtypeStruct((b, s, ATTN_WIDTH), BF16),
            jax.ShapeDtypeStruct((b, KV_WIDTH, s), BF16),
            jax.ShapeDtypeStruct((b, s, KV_WIDTH), BF16),
            jax.ShapeDtypeStruct((b, s, SSM_WIDTH), BF16),
            jax.ShapeDtypeStruct((b, s, XBC_WIDTH), BF16),
            jax.ShapeDtypeStruct((b, s, LANES), F32),
        ],
        compiler_params=pltpu.CompilerParams(
            dimension_semantics=("parallel", "parallel"), vmem_limit_bytes=VMEM_LIMIT_BYTES),
        name="in_proj",
    )(x, mod, norm1, w_main, w_dt)


def _stage_keys_values(kt_ref, ktp_ref, v_ref, vp_ref, ka_ref, vv_ref):
    zeros = jnp.zeros((HEAD_DIM, BLOCK + MIX_ROWS), BF16)
    for g in range(N_KV_HEADS):
        rows = slice(g * HEAD_DIM, (g + 1) * HEAD_DIM)
        for par in range(2):
            live = slice(par * HEAD_DIM, (par + 1) * HEAD_DIM)
            dead = slice((1 - par) * HEAD_DIM, (2 - par) * HEAD_DIM)
            ka_ref[g, par, live, 0:BLOCK] = ktp_ref[rows, :]
            ka_ref[g, par, live, BLOCK:] = kt_ref[rows, :]
            ka_ref[g, par, dead, :] = zeros

    def duplicate_halves(v):
        v32 = pltpu.bitcast(v, jnp.uint32)
        swapped = pltpu.roll(v32, HEAD_DIM, axis=1)
        lane = lax.broadcasted_iota(jnp.int32, v32.shape, 1)
        first = lane < HEAD_DIM
        return (pltpu.bitcast(jnp.where(first, v32, swapped), BF16),
                pltpu.bitcast(jnp.where(first, swapped, v32), BF16))

    for src, r0, r1 in ((vp_ref, 0, BLOCK), (v_ref, BLOCK, BLOCK + MIX_ROWS)):
        for g, dup in enumerate(duplicate_halves(src[...])):
            vv_ref[g, r0:r1, 0:LANES] = dup


def _scores(q_ref, ka_ref, j):
    rows = pl.ds(j * BLOCK, BLOCK)
    win = pl.ds(j * BLOCK, 2 * BLOCK)
    out = []
    for h in range(N_Q_HEADS):
        t, par = h // 2, h % 2
        out.append(_dot(q_ref[rows, t * LANES:(t + 1) * LANES], ka_ref[t // 2, par, :, win]))
    return out


def _softmax(scores, bias_ref, sinks_ref, own):
    s = [jnp.where(own, sc[:, BLOCK:], sc[:, :BLOCK]) + bias_ref[h] for h, sc in enumerate(scores)]
    m = [jnp.maximum(jnp.max(s[h], axis=-1, keepdims=True), sinks_ref[h]) for h in range(N_Q_HEADS)]
    p = [jnp.exp(s[h] - m[h]) for h in range(N_Q_HEADS)]
    pcat = [jnp.concatenate([jnp.where(own, 0.0, ph), jnp.where(own, ph, 0.0)], axis=1).astype(BF16) for ph in p]
    sink_term = [jnp.exp(sinks_ref[h] - m[h]) for h in range(N_Q_HEADS)]
    return pcat, sink_term


def _attend(pcat, sink_term, vv_ref, an_ref, o_ref, j):
    win = pl.ds(j * BLOCK, 2 * BLOCK)
    rows = pl.ds(j * BLOCK, BLOCK)
    o2 = [_dot(pcat[h], vv_ref[h // Q_PER_KV, win, :]) for h in range(N_Q_HEADS)]
    heads = [o2[h][:, :LANES] / (o2[h][:, LANES:] + sink_term[h]) for h in range(N_Q_HEADS)]
    lane = lax.broadcasted_iota(jnp.int32, (BLOCK, LANES), 1)
    tiles = [jnp.where(lane < HEAD_DIM, heads[2 * t], heads[2 * t + 1]) for t in range(N_Q_HEADS // 2)]
    ssq = sum(jnp.sum(tl * tl, axis=-1, keepdims=True) for tl in tiles)
    inv = lax.rsqrt(ssq * (1.0 / ATTN_WIDTH) + EPS)
    for t, tl in enumerate(tiles):
        cols = slice(t * LANES, (t + 1) * LANES)
        o_ref[rows, cols] = (tl * inv * an_ref[:, cols]).astype(BF16)


def _expand_heads(vals, expand):
    hi, lo = _split2(vals)
    return _dot(hi, expand) + _dot(lo, expand)


def _mixer_kernel(q_ref, kt_ref, ktp_ref, v_ref, vp_ref, z_ref, xbc_ref, dt_ref, bias_ref, sinks_ref, cw_ref,
                  cb_ref, dtb_ref, alog_ref, dskip_ref, an_ref, sn_ref, expand_ref, o_ref,
                  xe_ref, state_ref, ka_ref, vv_ref):
    step = pl.program_id(1)
    nch = MIX_CHUNKS

    @pl.when(step == 0)
    def _():
        xe_ref[0:SUBLANES, :] = jnp.zeros((SUBLANES, XBC_WIDTH), F32)
        state_ref[...] = jnp.zeros_like(state_ref)
        vv_ref[:, :, LANES:] = jnp.ones((N_KV_HEADS, BLOCK + MIX_ROWS, LANES), BF16)

    row_i = lax.broadcasted_iota(jnp.int32, (BLOCK, BLOCK), 0)
    col_i = lax.broadcasted_iota(jnp.int32, (BLOCK, BLOCK), 1)
    own = col_i <= row_i
    bias_of = lambda j: bias_ref.at[jnp.minimum(step, 1)] if j == 0 else bias_ref.at[1]

    _stage_keys_values(kt_ref, ktp_ref, v_ref, vp_ref, ka_ref, vv_ref)
    scores = {0: _scores(q_ref, ka_ref, 0), 1: _scores(q_ref, ka_ref, 1)}

    xbc = xbc_ref[...].astype(F32)
    xe_ref[SUBLANES:, :] = xbc
    conv = cw_ref[3:4, :] * xbc + cb_ref[...]
    for k in range(CONV_K - 1):
        conv = conv + cw_ref[k:k + 1, :] * xe_ref[pl.ds(SUBLANES - (CONV_K - 1) + k, MIX_ROWS), :]
    xe_ref[0:SUBLANES, :] = xbc[MIX_ROWS - SUBLANES:, :]
    act = _silu(conv)
    xs = act[:, :SSM_WIDTH]
    bc_b = act[:, SSM_WIDTH:].astype(BF16)

    dt = jax.nn.softplus(dt_ref[...] + dtb_ref[...])
    dta = dt * (-jnp.exp(alog_ref[...]))
    tri = own.astype(BF16)
    dta_hi, dta_lo = _split2(dta)
    acs = [_dot(tri, dta_hi[j * CHUNK:(j + 1) * CHUNK]) + _dot(tri, dta_lo[j * CHUNK:(j + 1) * CHUNK])
           for j in range(nch)]

    pcat, sink_term = _softmax(scores.pop(0), bias_of(0), sinks_ref, own)
    _attend(pcat, sink_term, vv_ref, an_ref, o_ref, 0)

    acs_t = [a.T for a in acs]
    dec_in = jnp.concatenate([jnp.exp(a) for a in acs], axis=0)
    dec_out = jnp.concatenate([jnp.exp(a[CHUNK - 1:CHUNK, :] - a) for a in acs], axis=0)
    expand = expand_ref[...]
    dt_x = _expand_heads(dt, expand)
    decay_in_x = _expand_heads(dec_in, expand)
    decay_out_x = _expand_heads(dec_out, expand)

    scores[2] = _scores(q_ref, ka_ref, 2)
    pcat, sink_term = _softmax(scores.pop(1), bias_of(1), sinks_ref, own)
    _attend(pcat, sink_term, vv_ref, an_ref, o_ref, 1)

    xdt = xs * dt_x
    xdt_b = xdt.astype(BF16)
    xdec_b = (xdt * decay_out_x).astype(BF16)
    bm = [[bc_b[j * CHUNK:(j + 1) * CHUNK, g * D_STATE:(g + 1) * D_STATE] for g in range(SSM_GROUPS)]
          for j in range(nch)]
    cm = [[bc_b[j * CHUNK:(j + 1) * CHUNK, (SSM_GROUPS + g) * D_STATE:(SSM_GROUPS + g + 1) * D_STATE]
           for g in range(SSM_GROUPS)] for j in range(nch)]
    grp = lambda a, j, g: a[j * CHUNK:(j + 1) * CHUNK, g * GROUP_WIDTH:(g + 1) * GROUP_WIDTH]
    cbm = [[_dot_nt(cm[j][g], bm[j][g]) for g in range(SSM_GROUPS)] for j in range(nch)]
    d_state = [[_dot_tn(bm[j][g], grp(xdec_b, j, g)) for g in range(SSM_GROUPS)] for j in range(nch)]

    scores[3] = _scores(q_ref, ka_ref, 3)
    pcat, sink_term = _softmax(scores.pop(2), bias_of(2), sinks_ref, own)
    _attend(pcat, sink_term, vv_ref, an_ref, o_ref, 2)

    lane = lax.broadcasted_iota(jnp.int32, (CHUNK, GROUP_WIDTH), 1)
    y_diag = []
    for j in range(nch):
        per_group = []
        for g in range(SSM_GROUPS):
            y_g = None
            for r in range(HEADS_PER_GROUP):
                h = g * HEADS_PER_GROUP + r
                seg = acs[j][:, h:h + 1] - acs_t[j][h:h + 1, :]
                lmat = jnp.exp(jnp.where(own, seg, -jnp.inf))
                y_r = _dot((cbm[j][g] * lmat).astype(BF16), grp(xdt_b, j, g))
                if y_g is None:
                    y_g = y_r
                else:
                    y_g = jnp.where(lane >= r * SSM_HEAD_DIM, y_r, y_g)
            per_group.append(y_g)
        y_diag.append(per_group)

    pcat, sink_term = _softmax(scores.pop(3), bias_of(3), sinks_ref, own)
    _attend(pcat, sink_term, vv_ref, an_ref, o_ref, 3)

    y_rows = []
    for j in range(nch):
        per_group = []
        for g in range(SSM_GROUPS):
            prev = state_ref[g]
            y_off = _dot(cm[j][g], prev.astype(BF16)) * grp(decay_in_x, j, g)
            chunk_decay = decay_in_x[(j + 1) * CHUNK - 1:(j + 1) * CHUNK, g * GROUP_WIDTH:(g + 1) * GROUP_WIDTH]
            state_ref[g] = prev * chunk_decay + d_state[j][g]
            per_group.append(y_diag[j][g] + y_off)
        y_rows.append(jnp.concatenate(per_group, axis=-1))
    y = jnp.concatenate(y_rows, axis=0) + dskip_ref[...] * xs
    y = y * _silu(z_ref[...].astype(F32))
    y = jnp.concatenate([_rms(y[:, :GROUP_WIDTH]), _rms(y[:, GROUP_WIDTH:])], axis=-1)
    o_ref[:, ATTN_WIDTH:] = (y * sn_ref[...]).astype(BF16)


def _mixer_call(q, kt, v, z, xbc, dt, bias, sinks, conv_w, conv_b, dt_bias, a_log, d_skip_x, attn_norm,
                ssm_norm, expand):
    b, s, _ = q.shape
    rows = MIX_ROWS
    row = lambda width: pl.BlockSpec((None, rows, width), lambda i, c: (i, c, 0))
    const = lambda shape: pl.BlockSpec(shape, lambda i, c: (0,) * len(shape))
    prev_blk = lambda i, c: jnp.maximum(c * MIX_CHUNKS - 1, 0)
    return pl.pallas_call(
        _mixer_kernel,
        grid=(b, s // rows),
        in_specs=[
            row(ATTN_WIDTH),
            pl.BlockSpec((None, KV_WIDTH, rows), lambda i, c: (i, 0, c)),
            pl.BlockSpec((None, KV_WIDTH, BLOCK), lambda i, c: (i, 0, prev_blk(i, c))),
            row(KV_WIDTH),
            pl.BlockSpec((None, BLOCK, KV_WIDTH), lambda i, c: (i, prev_blk(i, c), 0)),
            row(SSM_WIDTH), row(XBC_WIDTH), row(LANES),
            const((2, N_Q_HEADS, BLOCK, BLOCK)),
            pl.BlockSpec(memory_space=pltpu.SMEM),
            const((CONV_K, XBC_WIDTH)), const((1, XBC_WIDTH)), const((1, LANES)), const((1, LANES)),
            const((1, SSM_WIDTH)), const((1, ATTN_WIDTH)), const((1, SSM_WIDTH)),
            const((LANES, SSM_WIDTH)),
        ],
        out_specs=row(ATTN_WIDTH + SSM_WIDTH),
        out_shape=jax.ShapeDtypeStruct((b, s, ATTN_WIDTH + SSM_WIDTH), BF16),
        scratch_shapes=[
            pltpu.VMEM((SUBLANES + rows, XBC_WIDTH), F32),
            pltpu.VMEM((SSM_GROUPS, D_STATE, GROUP_WIDTH), F32),
            pltpu.VMEM((N_KV_HEADS, 2, KV_WIDTH, BLOCK + rows), BF16),
            pltpu.VMEM((N_KV_HEADS, BLOCK + rows, 2 * LANES), BF16),
        ],
        compiler_params=pltpu.CompilerParams(
            dimension_semantics=("parallel", "arbitrary"), vmem_limit_bytes=VMEM_LIMIT_BYTES),
        name="mixer",
    )(q, kt, kt, v, v, z, xbc, dt, bias, sinks, conv_w, conv_b, dt_bias, a_log, d_skip_x, attn_norm, ssm_norm,
      expand)


def _ffn_kernel(mix_ref, x_ref, mod_ref, wo_ref, n2_ref, wgu_ref, wd_ref, fn_ref, o_ref, h_ref, acc_ref):
    x1 = x_ref[...] + mod_ref[2:3, :] * _dot(mix_ref[...], wo_ref[...])
    h = _rms(x1) * n2_ref[...]
    h_ref[...] = (h * (1.0 + mod_ref[4:5, :]) + mod_ref[3:4, :]).astype(BF16)
    o_ref[...] = x1
    acc_ref[...] = jnp.zeros_like(acc_ref)

    for j in range(D_FF // FFN_FC):
        c0 = j * FFN_FC
        hb = h_ref[...]
        g = _dot(hb, wgu_ref[:, c0:c0 + FFN_FC])
        u = _dot(hb, wgu_ref[:, D_FF + c0:D_FF + c0 + FFN_FC])
        a = (_silu(g) * u).astype(BF16)
        acc_ref[...] += _dot(a, wd_ref[c0:c0 + FFN_FC, :])
    o_ref[...] = _rms(o_ref[...] + mod_ref[5:6, :] * acc_ref[...]) * fn_ref[...]


def _ffn_call(mix, x, mod, w_o, norm2, w_gu, w_down, final_norm):
    b, s, _ = x.shape
    tm = FFN_TM
    row = lambda: pl.BlockSpec((None, tm, D_MODEL), lambda i, j: (i, j, 0))
    const = lambda shape: pl.BlockSpec(shape, lambda i, j: (0,) * len(shape), pipeline_mode=pl.Buffered(1))
    return pl.pallas_call(
        _ffn_kernel,
        grid=(b, s // tm),
        in_specs=[
            row(), row(),
            pl.BlockSpec((None, 6, D_MODEL), lambda i, j: (i, 0, 0)),
            const((D_MODEL, D_MODEL)), const((1, D_MODEL)),
            const((D_MODEL, 2 * D_FF)), const((D_FF, D_MODEL)), const((1, D_MODEL)),
        ],
        out_specs=row(),
        out_shape=jax.ShapeDtypeStruct((b, s, D_MODEL), F32),
        scratch_shapes=[pltpu.VMEM((tm, D_MODEL), BF16), pltpu.VMEM((tm, D_MODEL), F32)],
        compiler_params=pltpu.CompilerParams(
            dimension_semantics=("parallel", "parallel"), vmem_limit_bytes=VMEM_LIMIT_BYTES),
        name="ffn",
    )(mix, x, mod, w_o, norm2, w_gu, w_down, final_norm)


def _head_expand_matrix():
    e = np.zeros((LANES, SSM_WIDTH), np.float32)
    for h in range(SSM_HEADS):
        e[h, h * SSM_HEAD_DIM:(h + 1) * SSM_HEAD_DIM] = 1.0
    return e


def kernel(x, c, ada_w, ada_b, norm1, w_in, conv_w, conv_b, dt_bias, A_log, D_skip, sinks, attn_out_norm,
           ssm_out_norm, w_o, norm2, w_gate_up, w_down, rel_bias, final_norm):
    assert ada_w.shape[0] == 1, "the final RMSNorm is fused into the (single) layer's FFN call"
    b = x.shape[0]
    bias = _bias_call(rel_bias)
    expand = jnp.asarray(_head_expand_matrix(), BF16)
    pad_heads = lambda v: jnp.pad(v, (0, LANES - SSM_HEADS))[None, :]
    c_pad = jnp.pad(c, ((0, SUBLANES - b), (0, 0)))
    mod = _ada_call(c_pad, ada_w[0], ada_b[0][None, :])[:b].reshape(b, 6, D_MODEL)
    w_q = w_in[0][:, :Q_END] * (HEAD_DIM ** -0.5)
    w_main = jnp.concatenate([w_q, w_in[0][:, Q_END:XBC_END]], axis=1).astype(BF16)
    w_dt = jnp.pad(w_in[0][:, XBC_END:], ((0, 0), (0, LANES - SSM_HEADS))).astype(BF16)
    q, kt, v, z, xbc, dt = _inproj_call(x, mod, norm1[0][None, :], w_main, w_dt)
    mix = _mixer_call(
        q, kt, v, z, xbc, dt, bias, sinks[0], conv_w[0], conv_b[0][None, :], pad_heads(dt_bias[0]),
        pad_heads(A_log[0]), jnp.repeat(D_skip[0], SSM_HEAD_DIM)[None, :], attn_out_norm[0][None, :],
        ssm_out_norm[0][None, :], expand)
    return _ffn_call(mix, x, mod, w_o[0].astype(BF16), norm2[0][None, :], w_gate_up[0].astype(BF16),
                     w_down[0].astype(BF16), final_norm[None, :])
```

```python
import numpy as np
import jax
import jax.numpy as jnp
from jax import lax
from jax.experimental import pallas as pl
from jax.experimental.pallas import tpu as pltpu

F32 = jnp.float32
BF16 = jnp.bfloat16

D_MODEL = 1024
HEAD_DIM = 64
N_Q_HEADS = 8
N_KV_HEADS = 2
Q_PER_KV = N_Q_HEADS // N_KV_HEADS
ATTN_WIDTH = N_Q_HEADS * HEAD_DIM
KV_WIDTH = N_KV_HEADS * HEAD_DIM
WINDOW = 128
BLOCK = 128
N_BUCKETS = 32
MAX_DISTANCE = 128

SSM_HEAD_DIM = 64
SSM_HEADS = 8
SSM_GROUPS = 2
HEADS_PER_GROUP = SSM_HEADS // SSM_GROUPS
SSM_WIDTH = SSM_HEADS * SSM_HEAD_DIM
GROUP_WIDTH = SSM_WIDTH // SSM_GROUPS
D_STATE = 128
CONV_K = 4
CHUNK = 128
XBC_WIDTH = SSM_WIDTH + 2 * SSM_GROUPS * D_STATE
D_FF = 2816
EPS = 1e-6

LANES = 128
SUBLANES = 8
VMEM_LIMIT_BYTES = 56 * 1024 * 1024

ADA_TN = 1536
INPROJ_TM = 512
FFN_TM = 512
FFN_FC = 256
MIX_CHUNKS = 4
MIX_ROWS = MIX_CHUNKS * CHUNK

Q_END = ATTN_WIDTH
K_END = Q_END + KV_WIDTH
V_END = K_END + KV_WIDTH
Z_END = V_END + SSM_WIDTH
XBC_END = Z_END + XBC_WIDTH

assert BLOCK == CHUNK == LANES and KV_WIDTH == LANES and 2 * HEAD_DIM == LANES


def _silu(x):
    return x * jax.nn.sigmoid(x)


def _rms(x):
    return x * lax.rsqrt(jnp.mean(x * x, axis=-1, keepdims=True) + EPS)


def _split2(x):
    hi = x.astype(BF16)
    lo = (x - hi.astype(F32)).astype(BF16)
    return hi, lo


def _dot(a, b):
    return jnp.dot(a, b, preferred_element_type=F32)


def _dot_nt(a, b):
    return lax.dot_general(a, b, (((1,), (1,)), ((), ())), preferred_element_type=F32)


def _dot_tn(a, b):
    return lax.dot_general(a, b, (((0,), (0,)), ((), ())), preferred_element_type=F32)


def _ada_kernel(c_ref, w_ref, b_ref, o_ref):
    cond = _silu(c_ref[...])
    o_ref[...] = _dot(cond.astype(BF16), w_ref[...].astype(BF16)) + b_ref[...]


def _ada_call(c_pad, ada_w, ada_b):
    n = ada_w.shape[1]
    return pl.pallas_call(
        _ada_kernel,
        grid=(n // ADA_TN,),
        in_specs=[
            pl.BlockSpec((SUBLANES, D_MODEL), lambda j: (0, 0)),
            pl.BlockSpec((D_MODEL, ADA_TN), lambda j: (0, j)),
            pl.BlockSpec((1, ADA_TN), lambda j: (0, j)),
        ],
        out_specs=pl.BlockSpec((SUBLANES, ADA_TN), lambda j: (0, j)),
        out_shape=jax.ShapeDtypeStruct((SUBLANES, n), F32),
        compiler_params=pltpu.CompilerParams(dimension_semantics=("parallel",)),
        name="ada_mod",
    )(c_pad, ada_w, ada_b)


def _t5_bucket_table():
    i = np.arange(BLOCK)[:, None]
    j = np.arange(BLOCK)[None, :]
    n = np.where(j <= i, i - j, i + BLOCK - j)
    assert WINDOW == BLOCK and n.min() >= 0 and n.max() < WINDOW
    max_exact = N_BUCKETS // 2
    large = max_exact + (np.log(np.maximum(n, 1) / max_exact) / np.log(MAX_DISTANCE / max_exact)
                         * (N_BUCKETS - max_exact)).astype(np.int32)
    large = np.minimum(large, N_BUCKETS - 1)
    bucket = np.where(n < max_exact, n, large).astype(np.int32)
    first = np.where(j <= i, bucket, -1)
    return np.stack([first, bucket]).astype(np.int32)


def _bias_kernel(idx_ref, rb_ref, o_ref):
    for v in range(2):
        idx = idx_ref[v]
        hits = [idx == b for b in range(N_BUCKETS)]
        for h in range(N_Q_HEADS):
            acc = jnp.full((BLOCK, BLOCK), -jnp.inf, F32)
            for b in range(N_BUCKETS):
                acc = jnp.where(hits[b], rb_ref[b, h], acc)
            o_ref[v, h] = acc


def _bias_call(rel_bias):
    idx = jnp.asarray(_t5_bucket_table())
    return pl.pallas_call(
        _bias_kernel,
        in_specs=[
            pl.BlockSpec(memory_space=pltpu.VMEM),
            pl.BlockSpec(memory_space=pltpu.SMEM),
        ],
        out_specs=pl.BlockSpec(memory_space=pltpu.VMEM),
        out_shape=jax.ShapeDtypeStruct((2, N_Q_HEADS, BLOCK, BLOCK), F32),
        name="t5_bias",
    )(idx, rel_bias)


def _inproj_kernel(x_ref, mod_ref, n1_ref, w_ref, wdt_ref, cw_ref, cb_ref, q_ref, kt_ref, v_ref, gz_ref, act_ref,
                   dt_ref, xe_ref):
    tm = x_ref.shape[0]

    @pl.when(pl.program_id(1) == 0)
    def _():
        xe_ref[0:SUBLANES, :] = jnp.zeros((SUBLANES, XBC_WIDTH), F32)

    h = _rms(x_ref[...]) * n1_ref[...]
    h = h * (1.0 + mod_ref[1:2, :]) + mod_ref[0:1, :]
    hb = h.astype(BF16)
    xbc = _dot(hb, w_ref[:, Z_END:XBC_END])
    xe_ref[SUBLANES:, :] = xbc
    conv = cw_ref[CONV_K - 1:CONV_K, :] * xbc + cb_ref[...]
    for k in range(CONV_K - 1):
        conv = conv + cw_ref[k:k + 1, :] * xe_ref[pl.ds(SUBLANES - (CONV_K - 1) + k, tm), :]
    xe_ref[0:SUBLANES, :] = xbc[tm - SUBLANES:, :]
    act_ref[...] = _silu(conv).astype(BF16)
    gz_ref[...] = _silu(_dot(hb, w_ref[:, V_END:Z_END])).astype(BF16)
    q_ref[...] = _dot(hb, w_ref[:, 0:Q_END]).astype(BF16)
    kt_ref[...] = _dot(hb, w_ref[:, Q_END:K_END]).T.astype(BF16)
    v_ref[...] = _dot(hb, w_ref[:, K_END:V_END]).astype(BF16)
    dt_ref[...] = _dot(hb, wdt_ref[...])


def _inproj_call(x, mod, norm1, w_main, w_dt, conv_w, conv_b):
    b, s, _ = x.shape
    tm = INPROJ_TM
    row = lambda width: pl.BlockSpec((None, tm, width), lambda i, j: (i, j, 0))
    const = lambda shape: pl.BlockSpec(shape, lambda i, j: (0,) * len(shape))
    return pl.pallas_call(
        _inproj_kernel,
        grid=(b, s // tm),
        in_specs=[
            row(D_MODEL),
            pl.BlockSpec((None, 6, D_MODEL), lambda i, j: (i, 0, 0)),
            const((1, D_MODEL)),
            const((D_MODEL, XBC_END)),
            const((D_MODEL, LANES)),
            const((CONV_K, XBC_WIDTH)), const((1, XBC_WIDTH)),
        ],
        out_specs=[row(ATTN_WIDTH), pl.BlockSpec((None, KV_WIDTH, tm), lambda i, j: (i, 0, j)), row(KV_WIDTH),
                   row(SSM_WIDTH), row(XBC_WIDTH), row(LANES)],
        out_shape=[
            jax.ShapeDtypeStruct((b, s, ATTN_WIDTH), BF16),
            jax.ShapeDtypeStruct((b, KV_WIDTH, s), BF16),
            jax.ShapeDtypeStruct((b, s, KV_WIDTH), BF16),
            jax.ShapeDtypeStruct((b, s, SSM_WIDTH), BF16),
            jax.ShapeDtypeStruct((b, s, XBC_WIDTH), BF16),
            jax.ShapeDtypeStruct((b, s, LANES), F32),
        ],
        scratch_shapes=[pltpu.VMEM((SUBLANES + tm, XBC_WIDTH), F32)],
        compiler_params=pltpu.CompilerParams(
            dimension_semantics=("parallel", "arbitrary"), vmem_limit_bytes=VMEM_LIMIT_BYTES),
        name="in_proj",
    )(x, mod, norm1, w_main, w_dt, conv_w, conv_b)


def _stage_keys_values(kt_ref, ktp_ref, v_ref, vp_ref, ka_ref, vv_ref):
    zeros = jnp.zeros((HEAD_DIM, BLOCK + MIX_ROWS), BF16)
    for g in range(N_KV_HEADS):
        rows = slice(g * HEAD_DIM, (g + 1) * HEAD_DIM)
        for par in range(2):
            live = slice(par * HEAD_DIM, (par + 1) * HEAD_DIM)
            dead = slice((1 - par) * HEAD_DIM, (2 - par) * HEAD_DIM)
            ka_ref[g, par, live, 0:BLOCK] = ktp_ref[rows, :]
            ka_ref[g, par, live, BLOCK:] = kt_ref[rows, :]
            ka_ref[g, par, dead, :] = zeros

    def duplicate_halves(v):
        v32 = pltpu.bitcast(v, jnp.uint32)
        swapped = pltpu.roll(v32, HEAD_DIM, axis=1)
        lane = lax.broadcasted_iota(jnp.int32, v32.shape, 1)
        first = lane < HEAD_DIM
        return (pltpu.bitcast(jnp.where(first, v32, swapped), BF16),
                pltpu.bitcast(jnp.where(first, swapped, v32), BF16))

    for src, r0, r1 in ((vp_ref, 0, BLOCK), (v_ref, BLOCK, BLOCK + MIX_ROWS)):
        for g, dup in enumerate(duplicate_halves(src[...])):
            vv_ref[g, r0:r1, 0:LANES] = dup


def _scores(q_ref, ka_ref, j):
    rows = pl.ds(j * BLOCK, BLOCK)
    win = pl.ds(j * BLOCK, 2 * BLOCK)
    out = []
    for h in range(N_Q_HEADS):
        t, par = h // 2, h % 2
        out.append(_dot(q_ref[rows, t * LANES:(t + 1) * LANES], ka_ref[t // 2, par, :, win]))
    return out


def _softmax(scores, bias_ref, sinks_ref, own):
    s = [jnp.where(own, sc[:, BLOCK:], sc[:, :BLOCK]) + bias_ref[h] for h, sc in enumerate(scores)]
    m = [jnp.maximum(jnp.max(s[h], axis=-1, keepdims=True), sinks_ref[h]) for h in range(N_Q_HEADS)]
    p = [jnp.exp(s[h] - m[h]) for h in range(N_Q_HEADS)]
    pcat = [jnp.concatenate([jnp.where(own, 0.0, ph), jnp.where(own, ph, 0.0)], axis=1).astype(BF16) for ph in p]
    sink_term = [jnp.exp(sinks_ref[h] - m[h]) for h in range(N_Q_HEADS)]
    return pcat, sink_term


def _attend(pcat, sink_term, vv_ref, an_ref, o_ref, j):
    win = pl.ds(j * BLOCK, 2 * BLOCK)
    rows = pl.ds(j * BLOCK, BLOCK)
    o2 = [_dot(pcat[h], vv_ref[h // Q_PER_KV, win, :]) for h in range(N_Q_HEADS)]
    heads = [o2[h][:, :LANES] / (o2[h][:, LANES:] + sink_term[h]) for h in range(N_Q_HEADS)]
    lane = lax.broadcasted_iota(jnp.int32, (BLOCK, LANES), 1)
    tiles = [jnp.where(lane < HEAD_DIM, heads[2 * t], heads[2 * t + 1]) for t in range(N_Q_HEADS // 2)]
    ssq = sum(jnp.sum(tl * tl, axis=-1, keepdims=True) for tl in tiles)
    inv = lax.rsqrt(ssq * (1.0 / ATTN_WIDTH) + EPS)
    for t, tl in enumerate(tiles):
        cols = slice(t * LANES, (t + 1) * LANES)
        o_ref[rows, cols] = (tl * inv * an_ref[:, cols]).astype(BF16)


def _expand_heads(vals, expand):
    hi, lo = _split2(vals)
    return _dot(hi, expand) + _dot(lo, expand)


def _mixer_kernel(q_ref, kt_ref, ktp_ref, v_ref, vp_ref, gz_ref, act_ref, dt_ref, bias_ref, sinks_ref,
                  dtb_ref, alog_ref, dskip_ref, an_ref, sn_ref, expand_ref, o_ref,
                  state_ref, ka_ref, vv_ref, xbd_ref):
    step = pl.program_id(1)
    nch = MIX_CHUNKS

    @pl.when(step == 0)
    def _():
        state_ref[...] = jnp.zeros_like(state_ref)
        vv_ref[:, :, LANES:] = jnp.ones((N_KV_HEADS, BLOCK + MIX_ROWS, LANES), BF16)

    row_i = lax.broadcasted_iota(jnp.int32, (BLOCK, BLOCK), 0)
    col_i = lax.broadcasted_iota(jnp.int32, (BLOCK, BLOCK), 1)
    own = col_i <= row_i
    bias_of = lambda j: bias_ref.at[jnp.minimum(step, 1)] if j == 0 else bias_ref.at[1]

    _stage_keys_values(kt_ref, ktp_ref, v_ref, vp_ref, ka_ref, vv_ref)
    scores = {0: _scores(q_ref, ka_ref, 0), 1: _scores(q_ref, ka_ref, 1)}

    xs = act_ref[:, :SSM_WIDTH].astype(F32)
    bc_b = act_ref[:, SSM_WIDTH:]

    dt = jax.nn.softplus(dt_ref[...] + dtb_ref[...])
    dta = dt * (-jnp.exp(alog_ref[...]))
    tri = own.astype(BF16)
    dta_hi, dta_lo = _split2(dta)
    acs = [_dot(tri, dta_hi[j * CHUNK:(j + 1) * CHUNK]) + _dot(tri, dta_lo[j * CHUNK:(j + 1) * CHUNK])
           for j in range(nch)]

    pcat, sink_term = _softmax(scores.pop(0), bias_of(0), sinks_ref, own)
    _attend(pcat, sink_term, vv_ref, an_ref, o_ref, 0)

    acs_t = [a.T for a in acs]
    dt_t = [dt[j * CHUNK:(j + 1) * CHUNK].T for j in range(nch)]
    dec_in = [jnp.exp(a) for a in acs]
    dec_out = jnp.concatenate([jnp.exp(a[CHUNK - 1:CHUNK, :] - a) for a in acs], axis=0)
    expand = expand_ref[...]
    state_in_x = _dot((dt * dec_out).astype(BF16), expand)
    decay_in_x = _dot(jnp.concatenate(dec_in, axis=0).astype(BF16), expand)
    chunk_decay = [_expand_heads(d[CHUNK - SUBLANES:, :], expand)[SUBLANES - 1:SUBLANES, :] for d in dec_in]

    scores[2] = _scores(q_ref, ka_ref, 2)
    pcat, sink_term = _softmax(scores.pop(1), bias_of(1), sinks_ref, own)
    _attend(pcat, sink_term, vv_ref, an_ref, o_ref, 1)

    xdec_b = (xs * state_in_x).astype(BF16)
    lane_head = lax.broadcasted_iota(jnp.int32, (CHUNK, GROUP_WIDTH), 1) // SSM_HEAD_DIM
    for j in range(nch):
        for g in range(SSM_GROUPS):
            xg = act_ref[j * CHUNK:(j + 1) * CHUNK, g * GROUP_WIDTH:(g + 1) * GROUP_WIDTH]
            for r in range(HEADS_PER_GROUP):
                xbd_ref[j, g, r * CHUNK:(r + 1) * CHUNK, :] = jnp.where(lane_head == r, xg, jnp.zeros_like(xg))
    bm =[[bc_b[j * CHUNK:(j + 1) * CHUNK, g * D_STATE:(g + 1) * D_STATE] for g in range(SSM_GROUPS)]
          for j in range(nch)]
    cm = [[bc_b[j * CHUNK:(j + 1) * CHUNK, (SSM_GROUPS + g) * D_STATE:(SSM_GROUPS + g + 1) * D_STATE]
           for g in range(SSM_GROUPS)] for j in range(nch)]
    grp = lambda a, j, g: a[j * CHUNK:(j + 1) * CHUNK, g * GROUP_WIDTH:(g + 1) * GROUP_WIDTH]
    cbm = [[_dot_nt(cm[j][g], bm[j][g]) for g in range(SSM_GROUPS)] for j in range(nch)]
    d_state = [[_dot_tn(bm[j][g], grp(xdec_b, j, g)) for g in range(SSM_GROUPS)] for j in range(nch)]

    scores[3] = _scores(q_ref, ka_ref, 3)
    pcat, sink_term = _softmax(scores.pop(2), bias_of(2), sinks_ref, own)
    _attend(pcat, sink_term, vv_ref, an_ref, o_ref, 2)

    y_diag = []
    for j in range(nch):
        per_group = []
        for g in range(SSM_GROUPS):
            w = []
            for r in range(HEADS_PER_GROUP):
                h = g * HEADS_PER_GROUP + r
                seg = acs[j][:, h:h + 1] - acs_t[j][h:h + 1, :]
                lmat = jnp.exp(jnp.where(own, seg, -jnp.inf))
                w.append(cbm[j][g] * lmat * dt_t[j][h:h + 1, :])
            y_g = (_dot(jnp.concatenate(w[:2], axis=1).astype(BF16), xbd_ref[j, g, 0:2 * CHUNK, :])
                   + _dot(jnp.concatenate(w[2:], axis=1).astype(BF16), xbd_ref[j, g, 2 * CHUNK:, :]))
            per_group.append(y_g)
        y_diag.append(per_group)

    pcat, sink_term = _softmax(scores.pop(3), bias_of(3), sinks_ref, own)
    _attend(pcat, sink_term, vv_ref, an_ref, o_ref, 3)

    y_rows = []
    for j in range(nch):
        per_group = []
        for g in range(SSM_GROUPS):
            prev = state_ref[g]
            y_off = _dot(cm[j][g], prev.astype(BF16)) * grp(decay_in_x, j, g)
            state_ref[g] = prev * chunk_decay[j][:, g * GROUP_WIDTH:(g + 1) * GROUP_WIDTH] + d_state[j][g]
            per_group.append(y_diag[j][g] + y_off)
        y_rows.append(jnp.concatenate(per_group, axis=-1))
    y = jnp.concatenate(y_rows, axis=0) + dskip_ref[...] * xs
    y = y * gz_ref[...].astype(F32)
    y = jnp.concatenate([_rms(y[:, :GROUP_WIDTH]), _rms(y[:, GROUP_WIDTH:])], axis=-1)
    o_ref[:, ATTN_WIDTH:] = (y * sn_ref[...]).astype(BF16)


def _mixer_call(q, kt, v, gz, act, dt, bias, sinks, dt_bias, a_log, d_skip_x, attn_norm, ssm_norm, expand):
    b, s, _ = q.shape
    rows = MIX_ROWS
    row = lambda width: pl.BlockSpec((None, rows, width), lambda i, c: (i, c, 0))
    const = lambda shape: pl.BlockSpec(shape, lambda i, c: (0,) * len(shape))
    prev_blk = lambda i, c: jnp.maximum(c * MIX_CHUNKS - 1, 0)
    return pl.pallas_call(
        _mixer_kernel,
        grid=(b, s // rows),
        in_specs=[
            row(ATTN_WIDTH),
            pl.BlockSpec((None, KV_WIDTH, rows), lambda i, c: (i, 0, c)),
            pl.BlockSpec((None, KV_WIDTH, BLOCK), lambda i, c: (i, 0, prev_blk(i, c))),
            row(KV_WIDTH),
            pl.BlockSpec((None, BLOCK, KV_WIDTH), lambda i, c: (i, prev_blk(i, c), 0)),
            row(SSM_WIDTH), row(XBC_WIDTH), row(LANES),
            const((2, N_Q_HEADS, BLOCK, BLOCK)),
            pl.BlockSpec(memory_space=pltpu.SMEM),
            const((1, LANES)), const((1, LANES)),
            const((1, SSM_WIDTH)), const((1, ATTN_WIDTH)), const((1, SSM_WIDTH)),
            const((LANES, SSM_WIDTH)),
        ],
        out_specs=row(ATTN_WIDTH + SSM_WIDTH),
        out_shape=jax.ShapeDtypeStruct((b, s, ATTN_WIDTH + SSM_WIDTH), BF16),
        scratch_shapes=[
            pltpu.VMEM((SSM_GROUPS, D_STATE, GROUP_WIDTH), F32),
            pltpu.VMEM((N_KV_HEADS, 2, KV_WIDTH, BLOCK + rows), BF16),
            pltpu.VMEM((N_KV_HEADS, BLOCK + rows, 2 * LANES), BF16),
            pltpu.VMEM((MIX_CHUNKS, SSM_GROUPS, HEADS_PER_GROUP * CHUNK, GROUP_WIDTH), BF16),
        ],
        compiler_params=pltpu.CompilerParams(
            dimension_semantics=("parallel", "arbitrary"), vmem_limit_bytes=VMEM_LIMIT_BYTES),
        name="mixer",
    )(q, kt, kt, v, v, gz, act, dt, bias, sinks, dt_bias, a_log, d_skip_x, attn_norm, ssm_norm, expand)


def _ffn_kernel(mix_ref, x_ref, mod_ref, wo_ref, n2_ref, wgu_ref, wd_ref, fn_ref, o_ref, h_ref, acc_ref):
    x1 = x_ref[...] + mod_ref[2:3, :] * _dot(mix_ref[...], wo_ref[...])
    h = _rms(x1) * n2_ref[...]
    h_ref[...] = (h * (1.0 + mod_ref[4:5, :]) + mod_ref[3:4, :]).astype(BF16)
    o_ref[...] = x1
    acc_ref[...] = jnp.zeros_like(acc_ref)

    for j in range(D_FF // FFN_FC):
        c0 = j * FFN_FC
        hb = h_ref[...]
        g = _dot(hb, wgu_ref[:, c0:c0 + FFN_FC])
        u = _dot(hb, wgu_ref[:, D_FF + c0:D_FF + c0 + FFN_FC])
        a = (_silu(g) * u).astype(BF16)
        acc_ref[...] += _dot(a, wd_ref[c0:c0 + FFN_FC, :])
    o_ref[...] = _rms(o_ref[...] + mod_ref[5:6, :] * acc_ref[...]) * fn_ref[...]


def _ffn_call(mix, x, mod, w_o, norm2, w_gu, w_down, final_norm):
    b, s, _ = x.shape
    tm = FFN_TM
    row = lambda: pl.BlockSpec((None, tm, D_MODEL), lambda i, j: (i, j, 0))
    const = lambda shape: pl.BlockSpec(shape, lambda i, j: (0,) * len(shape), pipeline_mode=pl.Buffered(1))
    return pl.pallas_call(
        _ffn_kernel,
        grid=(b, s // tm),
        in_specs=[
            row(), row(),
            pl.BlockSpec((None, 6, D_MODEL), lambda i, j: (i, 0, 0)),
            const((D_MODEL, D_MODEL)), const((1, D_MODEL)),
            const((D_MODEL, 2 * D_FF)), const((D_FF, D_MODEL)), const((1, D_MODEL)),
        ],
        out_specs=row(),
        out_shape=jax.ShapeDtypeStruct((b, s, D_MODEL), F32),
        scratch_shapes=[pltpu.VMEM((tm, D_MODEL), BF16), pltpu.VMEM((tm, D_MODEL), F32)],
        compiler_params=pltpu.CompilerParams(
            dimension_semantics=("parallel", "parallel"), vmem_limit_bytes=VMEM_LIMIT_BYTES),
        name="ffn",
    )(mix, x, mod, w_o, norm2, w_gu, w_down, final_norm)


def _head_expand_matrix():
    e = np.zeros((LANES, SSM_WIDTH), np.float32)
    for h in range(SSM_HEADS):
        e[h, h * SSM_HEAD_DIM:(h + 1) * SSM_HEAD_DIM] = 1.0
    return e


def kernel(x, c, ada_w, ada_b, norm1, w_in, conv_w, conv_b, dt_bias, A_log, D_skip, sinks, attn_out_norm,
           ssm_out_norm, w_o, norm2, w_gate_up, w_down, rel_bias, final_norm):
    assert ada_w.shape[0] == 1, "the final RMSNorm is fused into the (single) layer's FFN call"
    b = x.shape[0]
    bias = _bias_call(rel_bias)
    expand = jnp.asarray(_head_expand_matrix(), BF16)
    pad_heads = lambda v: jnp.pad(v, (0, LANES - SSM_HEADS))[None, :]
    c_pad = jnp.pad(c, ((0, SUBLANES - b), (0, 0)))
    mod = _ada_call(c_pad, ada_w[0], ada_b[0][None, :])[:b].reshape(b, 6, D_MODEL)
    w_q = w_in[0][:, :Q_END] * (HEAD_DIM ** -0.5)
    w_main = jnp.concatenate([w_q, w_in[0][:, Q_END:XBC_END]], axis=1).astype(BF16)
    w_dt = jnp.pad(w_in[0][:, XBC_END:], ((0, 0), (0, LANES - SSM_HEADS))).astype(BF16)
    q, kt, v, gz, act, dt = _inproj_call(x, mod, norm1[0][None, :], w_main, w_dt, conv_w[0], conv_b[0][None, :])
    mix = _mixer_call(
        q, kt, v, gz, act, dt, bias, sinks[0], pad_heads(dt_bias[0]), pad_heads(A_log[0]),
        jnp.repeat(D_skip[0], SSM_HEAD_DIM)[None, :], attn_out_norm[0][None, :], ssm_out_norm[0][None, :], expand)
    return _ffn_call(mix, x, mod, w_o[0].astype(BF16), norm2[0][None, :], w_gate_up[0].astype(BF16),
                     w_down[0].astype(BF16), final_norm[None, :])
```

```python
import numpy as np
import jax
import jax.numpy as jnp
from jax import lax
from jax.experimental import pallas as pl
from jax.experimental.pallas import tpu as pltpu

F32 = jnp.float32
BF16 = jnp.bfloat16

D_MODEL = 1024
HEAD_DIM = 64
N_Q_HEADS = 8
N_KV_HEADS = 2
Q_PER_KV = N_Q_HEADS // N_KV_HEADS
ATTN_WIDTH = N_Q_HEADS * HEAD_DIM
KV_WIDTH = N_KV_HEADS * HEAD_DIM
WINDOW = 128
BLOCK = 128
N_BUCKETS = 32
MAX_DISTANCE = 128

SSM_HEAD_DIM = 64
SSM_HEADS = 8
SSM_GROUPS = 2
HEADS_PER_GROUP = SSM_HEADS // SSM_GROUPS
SSM_WIDTH = SSM_HEADS * SSM_HEAD_DIM
GROUP_WIDTH = SSM_WIDTH // SSM_GROUPS
D_STATE = 128
CONV_K = 4
CHUNK = 128
XBC_WIDTH = SSM_WIDTH + 2 * SSM_GROUPS * D_STATE
D_FF = 2816
EPS = 1e-6

LANES = 128
SUBLANES = 8
VMEM_LIMIT_BYTES = 56 * 1024 * 1024

ADA_TN = 1536
INPROJ_TM = 512
FFN_TM = 512
FFN_FC = 256
MIX_CHUNKS = 4
MIX_ROWS = MIX_CHUNKS * CHUNK

Q_END = ATTN_WIDTH
K_END = Q_END + KV_WIDTH
V_END = K_END + KV_WIDTH
Z_END = V_END + SSM_WIDTH
XBC_END = Z_END + XBC_WIDTH

assert BLOCK == CHUNK == LANES and KV_WIDTH == LANES and 2 * HEAD_DIM == LANES


def _silu(x):
    return x * jax.nn.sigmoid(x)


def _rms(x):
    return x * lax.rsqrt(jnp.mean(x * x, axis=-1, keepdims=True) + EPS)


def _split2(x):
    hi = x.astype(BF16)
    lo = (x - hi.astype(F32)).astype(BF16)
    return hi, lo


def _dot(a, b):
    return jnp.dot(a, b, preferred_element_type=F32)


def _dot_nt(a, b):
    return lax.dot_general(a, b, (((1,), (1,)), ((), ())), preferred_element_type=F32)


def _dot_tn(a, b):
    return lax.dot_general(a, b, (((0,), (0,)), ((), ())), preferred_element_type=F32)


def _ada_kernel(c_ref, w_ref, b_ref, o_ref):
    cond = _silu(c_ref[...])
    o_ref[...] = _dot(cond.astype(BF16), w_ref[...].astype(BF16)) + b_ref[...]


def _ada_call(c_pad, ada_w, ada_b):
    n = ada_w.shape[1]
    return pl.pallas_call(
        _ada_kernel,
        grid=(n // ADA_TN,),
        in_specs=[
            pl.BlockSpec((SUBLANES, D_MODEL), lambda j: (0, 0)),
            pl.BlockSpec((D_MODEL, ADA_TN), lambda j: (0, j)),
            pl.BlockSpec((1, ADA_TN), lambda j: (0, j)),
        ],
        out_specs=pl.BlockSpec((SUBLANES, ADA_TN), lambda j: (0, j)),
        out_shape=jax.ShapeDtypeStruct((SUBLANES, n), F32),
        compiler_params=pltpu.CompilerParams(dimension_semantics=("parallel",)),
        name="ada_mod",
    )(c_pad, ada_w, ada_b)


def _t5_bucket_table():
    i = np.arange(BLOCK)[:, None]
    j = np.arange(BLOCK)[None, :]
    n = np.where(j <= i, i - j, i + BLOCK - j)
    assert WINDOW == BLOCK and n.min() >= 0 and n.max() < WINDOW
    max_exact = N_BUCKETS // 2
    large = max_exact + (np.log(np.maximum(n, 1) / max_exact) / np.log(MAX_DISTANCE / max_exact)
                         * (N_BUCKETS - max_exact)).astype(np.int32)
    large = np.minimum(large, N_BUCKETS - 1)
    bucket = np.where(n < max_exact, n, large).astype(np.int32)
    first = np.where(j <= i, bucket, -1)
    return np.stack([first, bucket]).astype(np.int32)


def _bias_kernel(idx_ref, rb_ref, o_ref):
    for v in range(2):
        idx = idx_ref[v]
        hits = [idx == b for b in range(N_BUCKETS)]
        for h in range(N_Q_HEADS):
            acc = jnp.full((BLOCK, BLOCK), -jnp.inf, F32)
            for b in range(N_BUCKETS):
                acc = jnp.where(hits[b], rb_ref[b, h], acc)
            o_ref[v, h] = acc


def _bias_call(rel_bias):
    idx = jnp.asarray(_t5_bucket_table())
    return pl.pallas_call(
        _bias_kernel,
        in_specs=[
            pl.BlockSpec(memory_space=pltpu.VMEM),
            pl.BlockSpec(memory_space=pltpu.SMEM),
        ],
        out_specs=pl.BlockSpec(memory_space=pltpu.VMEM),
        out_shape=jax.ShapeDtypeStruct((2, N_Q_HEADS, BLOCK, BLOCK), F32),
        name="t5_bias",
    )(idx, rel_bias)


def _inproj_norm(x_ref, mod_ref, n1_ref, hb_ref):
    gain = n1_ref[...] * (1.0 + mod_ref[1:2, :])
    hb_ref[...] = (_rms(x_ref[...]) * gain + mod_ref[0:1, :]).astype(BF16)


def _inproj_project(hb_ref, w_ref, wdt_ref, cw_ref, cb_ref, q_ref, kt_ref, v_ref, gz_ref, act_ref, dt_ref, xe_ref):
    tm = hb_ref.shape[0]
    proj = lambda lo, hi: _dot(hb_ref[...], w_ref[:, lo:hi])
    tile = 2 * LANES
    for i, c0 in enumerate(range(0, XBC_WIDTH, tile)):
        cols = slice(c0, c0 + tile)
        xbc = proj(Z_END + c0, Z_END + c0 + tile)
        xe = jnp.concatenate([xe_ref[:, cols], xbc], axis=0)
        conv = cw_ref[CONV_K - 1:CONV_K, cols] * xbc + cb_ref[:, cols]
        for k in range(CONV_K - 1):
            conv = conv + cw_ref[k:k + 1, cols] * pltpu.roll(xe, CONV_K - 1 - k, axis=0)[SUBLANES:, :]
        xe_ref[:, cols] = xbc[tm - SUBLANES:, :]
        act_ref[:, cols] = _silu(conv).astype(BF16)
        if i < Q_END // tile:
            q_ref[:, i * tile:(i + 1) * tile] = proj(i * tile, (i + 1) * tile).astype(BF16)
        else:
            z0 = (i - Q_END // tile) * tile
            gz_ref[:, z0:z0 + tile] = _silu(proj(V_END + z0, V_END + z0 + tile)).astype(BF16)
    kt_ref[...] = proj(Q_END, K_END).T.astype(BF16)
    v_ref[...] = proj(K_END, V_END).astype(BF16)
    dt_ref[...] = _dot(hb_ref[...], wdt_ref[...])


def _inproj_kernel(x_ref, mod_ref, n1_ref, w_ref, wdt_ref, cw_ref, cb_ref, q_ref, kt_ref, v_ref, gz_ref, act_ref,
                   dt_ref, hb_ref, xe_ref):
    s = pl.program_id(1)
    last = pl.num_programs(1) - 1
    outs = (q_ref, kt_ref, v_ref, gz_ref, act_ref, dt_ref)
    norm = lambda slot: _inproj_norm(x_ref, mod_ref, n1_ref, hb_ref.at[slot])
    project = lambda slot: _inproj_project(hb_ref.at[slot], w_ref, wdt_ref, cw_ref, cb_ref, *outs, xe_ref)

    @pl.when(s == 0)
    def _():
        norm(0)
        xe_ref[0:SUBLANES, :] = jnp.zeros((SUBLANES, XBC_WIDTH), F32)

    @pl.when((s > 0) & (s < last))
    def _():
        project((s - 1) % 2)
        norm(s % 2)

    @pl.when(s == last)
    def _():
        project((s - 1) % 2)


def _inproj_call(x, mod, norm1, w_main, w_dt, conv_w, conv_b):
    b, s, _ = x.shape
    tm = INPROJ_TM
    n_tiles = s // tm
    row = lambda width: pl.BlockSpec((None, tm, width), lambda i, j: (i, jnp.maximum(j - 1, 0), 0))
    const = lambda shape: pl.BlockSpec(shape, lambda i, j: (0,) * len(shape))
    return pl.pallas_call(
        _inproj_kernel,
        grid=(b, n_tiles + 1),
        in_specs=[
            pl.BlockSpec((None, tm, D_MODEL), lambda i, j: (i, jnp.minimum(j, n_tiles - 1), 0)),
            pl.BlockSpec((None, 6, D_MODEL), lambda i, j: (i, 0, 0)),
            const((1, D_MODEL)),
            const((D_MODEL, XBC_END)),
            const((D_MODEL, LANES)),
            const((CONV_K, XBC_WIDTH)), const((1, XBC_WIDTH)),
        ],
        out_specs=[row(ATTN_WIDTH),
                   pl.BlockSpec((None, KV_WIDTH, tm), lambda i, j: (i, 0, jnp.maximum(j - 1, 0))),
                   row(KV_WIDTH), row(SSM_WIDTH), row(XBC_WIDTH), row(LANES)],
        out_shape=[
            jax.ShapeDtypeStruct((b, s, ATTN_WIDTH), BF16),
            jax.ShapeDtypeStruct((b, KV_WIDTH, s), BF16),
            jax.ShapeDtypeStruct((b, s, KV_WIDTH), BF16),
            jax.ShapeDtypeStruct((b, s, SSM_WIDTH), BF16),
            jax.ShapeDtypeStruct((b, s, XBC_WIDTH), BF16),
            jax.ShapeDtypeStruct((b, s, LANES), F32),
        ],
        scratch_shapes=[
            pltpu.VMEM((2, tm, D_MODEL), BF16),
            pltpu.VMEM((SUBLANES, XBC_WIDTH), F32),
        ],
        compiler_params=pltpu.CompilerParams(
            dimension_semantics=("parallel", "arbitrary"), vmem_limit_bytes=VMEM_LIMIT_BYTES),
        name="in_proj",
    )(x, mod, norm1, w_main, w_dt, conv_w, conv_b)


def _stage_keys_values(kt_ref, ktp_ref, v_ref, vp_ref, ka_ref, vv_ref):
    zeros = jnp.zeros((HEAD_DIM, BLOCK + MIX_ROWS), BF16)
    for g in range(N_KV_HEADS):
        rows = slice(g * HEAD_DIM, (g + 1) * HEAD_DIM)
        for par in range(2):
            live = slice(par * HEAD_DIM, (par + 1) * HEAD_DIM)
            dead = slice((1 - par) * HEAD_DIM, (2 - par) * HEAD_DIM)
            ka_ref[g, par, live, 0:BLOCK] = ktp_ref[rows, :]
            ka_ref[g, par, live, BLOCK:] = kt_ref[rows, :]
            ka_ref[g, par, dead, :] = zeros

    def duplicate_halves(v):
        v32 = pltpu.bitcast(v, jnp.uint32)
        swapped = pltpu.roll(v32, HEAD_DIM, axis=1)
        lane = lax.broadcasted_iota(jnp.int32, v32.shape, 1)
        first = lane < HEAD_DIM
        return (pltpu.bitcast(jnp.where(first, v32, swapped), BF16),
                pltpu.bitcast(jnp.where(first, swapped, v32), BF16))

    for src, r0, r1 in ((vp_ref, 0, BLOCK), (v_ref, BLOCK, BLOCK + MIX_ROWS)):
        for g, dup in enumerate(duplicate_halves(src[...])):
            vv_ref[g, r0:r1, 0:LANES] = dup


def _scores(q_ref, ka_ref, j):
    rows = pl.ds(j * BLOCK, BLOCK)
    win = pl.ds(j * BLOCK, 2 * BLOCK)
    out = []
    for h in range(N_Q_HEADS):
        t, par = h // 2, h % 2
        out.append(_dot(q_ref[rows, t * LANES:(t + 1) * LANES], ka_ref[t // 2, par, :, win]))
    return out


def _softmax(scores, bias_ref, sinks_ref, own):
    s = [jnp.where(own, sc[:, BLOCK:], sc[:, :BLOCK]) + bias_ref[h] for h, sc in enumerate(scores)]
    m = [jnp.maximum(jnp.max(s[h], axis=-1, keepdims=True), sinks_ref[h]) for h in range(N_Q_HEADS)]
    p = [jnp.exp(s[h] - m[h]) for h in range(N_Q_HEADS)]
    pcat = [jnp.concatenate([jnp.where(own, 0.0, ph), jnp.where(own, ph, 0.0)], axis=1).astype(BF16) for ph in p]
    sink_term = [jnp.exp(sinks_ref[h] - m[h]) for h in range(N_Q_HEADS)]
    return pcat, sink_term


def _attend(pcat, sink_term, vv_ref, an_ref, o_ref, j):
    win = pl.ds(j * BLOCK, 2 * BLOCK)
    rows = pl.ds(j * BLOCK, BLOCK)
    o2 = [_dot(pcat[h], vv_ref[h // Q_PER_KV, win, :]) for h in range(N_Q_HEADS)]
    heads = [o2[h][:, :LANES] / (o2[h][:, LANES:] + sink_term[h]) for h in range(N_Q_HEADS)]
    lane = lax.broadcasted_iota(jnp.int32, (BLOCK, LANES), 1)
    tiles = [jnp.where(lane < HEAD_DIM, heads[2 * t], heads[2 * t + 1]) for t in range(N_Q_HEADS // 2)]
    ssq = sum(jnp.sum(tl * tl, axis=-1, keepdims=True) for tl in tiles)
    inv = lax.rsqrt(ssq * (1.0 / ATTN_WIDTH) + EPS)
    for t, tl in enumerate(tiles):
        cols = slice(t * LANES, (t + 1) * LANES)
        o_ref[rows, cols] = (tl * inv * an_ref[:, cols]).astype(BF16)


def _expand_heads(vals, expand):
    hi, lo = _split2(vals)
    return _dot(hi, expand) + _dot(lo, expand)


def _mixer_kernel(q_ref, kt_ref, ktp_ref, v_ref, vp_ref, gz_ref, act_ref, dt_ref, bias_ref, sinks_ref,
                  dtb_ref, alog_ref, dskip_ref, an_ref, sn_ref, expand_ref, o_ref,
                  state_ref, ka_ref, vv_ref, xbd_ref):
    step = pl.program_id(1)
    nch = MIX_CHUNKS

    @pl.when(step == 0)
    def _():
        state_ref[...] = jnp.zeros_like(state_ref)
        vv_ref[:, :, LANES:] = jnp.ones((N_KV_HEADS, BLOCK + MIX_ROWS, LANES), BF16)

    row_i = lax.broadcasted_iota(jnp.int32, (BLOCK, BLOCK), 0)
    col_i = lax.broadcasted_iota(jnp.int32, (BLOCK, BLOCK), 1)
    own = col_i <= row_i
    bias_of = lambda j: bias_ref.at[jnp.minimum(step, 1)] if j == 0 else bias_ref.at[1]

    _stage_keys_values(kt_ref, ktp_ref, v_ref, vp_ref, ka_ref, vv_ref)
    scores = {0: _scores(q_ref, ka_ref, 0), 1: _scores(q_ref, ka_ref, 1)}

    xs = act_ref[:, :SSM_WIDTH].astype(F32)
    bc_b = act_ref[:, SSM_WIDTH:]

    dt = jax.nn.softplus(dt_ref[...] + dtb_ref[...])
    dta = dt * (-jnp.exp(alog_ref[...]))
    tri = own.astype(BF16)
    dta_hi, dta_lo = _split2(dta)
    acs = [_dot(tri, dta_hi[j * CHUNK:(j + 1) * CHUNK]) + _dot(tri, dta_lo[j * CHUNK:(j + 1) * CHUNK])
           for j in range(nch)]

    pcat, sink_term = _softmax(scores.pop(0), bias_of(0), sinks_ref, own)
    _attend(pcat, sink_term, vv_ref, an_ref, o_ref, 0)

    acs_t = [a.T for a in acs]
    dt_t = [dt[j * CHUNK:(j + 1) * CHUNK].T for j in range(nch)]
    dec_in = [jnp.exp(a) for a in acs]
    dec_out = jnp.concatenate([jnp.exp(a[CHUNK - 1:CHUNK, :] - a) for a in acs], axis=0)
    expand = expand_ref[...]
    state_in_x = _dot((dt * dec_out).astype(BF16), expand)
    decay_in_x = _dot(jnp.concatenate(dec_in, axis=0).astype(BF16), expand)
    chunk_decay = [_expand_heads(d[CHUNK - SUBLANES:, :], expand)[SUBLANES - 1:SUBLANES, :] for d in dec_in]

    scores[2] = _scores(q_ref, ka_ref, 2)
    pcat, sink_term = _softmax(scores.pop(1), bias_of(1), sinks_ref, own)
    _attend(pcat, sink_term, vv_ref, an_ref, o_ref, 1)

    xdec_b = (xs * state_in_x).astype(BF16)
    lane_head = lax.broadcasted_iota(jnp.int32, (CHUNK, GROUP_WIDTH), 1) // SSM_HEAD_DIM
    for j in range(nch):
        for g in range(SSM_GROUPS):
            xg = act_ref[j * CHUNK:(j + 1) * CHUNK, g * GROUP_WIDTH:(g + 1) * GROUP_WIDTH]
            for r in range(HEADS_PER_GROUP):
                xbd_ref[j, g, r * CHUNK:(r + 1) * CHUNK, :] = jnp.where(lane_head == r, xg, jnp.zeros_like(xg))
    bm =[[bc_b[j * CHUNK:(j + 1) * CHUNK, g * D_STATE:(g + 1) * D_STATE] for g in range(SSM_GROUPS)]
          for j in range(nch)]
    cm = [[bc_b[j * CHUNK:(j + 1) * CHUNK, (SSM_GROUPS + g) * D_STATE:(SSM_GROUPS + g + 1) * D_STATE]
           for g in range(SSM_GROUPS)] for j in range(nch)]
    grp = lambda a, j, g: a[j * CHUNK:(j + 1) * CHUNK, g * GROUP_WIDTH:(g + 1) * GROUP_WIDTH]
    cbm = [[_dot_nt(cm[j][g], bm[j][g]) for g in range(SSM_GROUPS)] for j in range(nch)]
    d_state = [[_dot_tn(bm[j][g], grp(xdec_b, j, g)) for g in range(SSM_GROUPS)] for j in range(nch)]

    scores[3] = _scores(q_ref, ka_ref, 3)
    pcat, sink_term = _softmax(scores.pop(2), bias_of(2), sinks_ref, own)
    _attend(pcat, sink_term, vv_ref, an_ref, o_ref, 2)

    y_diag = []
    for j in range(nch):
        per_group = []
        for g in range(SSM_GROUPS):
            w = []
            for r in range(HEADS_PER_GROUP):
                h = g * HEADS_PER_GROUP + r
                seg = acs[j][:, h:h + 1] - acs_t[j][h:h + 1, :]
                lmat = jnp.exp(jnp.where(own, seg, -jnp.inf))
                w.append(cbm[j][g] * lmat * dt_t[j][h:h + 1, :])
            y_g = (_dot(jnp.concatenate(w[:2], axis=1).astype(BF16), xbd_ref[j, g, 0:2 * CHUNK, :])
                   + _dot(jnp.concatenate(w[2:], axis=1).astype(BF16), xbd_ref[j, g, 2 * CHUNK:, :]))
            per_group.append(y_g)
        y_diag.append(per_group)

    pcat, sink_term = _softmax(scores.pop(3), bias_of(3), sinks_ref, own)
    _attend(pcat, sink_term, vv_ref, an_ref, o_ref, 3)

    y_rows = []
    for j in range(nch):
        per_group = []
        for g in range(SSM_GROUPS):
            prev = state_ref[g]
            y_off = _dot(cm[j][g], prev.astype(BF16)) * grp(decay_in_x, j, g)
            state_ref[g] = prev * chunk_decay[j][:, g * GROUP_WIDTH:(g + 1) * GROUP_WIDTH] + d_state[j][g]
            per_group.append(y_diag[j][g] + y_off)
        y_rows.append(jnp.concatenate(per_group, axis=-1))
    y = jnp.concatenate(y_rows, axis=0) + dskip_ref[...] * xs
    y = y * gz_ref[...].astype(F32)
    y = jnp.concatenate([_rms(y[:, :GROUP_WIDTH]), _rms(y[:, GROUP_WIDTH:])], axis=-1)
    o_ref[:, ATTN_WIDTH:] = (y * sn_ref[...]).astype(BF16)


def _mixer_call(q, kt, v, gz, act, dt, bias, sinks, dt_bias, a_log, d_skip_x, attn_norm, ssm_norm, expand):
    b, s, _ = q.shape
    rows = MIX_ROWS
    row = lambda width: pl.BlockSpec((None, rows, width), lambda i, c: (i, c, 0))
    const = lambda shape: pl.BlockSpec(shape, lambda i, c: (0,) * len(shape))
    prev_blk = lambda i, c: jnp.maximum(c * MIX_CHUNKS - 1, 0)
    return pl.pallas_call(
        _mixer_kernel,
        grid=(b, s // rows),
        in_specs=[
            row(ATTN_WIDTH),
            pl.BlockSpec((None, KV_WIDTH, rows), lambda i, c: (i, 0, c)),
            pl.BlockSpec((None, KV_WIDTH, BLOCK), lambda i, c: (i, 0, prev_blk(i, c))),
            row(KV_WIDTH),
            pl.BlockSpec((None, BLOCK, KV_WIDTH), lambda i, c: (i, prev_blk(i, c), 0)),
            row(SSM_WIDTH), row(XBC_WIDTH), row(LANES),
            const((2, N_Q_HEADS, BLOCK, BLOCK)),
            pl.BlockSpec(memory_space=pltpu.SMEM),
            const((1, LANES)), const((1, LANES)),
            const((1, SSM_WIDTH)), const((1, ATTN_WIDTH)), const((1, SSM_WIDTH)),
            const((LANES, SSM_WIDTH)),
        ],
        out_specs=row(ATTN_WIDTH + SSM_WIDTH),
        out_shape=jax.ShapeDtypeStruct((b, s, ATTN_WIDTH + SSM_WIDTH), BF16),
        scratch_shapes=[
            pltpu.VMEM((SSM_GROUPS, D_STATE, GROUP_WIDTH), F32),
            pltpu.VMEM((N_KV_HEADS, 2, KV_WIDTH, BLOCK + rows), BF16),
            pltpu.VMEM((N_KV_HEADS, BLOCK + rows, 2 * LANES), BF16),
            pltpu.VMEM((MIX_CHUNKS, SSM_GROUPS, HEADS_PER_GROUP * CHUNK, GROUP_WIDTH), BF16),
        ],
        compiler_params=pltpu.CompilerParams(
            dimension_semantics=("parallel", "arbitrary"), vmem_limit_bytes=VMEM_LIMIT_BYTES),
        name="mixer",
    )(q, kt, kt, v, v, gz, act, dt, bias, sinks, dt_bias, a_log, d_skip_x, attn_norm, ssm_norm, expand)


def _ffn_kernel(mix_ref, x_ref, mod_ref, wo_ref, n2_ref, wgu_ref, wd_ref, fn_ref, o_ref, h_ref, acc_ref):
    x1 = x_ref[...] + mod_ref[2:3, :] * _dot(mix_ref[...], wo_ref[...])
    h = _rms(x1) * n2_ref[...]
    h_ref[...] = (h * (1.0 + mod_ref[4:5, :]) + mod_ref[3:4, :]).astype(BF16)
    o_ref[...] = x1
    acc_ref[...] = jnp.zeros_like(acc_ref)

    for j in range(D_FF // FFN_FC):
        c0 = j * FFN_FC
        hb = h_ref[...]
        g = _dot(hb, wgu_ref[:, c0:c0 + FFN_FC])
        u = _dot(hb, wgu_ref[:, D_FF + c0:D_FF + c0 + FFN_FC])
        a = (_silu(g) * u).astype(BF16)
        acc_ref[...] += _dot(a, wd_ref[c0:c0 + FFN_FC, :])
    o_ref[...] = _rms(o_ref[...] + mod_ref[5:6, :] * acc_ref[...]) * fn_ref[...]


def _ffn_call(mix, x, mod, w_o, norm2, w_gu, w_down, final_norm):
    b, s, _ = x.shape
    tm = FFN_TM
    row = lambda: pl.BlockSpec((None, tm, D_MODEL), lambda i, j: (i, j, 0))
    const = lambda shape: pl.BlockSpec(shape, lambda i, j: (0,) * len(shape), pipeline_mode=pl.Buffered(1))
    return pl.pallas_call(
        _ffn_kernel,
        grid=(b, s // tm),
        in_specs=[
            row(), row(),
            pl.BlockSpec((None, 6, D_MODEL), lambda i, j: (i, 0, 0)),
            const((D_MODEL, D_MODEL)), const((1, D_MODEL)),
            const((D_MODEL, 2 * D_FF)), const((D_FF, D_MODEL)), const((1, D_MODEL)),
        ],
        out_specs=row(),
        out_shape=jax.ShapeDtypeStruct((b, s, D_MODEL), F32),
        scratch_shapes=[pltpu.VMEM((tm, D_MODEL), BF16), pltpu.VMEM((tm, D_MODEL), F32)],
        compiler_params=pltpu.CompilerParams(
            dimension_semantics=("parallel", "parallel"), vmem_limit_bytes=VMEM_LIMIT_BYTES),
        name="ffn",
    )(mix, x, mod, w_o, norm2, w_gu, w_down, final_norm)


def _head_expand_matrix():
    e = np.zeros((LANES, SSM_WIDTH), np.float32)
    for h in range(SSM_HEADS):
        e[h, h * SSM_HEAD_DIM:(h + 1) * SSM_HEAD_DIM] = 1.0
    return e


def kernel(x, c, ada_w, ada_b, norm1, w_in, conv_w, conv_b, dt_bias, A_log, D_skip, sinks, attn_out_norm,
           ssm_out_norm, w_o, norm2, w_gate_up, w_down, rel_bias, final_norm):
    assert ada_w.shape[0] == 1, "the final RMSNorm is fused into the (single) layer's FFN call"
    b = x.shape[0]
    bias = _bias_call(rel_bias)
    expand = jnp.asarray(_head_expand_matrix(), BF16)
    pad_heads = lambda v: jnp.pad(v, (0, LANES - SSM_HEADS))[None, :]
    c_pad = jnp.pad(c, ((0, SUBLANES - b), (0, 0)))
    mod = _ada_call(c_pad, ada_w[0], ada_b[0][None, :])[:b].reshape(b, 6, D_MODEL)
    w_q = w_in[0][:, :Q_END] * (HEAD_DIM ** -0.5)
    w_main = jnp.concatenate([w_q, w_in[0][:, Q_END:XBC_END]], axis=1).astype(BF16)
    w_dt = jnp.pad(w_in[0][:, XBC_END:], ((0, 0), (0, LANES - SSM_HEADS))).astype(BF16)
    q, kt, v, gz, act, dt = _inproj_call(x, mod, norm1[0][None, :], w_main, w_dt, conv_w[0], conv_b[0][None, :])
    mix = _mixer_call(
        q, kt, v, gz, act, dt, bias, sinks[0], pad_heads(dt_bias[0]), pad_heads(A_log[0]),
        jnp.repeat(D_skip[0], SSM_HEAD_DIM)[None, :], attn_out_norm[0][None, :], ssm_out_norm[0][None, :], expand)
    return _ffn_call(mix, x, mod, w_o[0].astype(BF16), norm2[0][None, :], w_gate_up[0].astype(BF16),
                     w_down[0].astype(BF16), final_norm[None, :])
```

```python
import numpy as np
import jax
import jax.numpy as jnp
from jax import lax
from jax.experimental import pallas as pl
from jax.experimental.pallas import tpu as pltpu

F32 = jnp.float32
BF16 = jnp.bfloat16

D_MODEL = 1024
HEAD_DIM = 64
N_Q_HEADS = 8
N_KV_HEADS = 2
Q_PER_KV = N_Q_HEADS // N_KV_HEADS
ATTN_WIDTH = N_Q_HEADS * HEAD_DIM
KV_WIDTH = N_KV_HEADS * HEAD_DIM
WINDOW = 128
BLOCK = 128
N_BUCKETS = 32
MAX_DISTANCE = 128

SSM_HEAD_DIM = 64
SSM_HEADS = 8
SSM_GROUPS = 2
HEADS_PER_GROUP = SSM_HEADS // SSM_GROUPS
SSM_WIDTH = SSM_HEADS * SSM_HEAD_DIM
GROUP_WIDTH = SSM_WIDTH // SSM_GROUPS
D_STATE = 128
CONV_K = 4
CHUNK = 128
XBC_WIDTH = SSM_WIDTH + 2 * SSM_GROUPS * D_STATE
D_FF = 2816
EPS = 1e-6
LOG2E = float(np.log2(np.e))

LANES = 128
SUBLANES = 8
VMEM_LIMIT_BYTES = 56 * 1024 * 1024

ADA_TN = 1536
INPROJ_TM = 512
FFN_TM = 512
FFN_FC = 256
FFN_SUB = 1
MIX_CHUNKS = 4
MIX_ROWS = MIX_CHUNKS * CHUNK

Q_END = ATTN_WIDTH
K_END = Q_END + KV_WIDTH
V_END = K_END + KV_WIDTH
Z_END = V_END + SSM_WIDTH
XBC_END = Z_END + XBC_WIDTH

assert BLOCK == CHUNK == LANES and KV_WIDTH == LANES and 2 * HEAD_DIM == LANES


def _silu(x):
    return x * jax.nn.sigmoid(x)


def _rms(x):
    return x * lax.rsqrt(jnp.mean(x * x, axis=-1, keepdims=True) + EPS)


def _split2(x):
    hi = x.astype(BF16)
    lo = (x - hi.astype(F32)).astype(BF16)
    return hi, lo


def _dot(a, b):
    return jnp.dot(a, b, preferred_element_type=F32)


def _dot_nt(a, b):
    return lax.dot_general(a, b, (((1,), (1,)), ((), ())), preferred_element_type=F32)


def _dot_tn(a, b):
    return lax.dot_general(a, b, (((0,), (0,)), ((), ())), preferred_element_type=F32)


def _ada_kernel(c_ref, w_ref, b_ref, o_ref):
    cond = _silu(c_ref[...])
    o_ref[...] = _dot(cond.astype(BF16), w_ref[...].astype(BF16)) + b_ref[...]


def _ada_call(c_pad, ada_w, ada_b):
    n = ada_w.shape[1]
    return pl.pallas_call(
        _ada_kernel,
        grid=(n // ADA_TN,),
        in_specs=[
            pl.BlockSpec((SUBLANES, D_MODEL), lambda j: (0, 0)),
            pl.BlockSpec((D_MODEL, ADA_TN), lambda j: (0, j)),
            pl.BlockSpec((1, ADA_TN), lambda j: (0, j)),
        ],
        out_specs=pl.BlockSpec((SUBLANES, ADA_TN), lambda j: (0, j)),
        out_shape=jax.ShapeDtypeStruct((SUBLANES, n), F32),
        compiler_params=pltpu.CompilerParams(dimension_semantics=("parallel",)),
        name="ada_mod",
    )(c_pad, ada_w, ada_b)


def _t5_bucket_table():
    i = np.arange(BLOCK)[:, None]
    j = np.arange(BLOCK)[None, :]
    n = np.where(j <= i, i - j, i + BLOCK - j)
    assert WINDOW == BLOCK and n.min() >= 0 and n.max() < WINDOW
    max_exact = N_BUCKETS // 2
    large = max_exact + (np.log(np.maximum(n, 1) / max_exact) / np.log(MAX_DISTANCE / max_exact)
                         * (N_BUCKETS - max_exact)).astype(np.int32)
    large = np.minimum(large, N_BUCKETS - 1)
    bucket = np.where(n < max_exact, n, large).astype(np.int32)
    first = np.where(j <= i, bucket, -1)
    return np.stack([first, bucket]).astype(np.int32)


def _bias_kernel(idx_ref, rb_ref, o_ref):
    for v in range(2):
        idx = idx_ref[v]
        hits = [idx == b for b in range(N_BUCKETS)]
        for h in range(N_Q_HEADS):
            acc = jnp.full((BLOCK, BLOCK), -jnp.inf, F32)
            for b in range(N_BUCKETS):
                acc = jnp.where(hits[b], rb_ref[b, h] * LOG2E, acc)
            o_ref[v, h] = acc


def _bias_call(rel_bias):
    idx = jnp.asarray(_t5_bucket_table())
    return pl.pallas_call(
        _bias_kernel,
        in_specs=[
            pl.BlockSpec(memory_space=pltpu.VMEM),
            pl.BlockSpec(memory_space=pltpu.SMEM),
        ],
        out_specs=pl.BlockSpec(memory_space=pltpu.VMEM),
        out_shape=jax.ShapeDtypeStruct((2, N_Q_HEADS, BLOCK, BLOCK), F32),
        name="t5_bias",
    )(idx, rel_bias)


def _inproj_norm(x_ref, mod_ref, n1_ref, hb_ref):
    gain = n1_ref[...] * (1.0 + mod_ref[1:2, :])
    hb_ref[...] = (_rms(x_ref[...]) * gain + mod_ref[0:1, :]).astype(BF16)


def _inproj_project(hb_ref, w_ref, wdt_ref, cw_ref, cb_ref, q_ref, kt_ref, v_ref, gz_ref, act_ref, dt_ref, xe_ref):
    tm = hb_ref.shape[0]
    proj = lambda lo, hi: _dot(hb_ref[...], w_ref[:, lo:hi])
    tile = 2 * LANES
    for i, c0 in enumerate(range(0, XBC_WIDTH, tile)):
        cols = slice(c0, c0 + tile)
        xbc = proj(Z_END + c0, Z_END + c0 + tile)
        xe = jnp.concatenate([xe_ref[:, cols], xbc], axis=0)
        conv = cw_ref[CONV_K - 1:CONV_K, cols] * xbc + cb_ref[:, cols]
        for k in range(CONV_K - 1):
            conv = conv + cw_ref[k:k + 1, cols] * pltpu.roll(xe, CONV_K - 1 - k, axis=0)[SUBLANES:, :]
        xe_ref[:, cols] = xbc[tm - SUBLANES:, :]
        act_ref[:, cols] = _silu(conv).astype(BF16)
        if i < Q_END // tile:
            q_ref[:, i * tile:(i + 1) * tile] = proj(i * tile, (i + 1) * tile).astype(BF16)
        else:
            z0 = (i - Q_END // tile) * tile
            gz_ref[:, z0:z0 + tile] = _silu(proj(V_END + z0, V_END + z0 + tile)).astype(BF16)
    kt_ref[...] = proj(Q_END, K_END).T.astype(BF16)
    v_ref[...] = proj(K_END, V_END).astype(BF16)
    dt_ref[...] = _dot(hb_ref[...], wdt_ref[...])


def _inproj_kernel(x_ref, mod_ref, n1_ref, w_ref, wdt_ref, cw_ref, cb_ref, q_ref, kt_ref, v_ref, gz_ref, act_ref,
                   dt_ref, hb_ref, xe_ref):
    @pl.when(pl.program_id(1) == 0)
    def _():
        xe_ref[...] = jnp.zeros_like(xe_ref)

    _inproj_norm(x_ref, mod_ref, n1_ref, hb_ref)
    _inproj_project(hb_ref, w_ref, wdt_ref, cw_ref, cb_ref, q_ref, kt_ref, v_ref, gz_ref, act_ref, dt_ref, xe_ref)


def _inproj_call(x, mod, norm1, w_main, w_dt, conv_w, conv_b):
    b, s, _ = x.shape
    tm = INPROJ_TM
    row = lambda width: pl.BlockSpec((None, tm, width), lambda i, j: (i, j, 0))
    const = lambda shape: pl.BlockSpec(shape, lambda i, j: (0,) * len(shape))
    return pl.pallas_call(
        _inproj_kernel,
        grid=(b, s // tm),
        in_specs=[
            row(D_MODEL),
            pl.BlockSpec((None, 6, D_MODEL), lambda i, j: (i, 0, 0)),
            const((1, D_MODEL)),
            const((D_MODEL, XBC_END)),
            const((D_MODEL, LANES)),
            const((CONV_K, XBC_WIDTH)), const((1, XBC_WIDTH)),
        ],
        out_specs=[row(ATTN_WIDTH), pl.BlockSpec((None, KV_WIDTH, tm), lambda i, j: (i, 0, j)), row(KV_WIDTH),
                   row(SSM_WIDTH), row(XBC_WIDTH), row(LANES)],
        out_shape=[
            jax.ShapeDtypeStruct((b, s, ATTN_WIDTH), BF16),
            jax.ShapeDtypeStruct((b, KV_WIDTH, s), BF16),
            jax.ShapeDtypeStruct((b, s, KV_WIDTH), BF16),
            jax.ShapeDtypeStruct((b, s, SSM_WIDTH), BF16),
            jax.ShapeDtypeStruct((b, s, XBC_WIDTH), BF16),
            jax.ShapeDtypeStruct((b, s, LANES), F32),
        ],
        scratch_shapes=[
            pltpu.VMEM((tm, D_MODEL), BF16),
            pltpu.VMEM((SUBLANES, XBC_WIDTH), F32),
        ],
        compiler_params=pltpu.CompilerParams(
            dimension_semantics=("parallel", "arbitrary"), vmem_limit_bytes=VMEM_LIMIT_BYTES),
        name="in_proj",
    )(x, mod, norm1, w_main, w_dt, conv_w, conv_b)


def _stage_keys_values(kt_ref, ktp_ref, v_ref, vp_ref, ka_ref, vv_ref):
    zeros = jnp.zeros((HEAD_DIM, BLOCK + MIX_ROWS), BF16)
    for g in range(N_KV_HEADS):
        rows = slice(g * HEAD_DIM, (g + 1) * HEAD_DIM)
        for par in range(2):
            live = slice(par * HEAD_DIM, (par + 1) * HEAD_DIM)
            dead = slice((1 - par) * HEAD_DIM, (2 - par) * HEAD_DIM)
            ka_ref[g, par, live, 0:BLOCK] = ktp_ref[rows, :]
            ka_ref[g, par, live, BLOCK:] = kt_ref[rows, :]
            ka_ref[g, par, dead, :] = zeros

    def duplicate_halves(v):
        v32 = pltpu.bitcast(v, jnp.uint32)
        swapped = pltpu.roll(v32, HEAD_DIM, axis=1)
        lane = lax.broadcasted_iota(jnp.int32, v32.shape, 1)
        first = lane < HEAD_DIM
        return (pltpu.bitcast(jnp.where(first, v32, swapped), BF16),
                pltpu.bitcast(jnp.where(first, swapped, v32), BF16))

    for src, r0, r1 in ((vp_ref, 0, BLOCK), (v_ref, BLOCK, BLOCK + MIX_ROWS)):
        for g, dup in enumerate(duplicate_halves(src[...])):
            vv_ref[g, r0:r1, 0:LANES] = dup


def _scores(q_ref, ka_ref, j):
    rows = pl.ds(j * BLOCK, BLOCK)
    win = pl.ds(j * BLOCK, 2 * BLOCK)
    out = []
    for h in range(N_Q_HEADS):
        t, par = h // 2, h % 2
        out.append(_dot(q_ref[rows, t * LANES:(t + 1) * LANES], ka_ref[t // 2, par, :, win]))
    return out


def _softmax(scores, bias_ref, sinks_ref, own):
    s = [jnp.where(own, sc[:, BLOCK:], sc[:, :BLOCK]) + bias_ref[h] for h, sc in enumerate(scores)]
    sink = [sinks_ref[h] * LOG2E for h in range(N_Q_HEADS)]
    m = [jnp.maximum(jnp.max(s[h], axis=-1, keepdims=True), sink[h]) for h in range(N_Q_HEADS)]
    p = [jnp.exp2(s[h] - m[h]) for h in range(N_Q_HEADS)]
    pcat = [jnp.concatenate([jnp.where(own, 0.0, ph), jnp.where(own, ph, 0.0)], axis=1).astype(BF16) for ph in p]
    sink_term = [jnp.exp2(sink[h] - m[h]) for h in range(N_Q_HEADS)]
    return pcat, sink_term


def _values(pcat, vv_ref, j):
    win = pl.ds(j * BLOCK, 2 * BLOCK)
    return [_dot(pcat[h], vv_ref[h // Q_PER_KV, win, :]) for h in range(N_Q_HEADS)]


def _finish_attention(o2, sink_term, an_ref, o_ref, j):
    rows = pl.ds(j * BLOCK, BLOCK)
    heads = [o2[h][:, :LANES] / (o2[h][:, LANES:] + sink_term[h]) for h in range(N_Q_HEADS)]
    lane = lax.broadcasted_iota(jnp.int32, (BLOCK, LANES), 1)
    tiles = [jnp.where(lane < HEAD_DIM, heads[2 * t], heads[2 * t + 1]) for t in range(N_Q_HEADS // 2)]
    ssq = sum(jnp.sum(tl * tl, axis=-1, keepdims=True) for tl in tiles)
    inv = lax.rsqrt(ssq * (1.0 / ATTN_WIDTH) + EPS)
    for t, tl in enumerate(tiles):
        cols = slice(t * LANES, (t + 1) * LANES)
        o_ref[rows, cols] = (tl * inv * an_ref[:, cols]).astype(BF16)


def _expand_heads(vals, expand):
    hi, lo = _split2(vals)
    return _dot(hi, expand) + _dot(lo, expand)


def _mixer_kernel(q_ref, kt_ref, ktp_ref, v_ref, vp_ref, gz_ref, act_ref, dt_ref, bias_ref, sinks_ref,
                  dtb_ref, alog_ref, dskip_ref, an_ref, sn_ref, expand_ref, o_ref,
                  state_ref, ka_ref, vv_ref, xbd_ref):
    step = pl.program_id(1)
    nch = MIX_CHUNKS

    @pl.when(step == 0)
    def _():
        state_ref[...] = jnp.zeros_like(state_ref)
        vv_ref[:, :, LANES:] = jnp.ones((N_KV_HEADS, BLOCK + MIX_ROWS, LANES), BF16)

    row_i = lax.broadcasted_iota(jnp.int32, (BLOCK, BLOCK), 0)
    col_i = lax.broadcasted_iota(jnp.int32, (BLOCK, BLOCK), 1)
    own = col_i <= row_i
    bias_of = lambda j: bias_ref.at[jnp.minimum(step, 1)] if j == 0 else bias_ref.at[1]

    grp = lambda a, j, g: a[j * CHUNK:(j + 1) * CHUNK, g * GROUP_WIDTH:(g + 1) * GROUP_WIDTH]
    _stage_keys_values(kt_ref, ktp_ref, v_ref, vp_ref, ka_ref, vv_ref)
    scores = {0: _scores(q_ref, ka_ref, 0), 1: _scores(q_ref, ka_ref, 1)}

    xs = act_ref[:, :SSM_WIDTH].astype(F32)
    bc_b = act_ref[:, SSM_WIDTH:]

    dt = jax.nn.softplus(dt_ref[...] + dtb_ref[...])
    dta = dt * (-LOG2E * jnp.exp(alog_ref[...]))
    tri = own.astype(BF16)
    dta_hi, dta_lo = _split2(dta)
    acs = [_dot(tri, dta_hi[j * CHUNK:(j + 1) * CHUNK]) + _dot(tri, dta_lo[j * CHUNK:(j + 1) * CHUNK])
           for j in range(nch)]

    pcat, sink_term = _softmax(scores.pop(0), bias_of(0), sinks_ref, own)
    _finish_attention(_values(pcat, vv_ref, 0), sink_term, an_ref, o_ref, 0)

    acs_t = [a.T for a in acs]
    dt_t = [dt[j * CHUNK:(j + 1) * CHUNK].T for j in range(nch)]
    dec_in = [jnp.exp2(a) for a in acs]
    dec_out = jnp.concatenate([jnp.exp2(a[CHUNK - 1:CHUNK, :] - a) for a in acs], axis=0)
    expand = expand_ref[...]
    state_in_x = _dot((dt * dec_out).astype(BF16), expand)
    decay_in_x = _dot(jnp.concatenate(dec_in, axis=0).astype(BF16), expand)
    chunk_decay = [_expand_heads(d[CHUNK - SUBLANES:, :], expand)[SUBLANES - 1:SUBLANES, :] for d in dec_in]

    scores[2] = _scores(q_ref, ka_ref, 2)
    pcat, sink_term = _softmax(scores.pop(1), bias_of(1), sinks_ref, own)
    _finish_attention(_values(pcat, vv_ref, 1), sink_term, an_ref, o_ref, 1)

    xdec_b = (xs * state_in_x).astype(BF16)
    lane_head = lax.broadcasted_iota(jnp.int32, (CHUNK, GROUP_WIDTH), 1) // SSM_HEAD_DIM
    for j in range(nch):
        for g in range(SSM_GROUPS):
            xg = act_ref[j * CHUNK:(j + 1) * CHUNK, g * GROUP_WIDTH:(g + 1) * GROUP_WIDTH]
            for r in range(HEADS_PER_GROUP):
                xbd_ref[j, g, r * CHUNK:(r + 1) * CHUNK, :] = jnp.where(lane_head == r, xg, jnp.zeros_like(xg))
    bm = [[bc_b[j * CHUNK:(j + 1) * CHUNK, g * D_STATE:(g + 1) * D_STATE] for g in range(SSM_GROUPS)]
          for j in range(nch)]
    cm = [[bc_b[j * CHUNK:(j + 1) * CHUNK, (SSM_GROUPS + g) * D_STATE:(SSM_GROUPS + g + 1) * D_STATE]
           for g in range(SSM_GROUPS)] for j in range(nch)]
    cbm = [[_dot_nt(cm[j][g], bm[j][g]) for g in range(SSM_GROUPS)] for j in range(nch)]
    d_state = [[_dot_tn(bm[j][g], grp(xdec_b, j, g)) for g in range(SSM_GROUPS)] for j in range(nch)]

    scores[3] = _scores(q_ref, ka_ref, 3)
    pcat, sink_term = _softmax(scores.pop(2), bias_of(2), sinks_ref, own)
    _finish_attention(_values(pcat, vv_ref, 2), sink_term, an_ref, o_ref, 2)

    def intra_chunk(j):
        per_group = []
        for g in range(SSM_GROUPS):
            w = []
            for r in range(HEADS_PER_GROUP):
                h = g * HEADS_PER_GROUP + r
                seg = acs[j][:, h:h + 1] - acs_t[j][h:h + 1, :]
                lmat = jnp.exp2(jnp.where(own, seg, -jnp.inf))
                w.append(cbm[j][g] * lmat * dt_t[j][h:h + 1, :])
            per_group.append(_dot(jnp.concatenate(w[:2], axis=1).astype(BF16), xbd_ref[j, g, 0:2 * CHUNK, :])
                             + _dot(jnp.concatenate(w[2:], axis=1).astype(BF16), xbd_ref[j, g, 2 * CHUNK:, :]))
        return per_group

    y_diag = [intra_chunk(j) for j in range(nch)]

    pcat, sink_term = _softmax(scores.pop(3), bias_of(3), sinks_ref, own)
    _finish_attention(_values(pcat, vv_ref, 3), sink_term, an_ref, o_ref, 3)

    y_rows = []
    for j in range(nch):
        per_group = []
        for g in range(SSM_GROUPS):
            prev = state_ref[g]
            y_off = _dot(cm[j][g], prev.astype(BF16)) * grp(decay_in_x, j, g)
            state_ref[g] = prev * chunk_decay[j][:, g * GROUP_WIDTH:(g + 1) * GROUP_WIDTH] + d_state[j][g]
            per_group.append(y_diag[j][g] + y_off)
        y_rows.append(jnp.concatenate(per_group, axis=-1))
    y = jnp.concatenate(y_rows, axis=0) + dskip_ref[...] * xs
    y = y * gz_ref[...].astype(F32)
    y = jnp.concatenate([_rms(y[:, :GROUP_WIDTH]), _rms(y[:, GROUP_WIDTH:])], axis=-1)
    o_ref[:, ATTN_WIDTH:] = (y * sn_ref[...]).astype(BF16)


def _mixer_call(q, kt, v, gz, act, dt, bias, sinks, dt_bias, a_log, d_skip_x, attn_norm, ssm_norm, expand):
    b, s, _ = q.shape
    rows = MIX_ROWS
    row = lambda width: pl.BlockSpec((None, rows, width), lambda i, c: (i, c, 0))
    const = lambda shape: pl.BlockSpec(shape, lambda i, c: (0,) * len(shape))
    prev_blk = lambda i, c: jnp.maximum(c * MIX_CHUNKS - 1, 0)
    return pl.pallas_call(
        _mixer_kernel,
        grid=(b, s // rows),
        in_specs=[
            row(ATTN_WIDTH),
            pl.BlockSpec((None, KV_WIDTH, rows), lambda i, c: (i, 0, c)),
            pl.BlockSpec((None, KV_WIDTH, BLOCK), lambda i, c: (i, 0, prev_blk(i, c))),
            row(KV_WIDTH),
            pl.BlockSpec((None, BLOCK, KV_WIDTH), lambda i, c: (i, prev_blk(i, c), 0)),
            row(SSM_WIDTH), row(XBC_WIDTH), row(LANES),
            const((2, N_Q_HEADS, BLOCK, BLOCK)),
            pl.BlockSpec(memory_space=pltpu.SMEM),
            const((1, LANES)), const((1, LANES)),
            const((1, SSM_WIDTH)), const((1, ATTN_WIDTH)), const((1, SSM_WIDTH)),
            const((LANES, SSM_WIDTH)),
        ],
        out_specs=row(ATTN_WIDTH + SSM_WIDTH),
        out_shape=jax.ShapeDtypeStruct((b, s, ATTN_WIDTH + SSM_WIDTH), BF16),
        scratch_shapes=[
            pltpu.VMEM((SSM_GROUPS, D_STATE, GROUP_WIDTH), F32),
            pltpu.VMEM((N_KV_HEADS, 2, KV_WIDTH, BLOCK + rows), BF16),
            pltpu.VMEM((N_KV_HEADS, BLOCK + rows, 2 * LANES), BF16),
            pltpu.VMEM((MIX_CHUNKS, SSM_GROUPS, HEADS_PER_GROUP * CHUNK, GROUP_WIDTH), BF16),
        ],
        compiler_params=pltpu.CompilerParams(
            dimension_semantics=("parallel", "arbitrary"), vmem_limit_bytes=VMEM_LIMIT_BYTES),
        name="mixer",
    )(q, kt, kt, v, v, gz, act, dt, bias, sinks, dt_bias, a_log, d_skip_x, attn_norm, ssm_norm, expand)


def _ffn_kernel(mix_ref, x_ref, mod_ref, wo_ref, n2_ref, wgu_ref, wd_ref, fn_ref, o_ref, h_ref, acc_ref):
    sub = x_ref.shape[0] // FFN_SUB
    blocks = [pl.ds(i * sub, sub) for i in range(FFN_SUB)]
    gain2 = n2_ref[...] * (1.0 + mod_ref[4:5, :])
    for rows in blocks:
        x1 = x_ref[rows, :] + mod_ref[2:3, :] * _dot(mix_ref[rows, :], wo_ref[...])
        h_ref[rows, :] = (_rms(x1) * gain2 + mod_ref[3:4, :]).astype(BF16)
        o_ref[rows, :] = x1

    n_slices = D_FF // FFN_FC
    for j in range(n_slices):
        c0 = j * FFN_FC
        for rows in blocks:
            hb = h_ref[rows, :]
            g = _dot(hb, wgu_ref[:, c0:c0 + FFN_FC])
            u = _dot(hb, wgu_ref[:, D_FF + c0:D_FF + c0 + FFN_FC])
            d = _dot((_silu(g) * u).astype(BF16), wd_ref[c0:c0 + FFN_FC, :])
            if j == 0:
                acc_ref[rows, :] = d
            elif j < n_slices - 1:
                acc_ref[rows, :] += d
            else:
                o_ref[rows, :] = _rms(o_ref[rows, :] + mod_ref[5:6, :] * (acc_ref[rows, :] + d)) * fn_ref[...]


def _ffn_call(mix, x, mod, w_o, norm2, w_gu, w_down, final_norm):
    b, s, _ = x.shape
    tm = FFN_TM
    row = lambda: pl.BlockSpec((None, tm, D_MODEL), lambda i, j: (i, j, 0))
    const = lambda shape: pl.BlockSpec(shape, lambda i, j: (0,) * len(shape), pipeline_mode=pl.Buffered(1))
    return pl.pallas_call(
        _ffn_kernel,
        grid=(b, s // tm),
        in_specs=[
            row(), row(),
            pl.BlockSpec((None, 6, D_MODEL), lambda i, j: (i, 0, 0)),
            const((D_MODEL, D_MODEL)), const((1, D_MODEL)),
            const((D_MODEL, 2 * D_FF)), const((D_FF, D_MODEL)), const((1, D_MODEL)),
        ],
        out_specs=row(),
        out_shape=jax.ShapeDtypeStruct((b, s, D_MODEL), F32),
        scratch_shapes=[pltpu.VMEM((tm, D_MODEL), BF16), pltpu.VMEM((tm, D_MODEL), F32)],
        compiler_params=pltpu.CompilerParams(
            dimension_semantics=("parallel", "parallel"), vmem_limit_bytes=VMEM_LIMIT_BYTES),
        name="ffn",
    )(mix, x, mod, w_o, norm2, w_gu, w_down, final_norm)


def _head_expand_matrix():
    e = np.zeros((LANES, SSM_WIDTH), np.float32)
    for h in range(SSM_HEADS):
        e[h, h * SSM_HEAD_DIM:(h + 1) * SSM_HEAD_DIM] = 1.0
    return e


def kernel(x, c, ada_w, ada_b, norm1, w_in, conv_w, conv_b, dt_bias, A_log, D_skip, sinks, attn_out_norm,
           ssm_out_norm, w_o, norm2, w_gate_up, w_down, rel_bias, final_norm):
    assert ada_w.shape[0] == 1, "the final RMSNorm is fused into the (single) layer's FFN call"
    b = x.shape[0]
    bias = _bias_call(rel_bias)
    expand = jnp.asarray(_head_expand_matrix(), BF16)
    pad_heads = lambda v: jnp.pad(v, (0, LANES - SSM_HEADS))[None, :]
    c_pad = jnp.pad(c, ((0, SUBLANES - b), (0, 0)))
    mod = _ada_call(c_pad, ada_w[0], ada_b[0][None, :])[:b].reshape(b, 6, D_MODEL)
    w_q = w_in[0][:, :Q_END] * (HEAD_DIM ** -0.5 * LOG2E)
    w_main = jnp.concatenate([w_q, w_in[0][:, Q_END:XBC_END]], axis=1).astype(BF16)
    w_dt = jnp.pad(w_in[0][:, XBC_END:], ((0, 0), (0, LANES - SSM_HEADS))).astype(BF16)
    q, kt, v, gz, act, dt = _inproj_call(x, mod, norm1[0][None, :], w_main, w_dt, conv_w[0], conv_b[0][None, :])
    mix = _mixer_call(
        q, kt, v, gz, act, dt, bias, sinks[0], pad_heads(dt_bias[0]), pad_heads(A_log[0]),
        jnp.repeat(D_skip[0], SSM_HEAD_DIM)[None, :], attn_out_norm[0][None, :], ssm_out_norm[0][None, :], expand)
    return _ffn_call(mix, x, mod, w_o[0].astype(BF16), norm2[0][None, :], w_gate_up[0].astype(BF16),
                     w_down[0].astype(BF16), final_norm[None, :])
```

```python
import numpy as np
import jax
import jax.numpy as jnp
from jax import lax
from jax.experimental import pallas as pl
from jax.experimental.pallas import tpu as pltpu

F32 = jnp.float32
BF16 = jnp.bfloat16

D_MODEL = 1024
HEAD_DIM = 64
N_Q_HEADS = 8
N_KV_HEADS = 2
Q_PER_KV = N_Q_HEADS // N_KV_HEADS
ATTN_WIDTH = N_Q_HEADS * HEAD_DIM
KV_WIDTH = N_KV_HEADS * HEAD_DIM
WINDOW = 128
BLOCK = 128
N_BUCKETS = 32
MAX_DISTANCE = 128

SSM_HEAD_DIM = 64
SSM_HEADS = 8
SSM_GROUPS = 2
HEADS_PER_GROUP = SSM_HEADS // SSM_GROUPS
SSM_WIDTH = SSM_HEADS * SSM_HEAD_DIM
GROUP_WIDTH = SSM_WIDTH // SSM_GROUPS
D_STATE = 128
CONV_K = 4
CHUNK = 128
XBC_WIDTH = SSM_WIDTH + 2 * SSM_GROUPS * D_STATE
D_FF = 2816
EPS = 1e-6
LOG2E = float(np.log2(np.e))

LANES = 128
SUBLANES = 8
VMEM_LIMIT_BYTES = 56 * 1024 * 1024

ADA_TN = 1536
INPROJ_TM = 1024
FFN_TM = 512
FFN_FC = 256
FFN_SUB = 1
MIX_CHUNKS = 8
MIX_ROWS = MIX_CHUNKS * CHUNK

Q_END = ATTN_WIDTH
K_END = Q_END + KV_WIDTH
V_END = K_END + KV_WIDTH
Z_END = V_END + SSM_WIDTH
XBC_END = Z_END + XBC_WIDTH

assert BLOCK == CHUNK == LANES and KV_WIDTH == LANES and 2 * HEAD_DIM == LANES


def _silu(x):
    return x * jax.nn.sigmoid(x)


def _rms(x):
    return x * lax.rsqrt(jnp.mean(x * x, axis=-1, keepdims=True) + EPS)


def _split2(x):
    hi = x.astype(BF16)
    lo = (x - hi.astype(F32)).astype(BF16)
    return hi, lo


def _dot(a, b):
    return jnp.dot(a, b, preferred_element_type=F32)


def _dot_nt(a, b):
    return lax.dot_general(a, b, (((1,), (1,)), ((), ())), preferred_element_type=F32)


def _dot_tn(a, b):
    return lax.dot_general(a, b, (((0,), (0,)), ((), ())), preferred_element_type=F32)


def _ada_kernel(c_ref, w_ref, b_ref, o_ref):
    cond = _silu(c_ref[...])
    o_ref[...] = _dot(cond.astype(BF16), w_ref[...].astype(BF16)) + b_ref[...]


def _ada_call(c_pad, ada_w, ada_b):
    n = ada_w.shape[1]
    return pl.pallas_call(
        _ada_kernel,
        grid=(n // ADA_TN,),
        in_specs=[
            pl.BlockSpec((SUBLANES, D_MODEL), lambda j: (0, 0)),
            pl.BlockSpec((D_MODEL, ADA_TN), lambda j: (0, j)),
            pl.BlockSpec((1, ADA_TN), lambda j: (0, j)),
        ],
        out_specs=pl.BlockSpec((SUBLANES, ADA_TN), lambda j: (0, j)),
        out_shape=jax.ShapeDtypeStruct((SUBLANES, n), F32),
        compiler_params=pltpu.CompilerParams(dimension_semantics=("parallel",)),
        name="ada_mod",
    )(c_pad, ada_w, ada_b)


def _t5_bucket_table():
    i = np.arange(BLOCK)[:, None]
    j = np.arange(BLOCK)[None, :]
    n = np.where(j <= i, i - j, i + BLOCK - j)
    assert WINDOW == BLOCK and n.min() >= 0 and n.max() < WINDOW
    max_exact = N_BUCKETS // 2
    large = max_exact + (np.log(np.maximum(n, 1) / max_exact) / np.log(MAX_DISTANCE / max_exact)
                         * (N_BUCKETS - max_exact)).astype(np.int32)
    large = np.minimum(large, N_BUCKETS - 1)
    bucket = np.where(n < max_exact, n, large).astype(np.int32)
    first = np.where(j <= i, bucket, -1)
    return np.stack([first, bucket]).astype(np.int32)


def _bias_kernel(idx_ref, rb_ref, o_ref):
    for v in range(2):
        idx = idx_ref[v]
        hits = [idx == b for b in range(N_BUCKETS)]
        for h in range(N_Q_HEADS):
            acc = jnp.full((BLOCK, BLOCK), -jnp.inf, F32)
            for b in range(N_BUCKETS):
                acc = jnp.where(hits[b], rb_ref[b, h] * LOG2E, acc)
            o_ref[v, h] = acc


def _bias_call(rel_bias):
    idx = jnp.asarray(_t5_bucket_table())
    return pl.pallas_call(
        _bias_kernel,
        in_specs=[
            pl.BlockSpec(memory_space=pltpu.VMEM),
            pl.BlockSpec(memory_space=pltpu.SMEM),
        ],
        out_specs=pl.BlockSpec(memory_space=pltpu.VMEM),
        out_shape=jax.ShapeDtypeStruct((2, N_Q_HEADS, BLOCK, BLOCK), F32),
        name="t5_bias",
    )(idx, rel_bias)


def _inproj_norm(x_ref, mod_ref, n1_ref, hb_ref):
    gain = n1_ref[...] * (1.0 + mod_ref[1:2, :])
    hb_ref[...] = (_rms(x_ref[...]) * gain + mod_ref[0:1, :]).astype(BF16)


def _inproj_project(hb_ref, w_ref, wdt_ref, cw_ref, cb_ref, q_ref, kt_ref, v_ref, gz_ref, act_ref, dt_ref, xe_ref):
    tm = hb_ref.shape[0]
    proj = lambda lo, hi: _dot(hb_ref[...], w_ref[:, lo:hi])
    tile = 2 * LANES
    for i, c0 in enumerate(range(0, XBC_WIDTH, tile)):
        cols = slice(c0, c0 + tile)
        xbc = proj(Z_END + c0, Z_END + c0 + tile)
        xe = jnp.concatenate([xe_ref[:, cols], xbc], axis=0)
        conv = cw_ref[CONV_K - 1:CONV_K, cols] * xbc + cb_ref[:, cols]
        for k in range(CONV_K - 1):
            conv = conv + cw_ref[k:k + 1, cols] * pltpu.roll(xe, CONV_K - 1 - k, axis=0)[SUBLANES:, :]
        xe_ref[:, cols] = xbc[tm - SUBLANES:, :]
        act_ref[:, cols] = _silu(conv).astype(BF16)
        if i < Q_END // tile:
            q_ref[:, i * tile:(i + 1) * tile] = proj(i * tile, (i + 1) * tile).astype(BF16)
        else:
            z0 = (i - Q_END // tile) * tile
            gz_ref[:, z0:z0 + tile] = _silu(proj(V_END + z0, V_END + z0 + tile)).astype(BF16)
    kt_ref[...] = proj(Q_END, K_END).T.astype(BF16)
    v_ref[...] = proj(K_END, V_END).astype(BF16)
    dt_ref[...] = _dot(hb_ref[...], wdt_ref[...])


def _inproj_kernel(x_ref, mod_ref, n1_ref, w_ref, wdt_ref, cw_ref, cb_ref, q_ref, kt_ref, v_ref, gz_ref, act_ref,
                   dt_ref, hb_ref, xe_ref):
    @pl.when(pl.program_id(1) == 0)
    def _():
        xe_ref[...] = jnp.zeros_like(xe_ref)

    _inproj_norm(x_ref, mod_ref, n1_ref, hb_ref)
    _inproj_project(hb_ref, w_ref, wdt_ref, cw_ref, cb_ref, q_ref, kt_ref, v_ref, gz_ref, act_ref, dt_ref, xe_ref)


def _inproj_call(x, mod, norm1, w_main, w_dt, conv_w, conv_b):
    b, s, _ = x.shape
    tm = INPROJ_TM
    row = lambda width: pl.BlockSpec((None, tm, width), lambda i, j: (i, j, 0))
    const = lambda shape: pl.BlockSpec(shape, lambda i, j: (0,) * len(shape))
    return pl.pallas_call(
        _inproj_kernel,
        grid=(b, s // tm),
        in_specs=[
            row(D_MODEL),
            pl.BlockSpec((None, 6, D_MODEL), lambda i, j: (i, 0, 0)),
            const((1, D_MODEL)),
            const((D_MODEL, XBC_END)),
            const((D_MODEL, LANES)),
            const((CONV_K, XBC_WIDTH)), const((1, XBC_WIDTH)),
        ],
        out_specs=[row(ATTN_WIDTH), pl.BlockSpec((None, KV_WIDTH, tm), lambda i, j: (i, 0, j)), row(KV_WIDTH),
                   row(SSM_WIDTH), row(XBC_WIDTH), row(LANES)],
        out_shape=[
            jax.ShapeDtypeStruct((b, s, ATTN_WIDTH), BF16),
            jax.ShapeDtypeStruct((b, KV_WIDTH, s), BF16),
            jax.ShapeDtypeStruct((b, s, KV_WIDTH), BF16),
            jax.ShapeDtypeStruct((b, s, SSM_WIDTH), BF16),
            jax.ShapeDtypeStruct((b, s, XBC_WIDTH), BF16),
            jax.ShapeDtypeStruct((b, s, LANES), F32),
        ],
        scratch_shapes=[
            pltpu.VMEM((tm, D_MODEL), BF16),
            pltpu.VMEM((SUBLANES, XBC_WIDTH), F32),
        ],
        compiler_params=pltpu.CompilerParams(
            dimension_semantics=("parallel", "arbitrary"), vmem_limit_bytes=VMEM_LIMIT_BYTES),
        name="in_proj",
    )(x, mod, norm1, w_main, w_dt, conv_w, conv_b)


def _stage_keys_values(kt_ref, ktp_ref, v_ref, vp_ref, ka_ref, vv_ref):
    zeros = jnp.zeros((HEAD_DIM, BLOCK + MIX_ROWS), BF16)
    for g in range(N_KV_HEADS):
        rows = slice(g * HEAD_DIM, (g + 1) * HEAD_DIM)
        for par in range(2):
            live = slice(par * HEAD_DIM, (par + 1) * HEAD_DIM)
            dead = slice((1 - par) * HEAD_DIM, (2 - par) * HEAD_DIM)
            ka_ref[g, par, live, 0:BLOCK] = ktp_ref[rows, :]
            ka_ref[g, par, live, BLOCK:] = kt_ref[rows, :]
            ka_ref[g, par, dead, :] = zeros

    def duplicate_halves(v):
        v32 = pltpu.bitcast(v, jnp.uint32)
        swapped = pltpu.roll(v32, HEAD_DIM, axis=1)
        lane = lax.broadcasted_iota(jnp.int32, v32.shape, 1)
        first = lane < HEAD_DIM
        return (pltpu.bitcast(jnp.where(first, v32, swapped), BF16),
                pltpu.bitcast(jnp.where(first, swapped, v32), BF16))

    for src, r0, r1 in ((vp_ref, 0, BLOCK), (v_ref, BLOCK, BLOCK + MIX_ROWS)):
        for g, dup in enumerate(duplicate_halves(src[...])):
            vv_ref[g, r0:r1, 0:LANES] = dup


def _scores(q_ref, ka_ref, j):
    rows = pl.ds(j * BLOCK, BLOCK)
    win = pl.ds(j * BLOCK, 2 * BLOCK)
    out = []
    for h in range(N_Q_HEADS):
        t, par = h // 2, h % 2
        out.append(_dot(q_ref[rows, t * LANES:(t + 1) * LANES], ka_ref[t // 2, par, :, win]))
    return out


def _softmax(scores, bias_ref, sinks_ref, own):
    s = [jnp.where(own, sc[:, BLOCK:], sc[:, :BLOCK]) + bias_ref[h] for h, sc in enumerate(scores)]
    sink = [sinks_ref[h] * LOG2E for h in range(N_Q_HEADS)]
    m = [jnp.maximum(jnp.max(s[h], axis=-1, keepdims=True), sink[h]) for h in range(N_Q_HEADS)]
    p = [jnp.exp2(s[h] - m[h]) for h in range(N_Q_HEADS)]
    pcat = [jnp.concatenate([jnp.where(own, 0.0, ph), jnp.where(own, ph, 0.0)], axis=1).astype(BF16) for ph in p]
    sink_term = [jnp.exp2(sink[h] - m[h]) for h in range(N_Q_HEADS)]
    return pcat, sink_term


def _values(pcat, vv_ref, j):
    win = pl.ds(j * BLOCK, 2 * BLOCK)
    return [_dot(pcat[h], vv_ref[h // Q_PER_KV, win, :]) for h in range(N_Q_HEADS)]


def _finish_attention(o2, sink_term, an_ref, o_ref, j):
    rows = pl.ds(j * BLOCK, BLOCK)
    heads = [o2[h][:, :LANES] / (o2[h][:, LANES:] + sink_term[h]) for h in range(N_Q_HEADS)]
    lane = lax.broadcasted_iota(jnp.int32, (BLOCK, LANES), 1)
    tiles = [jnp.where(lane < HEAD_DIM, heads[2 * t], heads[2 * t + 1]) for t in range(N_Q_HEADS // 2)]
    ssq = sum(jnp.sum(tl * tl, axis=-1, keepdims=True) for tl in tiles)
    inv = lax.rsqrt(ssq * (1.0 / ATTN_WIDTH) + EPS)
    for t, tl in enumerate(tiles):
        cols = slice(t * LANES, (t + 1) * LANES)
        o_ref[rows, cols] = (tl * inv * an_ref[:, cols]).astype(BF16)


def _expand_heads(vals, expand):
    hi, lo = _split2(vals)
    return _dot(hi, expand) + _dot(lo, expand)


def _mixer_kernel(q_ref, kt_ref, ktp_ref, v_ref, vp_ref, gz_ref, act_ref, dt_ref, bias_ref, sinks_ref,
                  dtb_ref, alog_ref, dskip_ref, an_ref, sn_ref, expand_ref, o_ref,
                  state_ref, ka_ref, vv_ref, xbd_ref):
    step = pl.program_id(1)
    nch = MIX_CHUNKS

    @pl.when(step == 0)
    def _():
        state_ref[...] = jnp.zeros_like(state_ref)
        vv_ref[:, :, LANES:] = jnp.ones((N_KV_HEADS, BLOCK + MIX_ROWS, LANES), BF16)

    row_i = lax.broadcasted_iota(jnp.int32, (BLOCK, BLOCK), 0)
    col_i = lax.broadcasted_iota(jnp.int32, (BLOCK, BLOCK), 1)
    own = col_i <= row_i
    bias_of = lambda j: bias_ref.at[jnp.minimum(step, 1)] if j == 0 else bias_ref.at[1]

    grp = lambda a, j, g: a[j * CHUNK:(j + 1) * CHUNK, g * GROUP_WIDTH:(g + 1) * GROUP_WIDTH]
    _stage_keys_values(kt_ref, ktp_ref, v_ref, vp_ref, ka_ref, vv_ref)
    scores = {0: _scores(q_ref, ka_ref, 0), 1: _scores(q_ref, ka_ref, 1)}
    blocks_done = [0]

    def attend(count):
        for j in range(blocks_done[0], blocks_done[0] + count):
            if j + 2 < nch:
                scores[j + 2] = _scores(q_ref, ka_ref, j + 2)
            pcat, sink_term = _softmax(scores.pop(j), bias_of(j), sinks_ref, own)
            _finish_attention(_values(pcat, vv_ref, j), sink_term, an_ref, o_ref, j)
        blocks_done[0] += count

    xs = act_ref[:, :SSM_WIDTH].astype(F32)
    bc_b = act_ref[:, SSM_WIDTH:]

    dt = jax.nn.softplus(dt_ref[...] + dtb_ref[...])
    dta = dt * (-LOG2E * jnp.exp(alog_ref[...]))
    tri = own.astype(BF16)
    dta_hi, dta_lo = _split2(dta)
    acs = [_dot(tri, dta_hi[j * CHUNK:(j + 1) * CHUNK]) + _dot(tri, dta_lo[j * CHUNK:(j + 1) * CHUNK])
           for j in range(nch)]

    attend(nch // 4)

    acs_t = [a.T for a in acs]
    dt_t = [dt[j * CHUNK:(j + 1) * CHUNK].T for j in range(nch)]
    dec_in = [jnp.exp2(a) for a in acs]
    dec_out = jnp.concatenate([jnp.exp2(a[CHUNK - 1:CHUNK, :] - a) for a in acs], axis=0)
    expand = expand_ref[...]
    state_in_x = _dot((dt * dec_out).astype(BF16), expand)
    decay_in_x = _dot(jnp.concatenate(dec_in, axis=0).astype(BF16), expand)
    chunk_decay = [_expand_heads(d[CHUNK - SUBLANES:, :], expand)[SUBLANES - 1:SUBLANES, :] for d in dec_in]

    attend(nch // 4)

    xdec_b = (xs * state_in_x).astype(BF16)
    lane_head = lax.broadcasted_iota(jnp.int32, (CHUNK, GROUP_WIDTH), 1) // SSM_HEAD_DIM
    for j in range(nch):
        for g in range(SSM_GROUPS):
            xg = act_ref[j * CHUNK:(j + 1) * CHUNK, g * GROUP_WIDTH:(g + 1) * GROUP_WIDTH]
            for r in range(HEADS_PER_GROUP):
                xbd_ref[j, g, r * CHUNK:(r + 1) * CHUNK, :] = jnp.where(lane_head == r, xg, jnp.zeros_like(xg))
    bm = [[bc_b[j * CHUNK:(j + 1) * CHUNK, g * D_STATE:(g + 1) * D_STATE] for g in range(SSM_GROUPS)]
          for j in range(nch)]
    cm = [[bc_b[j * CHUNK:(j + 1) * CHUNK, (SSM_GROUPS + g) * D_STATE:(SSM_GROUPS + g + 1) * D_STATE]
           for g in range(SSM_GROUPS)] for j in range(nch)]
    cbm = [[_dot_nt(cm[j][g], bm[j][g]) for g in range(SSM_GROUPS)] for j in range(nch)]
    d_state = [[_dot_tn(bm[j][g], grp(xdec_b, j, g)) for g in range(SSM_GROUPS)] for j in range(nch)]

    attend(nch // 4)

    def intra_chunk(j):
        per_group = []
        for g in range(SSM_GROUPS):
            w = []
            for r in range(HEADS_PER_GROUP):
                h = g * HEADS_PER_GROUP + r
                seg = acs[j][:, h:h + 1] - acs_t[j][h:h + 1, :]
                lmat = jnp.exp2(jnp.where(own, seg, -jnp.inf))
                w.append(cbm[j][g] * lmat * dt_t[j][h:h + 1, :])
            per_group.append(_dot(jnp.concatenate(w[:2], axis=1).astype(BF16), xbd_ref[j, g, 0:2 * CHUNK, :])
                             + _dot(jnp.concatenate(w[2:], axis=1).astype(BF16), xbd_ref[j, g, 2 * CHUNK:, :]))
        return per_group

    y_diag = [intra_chunk(j) for j in range(nch)]

    attend(nch - blocks_done[0])

    y_rows = []
    for j in range(nch):
        per_group = []
        for g in range(SSM_GROUPS):
            prev = state_ref[g]
            y_off = _dot(cm[j][g], prev.astype(BF16)) * grp(decay_in_x, j, g)
            state_ref[g] = prev * chunk_decay[j][:, g * GROUP_WIDTH:(g + 1) * GROUP_WIDTH] + d_state[j][g]
            per_group.append(y_diag[j][g] + y_off)
        y_rows.append(jnp.concatenate(per_group, axis=-1))
    y = jnp.concatenate(y_rows, axis=0) + dskip_ref[...] * xs
    y = y * gz_ref[...].astype(F32)
    y = jnp.concatenate([_rms(y[:, :GROUP_WIDTH]), _rms(y[:, GROUP_WIDTH:])], axis=-1)
    o_ref[:, ATTN_WIDTH:] = (y * sn_ref[...]).astype(BF16)


def _mixer_call(q, kt, v, gz, act, dt, bias, sinks, dt_bias, a_log, d_skip_x, attn_norm, ssm_norm, expand):
    b, s, _ = q.shape
    rows = MIX_ROWS
    row = lambda width: pl.BlockSpec((None, rows, width), lambda i, c: (i, c, 0))
    const = lambda shape: pl.BlockSpec(shape, lambda i, c: (0,) * len(shape))
    prev_blk = lambda i, c: jnp.maximum(c * MIX_CHUNKS - 1, 0)
    return pl.pallas_call(
        _mixer_kernel,
        grid=(b, s // rows),
        in_specs=[
            row(ATTN_WIDTH),
            pl.BlockSpec((None, KV_WIDTH, rows), lambda i, c: (i, 0, c)),
            pl.BlockSpec((None, KV_WIDTH, BLOCK), lambda i, c: (i, 0, prev_blk(i, c))),
            row(KV_WIDTH),
            pl.BlockSpec((None, BLOCK, KV_WIDTH), lambda i, c: (i, prev_blk(i, c), 0)),
            row(SSM_WIDTH), row(XBC_WIDTH), row(LANES),
            const((2, N_Q_HEADS, BLOCK, BLOCK)),
            pl.BlockSpec(memory_space=pltpu.SMEM),
            const((1, LANES)), const((1, LANES)),
            const((1, SSM_WIDTH)), const((1, ATTN_WIDTH)), const((1, SSM_WIDTH)),
            const((LANES, SSM_WIDTH)),
        ],
        out_specs=row(ATTN_WIDTH + SSM_WIDTH),
        out_shape=jax.ShapeDtypeStruct((b, s, ATTN_WIDTH + SSM_WIDTH), BF16),
        scratch_shapes=[
            pltpu.VMEM((SSM_GROUPS, D_STATE, GROUP_WIDTH), F32),
            pltpu.VMEM((N_KV_HEADS, 2, KV_WIDTH, BLOCK + rows), BF16),
            pltpu.VMEM((N_KV_HEADS, BLOCK + rows, 2 * LANES), BF16),
            pltpu.VMEM((MIX_CHUNKS, SSM_GROUPS, HEADS_PER_GROUP * CHUNK, GROUP_WIDTH), BF16),
        ],
        compiler_params=pltpu.CompilerParams(
            dimension_semantics=("parallel", "arbitrary"), vmem_limit_bytes=VMEM_LIMIT_BYTES),
        name="mixer",
    )(q, kt, kt, v, v, gz, act, dt, bias, sinks, dt_bias, a_log, d_skip_x, attn_norm, ssm_norm, expand)


def _ffn_kernel(mix_ref, x_ref, mod_ref, wo_ref, n2_ref, wgu_ref, wd_ref, fn_ref, o_ref, h_ref, acc_ref):
    sub = x_ref.shape[0] // FFN_SUB
    blocks = [pl.ds(i * sub, sub) for i in range(FFN_SUB)]
    gain2 = n2_ref[...] * (1.0 + mod_ref[4:5, :])
    for rows in blocks:
        x1 = x_ref[rows, :] + mod_ref[2:3, :] * _dot(mix_ref[rows, :], wo_ref[...])
        h_ref[rows, :] = (_rms(x1) * gain2 + mod_ref[3:4, :]).astype(BF16)
        o_ref[rows, :] = x1

    n_slices = D_FF // FFN_FC
    for j in range(n_slices):
        c0 = j * FFN_FC
        for rows in blocks:
            hb = h_ref[rows, :]
            g = _dot(hb, wgu_ref[:, c0:c0 + FFN_FC])
            u = _dot(hb, wgu_ref[:, D_FF + c0:D_FF + c0 + FFN_FC])
            d = _dot((_silu(g) * u).astype(BF16), wd_ref[c0:c0 + FFN_FC, :])
            if j == 0:
                acc_ref[rows, :] = d
            elif j < n_slices - 1:
                acc_ref[rows, :] += d
            else:
                o_ref[rows, :] = _rms(o_ref[rows, :] + mod_ref[5:6, :] * (acc_ref[rows, :] + d)) * fn_ref[...]


def _ffn_call(mix, x, mod, w_o, norm2, w_gu, w_down, final_norm):
    b, s, _ = x.shape
    tm = FFN_TM
    row = lambda: pl.BlockSpec((None, tm, D_MODEL), lambda i, j: (i, j, 0))
    const = lambda shape: pl.BlockSpec(shape, lambda i, j: (0,) * len(shape), pipeline_mode=pl.Buffered(1))
    return pl.pallas_call(
        _ffn_kernel,
        grid=(b, s // tm),
        in_specs=[
            row(), row(),
            pl.BlockSpec((None, 6, D_MODEL), lambda i, j: (i, 0, 0)),
            const((D_MODEL, D_MODEL)), const((1, D_MODEL)),
            const((D_MODEL, 2 * D_FF)), const((D_FF, D_MODEL)), const((1, D_MODEL)),
        ],
        out_specs=row(),
        out_shape=jax.ShapeDtypeStruct((b, s, D_MODEL), F32),
        scratch_shapes=[pltpu.VMEM((tm, D_MODEL), BF16), pltpu.VMEM((tm, D_MODEL), F32)],
        compiler_params=pltpu.CompilerParams(
            dimension_semantics=("parallel", "parallel"), vmem_limit_bytes=VMEM_LIMIT_BYTES),
        name="ffn",
    )(mix, x, mod, w_o, norm2, w_gu, w_down, final_norm)


def _head_expand_matrix():
    e = np.zeros((LANES, SSM_WIDTH), np.float32)
    for h in range(SSM_HEADS):
        e[h, h * SSM_HEAD_DIM:(h + 1) * SSM_HEAD_DIM] = 1.0
    return e


def kernel(x, c, ada_w, ada_b, norm1, w_in, conv_w, conv_b, dt_bias, A_log, D_skip, sinks, attn_out_norm,
           ssm_out_norm, w_o, norm2, w_gate_up, w_down, rel_bias, final_norm):
    assert ada_w.shape[0] == 1, "the final RMSNorm is fused into the (single) layer's FFN call"
    b = x.shape[0]
    bias = _bias_call(rel_bias)
    expand = jnp.asarray(_head_expand_matrix(), BF16)
    pad_heads = lambda v: jnp.pad(v, (0, LANES - SSM_HEADS))[None, :]
    c_pad = jnp.pad(c, ((0, SUBLANES - b), (0, 0)))
    mod = _ada_call(c_pad, ada_w[0], ada_b[0][None, :])[:b].reshape(b, 6, D_MODEL)
    w_q = w_in[0][:, :Q_END] * (HEAD_DIM ** -0.5 * LOG2E)
    w_main = jnp.concatenate([w_q, w_in[0][:, Q_END:XBC_END]], axis=1).astype(BF16)
    w_dt = jnp.pad(w_in[0][:, XBC_END:], ((0, 0), (0, LANES - SSM_HEADS))).astype(BF16)
    q, kt, v, gz, act, dt = _inproj_call(x, mod, norm1[0][None, :], w_main, w_dt, conv_w[0], conv_b[0][None, :])
    mix = _mixer_call(
        q, kt, v, gz, act, dt, bias, sinks[0], pad_heads(dt_bias[0]), pad_heads(A_log[0]),
        jnp.repeat(D_skip[0], SSM_HEAD_DIM)[None, :], attn_out_norm[0][None, :], ssm_out_norm[0][None, :], expand)
    return _ffn_call(mix, x, mod, w_o[0].astype(BF16), norm2[0][None, :], w_gate_up[0].astype(BF16),
                     w_down[0].astype(BF16), final_norm[None, :])
```

```python
import numpy as np
import jax
import jax.numpy as jnp
from jax import lax
from jax.experimental import pallas as pl
from jax.experimental.pallas import tpu as pltpu

F32 = jnp.float32
BF16 = jnp.bfloat16

D_MODEL = 1024
HEAD_DIM = 64
N_Q_HEADS = 8
N_KV_HEADS = 2
Q_PER_KV = N_Q_HEADS // N_KV_HEADS
ATTN_WIDTH = N_Q_HEADS * HEAD_DIM
KV_WIDTH = N_KV_HEADS * HEAD_DIM
WINDOW = 128
BLOCK = 128
N_BUCKETS = 32
MAX_DISTANCE = 128

SSM_HEAD_DIM = 64
SSM_HEADS = 8
SSM_GROUPS = 2
HEADS_PER_GROUP = SSM_HEADS // SSM_GROUPS
SSM_WIDTH = SSM_HEADS * SSM_HEAD_DIM
GROUP_WIDTH = SSM_WIDTH // SSM_GROUPS
D_STATE = 128
CONV_K = 4
CHUNK = 128
XBC_WIDTH = SSM_WIDTH + 2 * SSM_GROUPS * D_STATE
D_FF = 2816
EPS = 1e-6
LOG2E = float(np.log2(np.e))

LANES = 128
SUBLANES = 8
VMEM_LIMIT_BYTES = 56 * 1024 * 1024

ADA_TN = 1536
INPROJ_TM = 1024
FFN_TM = 512
FFN_TILES = 2
FFN_FC = 256
FFN_NEXT_PROLOGUE_AT = 4
MIX_CHUNKS = 8
MIX_ROWS = MIX_CHUNKS * CHUNK

Q_END = ATTN_WIDTH
K_END = Q_END + KV_WIDTH
V_END = K_END + KV_WIDTH
Z_END = V_END + SSM_WIDTH
XBC_END = Z_END + XBC_WIDTH

assert BLOCK == CHUNK == LANES and KV_WIDTH == LANES and 2 * HEAD_DIM == LANES


def _silu(x):
    return x * jax.nn.sigmoid(x)


def _rms(x):
    return x * lax.rsqrt(jnp.mean(x * x, axis=-1, keepdims=True) + EPS)


def _split2(x):
    hi = x.astype(BF16)
    lo = (x - hi.astype(F32)).astype(BF16)
    return hi, lo


def _dot(a, b):
    return jnp.dot(a, b, preferred_element_type=F32)


def _dot_nt(a, b):
    return lax.dot_general(a, b, (((1,), (1,)), ((), ())), preferred_element_type=F32)


def _dot_tn(a, b):
    return lax.dot_general(a, b, (((0,), (0,)), ((), ())), preferred_element_type=F32)


def _ada_kernel(c_ref, w_ref, b_ref, o_ref):
    cond = _silu(c_ref[...])
    o_ref[...] = _dot(cond.astype(BF16), w_ref[...].astype(BF16)) + b_ref[...]


def _ada_call(c_pad, ada_w, ada_b):
    n = ada_w.shape[1]
    return pl.pallas_call(
        _ada_kernel,
        grid=(n // ADA_TN,),
        in_specs=[
            pl.BlockSpec((SUBLANES, D_MODEL), lambda j: (0, 0)),
            pl.BlockSpec((D_MODEL, ADA_TN), lambda j: (0, j)),
            pl.BlockSpec((1, ADA_TN), lambda j: (0, j)),
        ],
        out_specs=pl.BlockSpec((SUBLANES, ADA_TN), lambda j: (0, j)),
        out_shape=jax.ShapeDtypeStruct((SUBLANES, n), F32),
        compiler_params=pltpu.CompilerParams(dimension_semantics=("parallel",)),
        name="ada_mod",
    )(c_pad, ada_w, ada_b)


def _t5_bucket_table():
    i = np.arange(BLOCK)[:, None]
    j = np.arange(BLOCK)[None, :]
    n = np.where(j <= i, i - j, i + BLOCK - j)
    assert WINDOW == BLOCK and n.min() >= 0 and n.max() < WINDOW
    max_exact = N_BUCKETS // 2
    large = max_exact + (np.log(np.maximum(n, 1) / max_exact) / np.log(MAX_DISTANCE / max_exact)
                         * (N_BUCKETS - max_exact)).astype(np.int32)
    large = np.minimum(large, N_BUCKETS - 1)
    bucket = np.where(n < max_exact, n, large).astype(np.int32)
    first = np.where(j <= i, bucket, -1)
    return np.stack([first, bucket]).astype(np.int32)


def _bias_kernel(idx_ref, rb_ref, o_ref):
    for v in range(2):
        idx = idx_ref[v]
        hits = [idx == b for b in range(N_BUCKETS)]
        for h in range(N_Q_HEADS):
            acc = jnp.full((BLOCK, BLOCK), -jnp.inf, F32)
            for b in range(N_BUCKETS):
                acc = jnp.where(hits[b], rb_ref[b, h] * LOG2E, acc)
            o_ref[v, h] = acc


def _bias_call(rel_bias):
    idx = jnp.asarray(_t5_bucket_table())
    return pl.pallas_call(
        _bias_kernel,
        in_specs=[
            pl.BlockSpec(memory_space=pltpu.VMEM),
            pl.BlockSpec(memory_space=pltpu.SMEM),
        ],
        out_specs=pl.BlockSpec(memory_space=pltpu.VMEM),
        out_shape=jax.ShapeDtypeStruct((2, N_Q_HEADS, BLOCK, BLOCK), F32),
        name="t5_bias",
    )(idx, rel_bias)


def _inproj_norm(x_ref, mod_ref, n1_ref, hb_ref):
    gain = n1_ref[...] * (1.0 + mod_ref[1:2, :])
    hb_ref[...] = (_rms(x_ref[...]) * gain + mod_ref[0:1, :]).astype(BF16)


def _inproj_project(hb_ref, w_ref, wdt_ref, cw_ref, cb_ref, q_ref, kt_ref, v_ref, gz_ref, act_ref, dt_ref, xe_ref):
    tm = hb_ref.shape[0]
    proj = lambda lo, hi: _dot(hb_ref[...], w_ref[:, lo:hi])
    tile = 2 * LANES
    for i, c0 in enumerate(range(0, XBC_WIDTH, tile)):
        cols = slice(c0, c0 + tile)
        xbc = proj(Z_END + c0, Z_END + c0 + tile)
        xe = jnp.concatenate([xe_ref[:, cols], xbc], axis=0)
        conv = cw_ref[CONV_K - 1:CONV_K, cols] * xbc + cb_ref[:, cols]
        for k in range(CONV_K - 1):
            conv = conv + cw_ref[k:k + 1, cols] * pltpu.roll(xe, CONV_K - 1 - k, axis=0)[SUBLANES:, :]
        xe_ref[:, cols] = xbc[tm - SUBLANES:, :]
        act_ref[:, cols] = _silu(conv).astype(BF16)
        if i < Q_END // tile:
            q_ref[:, i * tile:(i + 1) * tile] = proj(i * tile, (i + 1) * tile).astype(BF16)
        else:
            z0 = (i - Q_END // tile) * tile
            gz_ref[:, z0:z0 + tile] = _silu(proj(V_END + z0, V_END + z0 + tile)).astype(BF16)
    kt_ref[...] = proj(Q_END, K_END).T.astype(BF16)
    v_ref[...] = proj(K_END, V_END).astype(BF16)
    dt_ref[...] = _dot(hb_ref[...], wdt_ref[...])


def _inproj_kernel(x_ref, mod_ref, n1_ref, w_ref, wdt_ref, cw_ref, cb_ref, q_ref, kt_ref, v_ref, gz_ref, act_ref,
                   dt_ref, hb_ref, xe_ref):
    @pl.when(pl.program_id(1) == 0)
    def _():
        xe_ref[...] = jnp.zeros_like(xe_ref)

    _inproj_norm(x_ref, mod_ref, n1_ref, hb_ref)
    _inproj_project(hb_ref, w_ref, wdt_ref, cw_ref, cb_ref, q_ref, kt_ref, v_ref, gz_ref, act_ref, dt_ref, xe_ref)


def _inproj_call(x, mod, norm1, w_main, w_dt, conv_w, conv_b):
    b, s, _ = x.shape
    tm = INPROJ_TM
    row = lambda width: pl.BlockSpec((None, tm, width), lambda i, j: (i, j, 0))
    const = lambda shape: pl.BlockSpec(shape, lambda i, j: (0,) * len(shape))
    return pl.pallas_call(
        _inproj_kernel,
        grid=(b, s // tm),
        in_specs=[
            row(D_MODEL),
            pl.BlockSpec((None, 6, D_MODEL), lambda i, j: (i, 0, 0)),
            const((1, D_MODEL)),
            const((D_MODEL, XBC_END)),
            const((D_MODEL, LANES)),
            const((CONV_K, XBC_WIDTH)), const((1, XBC_WIDTH)),
        ],
        out_specs=[row(ATTN_WIDTH), pl.BlockSpec((None, KV_WIDTH, tm), lambda i, j: (i, 0, j)), row(KV_WIDTH),
                   row(SSM_WIDTH), row(XBC_WIDTH), row(LANES)],
        out_shape=[
            jax.ShapeDtypeStruct((b, s, ATTN_WIDTH), BF16),
            jax.ShapeDtypeStruct((b, KV_WIDTH, s), BF16),
            jax.ShapeDtypeStruct((b, s, KV_WIDTH), BF16),
            jax.ShapeDtypeStruct((b, s, SSM_WIDTH), BF16),
            jax.ShapeDtypeStruct((b, s, XBC_WIDTH), BF16),
            jax.ShapeDtypeStruct((b, s, LANES), F32),
        ],
        scratch_shapes=[
            pltpu.VMEM((tm, D_MODEL), BF16),
            pltpu.VMEM((SUBLANES, XBC_WIDTH), F32),
        ],
        compiler_params=pltpu.CompilerParams(
            dimension_semantics=("parallel", "arbitrary"), vmem_limit_bytes=VMEM_LIMIT_BYTES),
        name="in_proj",
    )(x, mod, norm1, w_main, w_dt, conv_w, conv_b)


def _stage_keys_values(kt_ref, ktp_ref, v_ref, vp_ref, ka_ref, vv_ref):
    zeros = jnp.zeros((HEAD_DIM, BLOCK + MIX_ROWS), BF16)
    for g in range(N_KV_HEADS):
        rows = slice(g * HEAD_DIM, (g + 1) * HEAD_DIM)
        for par in range(2):
            live = slice(par * HEAD_DIM, (par + 1) * HEAD_DIM)
            dead = slice((1 - par) * HEAD_DIM, (2 - par) * HEAD_DIM)
            ka_ref[g, par, live, 0:BLOCK] = ktp_ref[rows, :]
            ka_ref[g, par, live, BLOCK:] = kt_ref[rows, :]
            ka_ref[g, par, dead, :] = zeros

    def duplicate_halves(v):
        v32 = pltpu.bitcast(v, jnp.uint32)
        swapped = pltpu.roll(v32, HEAD_DIM, axis=1)
        lane = lax.broadcasted_iota(jnp.int32, v32.shape, 1)
        first = lane < HEAD_DIM
        return (pltpu.bitcast(jnp.where(first, v32, swapped), BF16),
                pltpu.bitcast(jnp.where(first, swapped, v32), BF16))

    for src, r0, r1 in ((vp_ref, 0, BLOCK), (v_ref, BLOCK, BLOCK + MIX_ROWS)):
        for g, dup in enumerate(duplicate_halves(src[...])):
            vv_ref[g, r0:r1, 0:LANES] = dup


def _scores(q_ref, ka_ref, j):
    rows = pl.ds(j * BLOCK, BLOCK)
    win = pl.ds(j * BLOCK, 2 * BLOCK)
    out = []
    for h in range(N_Q_HEADS):
        t, par = h // 2, h % 2
        out.append(_dot(q_ref[rows, t * LANES:(t + 1) * LANES], ka_ref[t // 2, par, :, win]))
    return out


def _softmax(scores, bias_ref, sinks_ref, own):
    s = [jnp.where(own, sc[:, BLOCK:], sc[:, :BLOCK]) + bias_ref[h] for h, sc in enumerate(scores)]
    sink = [sinks_ref[h] * LOG2E for h in range(N_Q_HEADS)]
    m = [jnp.maximum(jnp.max(s[h], axis=-1, keepdims=True), sink[h]) for h in range(N_Q_HEADS)]
    p = [jnp.exp2(s[h] - m[h]) for h in range(N_Q_HEADS)]
    pcat = [jnp.concatenate([jnp.where(own, 0.0, ph), jnp.where(own, ph, 0.0)], axis=1).astype(BF16) for ph in p]
    sink_term = [jnp.exp2(sink[h] - m[h]) for h in range(N_Q_HEADS)]
    return pcat, sink_term


def _values(pcat, vv_ref, j):
    win = pl.ds(j * BLOCK, 2 * BLOCK)
    return [_dot(pcat[h], vv_ref[h // Q_PER_KV, win, :]) for h in range(N_Q_HEADS)]


def _finish_attention(o2, sink_term, an_ref, o_ref, j):
    rows = pl.ds(j * BLOCK, BLOCK)
    heads = [o2[h][:, :LANES] / (o2[h][:, LANES:] + sink_term[h]) for h in range(N_Q_HEADS)]
    lane = lax.broadcasted_iota(jnp.int32, (BLOCK, LANES), 1)
    tiles = [jnp.where(lane < HEAD_DIM, heads[2 * t], heads[2 * t + 1]) for t in range(N_Q_HEADS // 2)]
    ssq = sum(jnp.sum(tl * tl, axis=-1, keepdims=True) for tl in tiles)
    inv = lax.rsqrt(ssq * (1.0 / ATTN_WIDTH) + EPS)
    for t, tl in enumerate(tiles):
        cols = slice(t * LANES, (t + 1) * LANES)
        o_ref[rows, cols] = (tl * inv * an_ref[:, cols]).astype(BF16)


def _expand_heads(vals, expand):
    hi, lo = _split2(vals)
    return _dot(hi, expand) + _dot(lo, expand)


def _mixer_kernel(q_ref, kt_ref, ktp_ref, v_ref, vp_ref, gz_ref, act_ref, dt_ref, bias_ref, sinks_ref,
                  dtb_ref, alog_ref, dskip_ref, an_ref, sn_ref, expand_ref, o_ref,
                  state_ref, ka_ref, vv_ref, xbd_ref):
    step = pl.program_id(1)
    nch = MIX_CHUNKS

    @pl.when(step == 0)
    def _():
        state_ref[...] = jnp.zeros_like(state_ref)
        vv_ref[:, :, LANES:] = jnp.ones((N_KV_HEADS, BLOCK + MIX_ROWS, LANES), BF16)

    row_i = lax.broadcasted_iota(jnp.int32, (BLOCK, BLOCK), 0)
    col_i = lax.broadcasted_iota(jnp.int32, (BLOCK, BLOCK), 1)
    own = col_i <= row_i
    bias_of = lambda j: bias_ref.at[jnp.minimum(step, 1)] if j == 0 else bias_ref.at[1]

    grp = lambda a, j, g: a[j * CHUNK:(j + 1) * CHUNK, g * GROUP_WIDTH:(g + 1) * GROUP_WIDTH]
    _stage_keys_values(kt_ref, ktp_ref, v_ref, vp_ref, ka_ref, vv_ref)
    scores = {0: _scores(q_ref, ka_ref, 0), 1: _scores(q_ref, ka_ref, 1)}
    blocks_done = [0]

    def attend(count):
        for j in range(blocks_done[0], blocks_done[0] + count):
            if j + 2 < nch:
                scores[j + 2] = _scores(q_ref, ka_ref, j + 2)
            pcat, sink_term = _softmax(scores.pop(j), bias_of(j), sinks_ref, own)
            _finish_attention(_values(pcat, vv_ref, j), sink_term, an_ref, o_ref, j)
        blocks_done[0] += count

    xs = act_ref[:, :SSM_WIDTH].astype(F32)
    bc_b = act_ref[:, SSM_WIDTH:]

    dt = jax.nn.softplus(dt_ref[...] + dtb_ref[...])
    dta = dt * (-LOG2E * jnp.exp(alog_ref[...]))
    tri = own.astype(BF16)
    dta_hi, dta_lo = _split2(dta)
    acs = [_dot(tri, dta_hi[j * CHUNK:(j + 1) * CHUNK]) + _dot(tri, dta_lo[j * CHUNK:(j + 1) * CHUNK])
           for j in range(nch)]

    attend(nch // 4)

    acs_t = [a.T for a in acs]
    dt_t = [dt[j * CHUNK:(j + 1) * CHUNK].T for j in range(nch)]
    dec_in = [jnp.exp2(a) for a in acs]
    dec_out = jnp.concatenate([jnp.exp2(a[CHUNK - 1:CHUNK, :] - a) for a in acs], axis=0)
    expand = expand_ref[...]
    state_in_x = _dot((dt * dec_out).astype(BF16), expand)
    decay_in_x = _dot(jnp.concatenate(dec_in, axis=0).astype(BF16), expand)
    chunk_decay = [_expand_heads(d[CHUNK - SUBLANES:, :], expand)[SUBLANES - 1:SUBLANES, :] for d in dec_in]

    attend(nch // 4)

    xdec_b = (xs * state_in_x).astype(BF16)
    lane_head = lax.broadcasted_iota(jnp.int32, (CHUNK, GROUP_WIDTH), 1) // SSM_HEAD_DIM
    for j in range(nch):
        for g in range(SSM_GROUPS):
            xg = act_ref[j * CHUNK:(j + 1) * CHUNK, g * GROUP_WIDTH:(g + 1) * GROUP_WIDTH]
            for r in range(HEADS_PER_GROUP):
                xbd_ref[j, g, r * CHUNK:(r + 1) * CHUNK, :] = jnp.where(lane_head == r, xg, jnp.zeros_like(xg))
    bm = [[bc_b[j * CHUNK:(j + 1) * CHUNK, g * D_STATE:(g + 1) * D_STATE] for g in range(SSM_GROUPS)]
          for j in range(nch)]
    cm = [[bc_b[j * CHUNK:(j + 1) * CHUNK, (SSM_GROUPS + g) * D_STATE:(SSM_GROUPS + g + 1) * D_STATE]
           for g in range(SSM_GROUPS)] for j in range(nch)]
    cbm = [[_dot_nt(cm[j][g], bm[j][g]) for g in range(SSM_GROUPS)] for j in range(nch)]
    d_state = [[_dot_tn(bm[j][g], grp(xdec_b, j, g)) for g in range(SSM_GROUPS)] for j in range(nch)]

    attend(nch // 4)

    def intra_chunk(j):
        per_group = []
        for g in range(SSM_GROUPS):
            w = []
            for r in range(HEADS_PER_GROUP):
                h = g * HEADS_PER_GROUP + r
                seg = acs[j][:, h:h + 1] - acs_t[j][h:h + 1, :]
                lmat = jnp.exp2(jnp.where(own, seg, -jnp.inf))
                w.append(cbm[j][g] * lmat * dt_t[j][h:h + 1, :])
            per_group.append(_dot(jnp.concatenate(w[:2], axis=1).astype(BF16), xbd_ref[j, g, 0:2 * CHUNK, :])
                             + _dot(jnp.concatenate(w[2:], axis=1).astype(BF16), xbd_ref[j, g, 2 * CHUNK:, :]))
        return per_group

    y_diag = [intra_chunk(j) for j in range(nch)]

    attend(nch - blocks_done[0])

    y_rows = []
    for j in range(nch):
        per_group = []
        for g in range(SSM_GROUPS):
            prev = state_ref[g]
            y_off = _dot(cm[j][g], prev.astype(BF16)) * grp(decay_in_x, j, g)
            state_ref[g] = prev * chunk_decay[j][:, g * GROUP_WIDTH:(g + 1) * GROUP_WIDTH] + d_state[j][g]
            per_group.append(y_diag[j][g] + y_off)
        y_rows.append(jnp.concatenate(per_group, axis=-1))
    y = jnp.concatenate(y_rows, axis=0) + dskip_ref[...] * xs
    y = y * gz_ref[...].astype(F32)
    y = jnp.concatenate([_rms(y[:, :GROUP_WIDTH]), _rms(y[:, GROUP_WIDTH:])], axis=-1)
    o_ref[:, ATTN_WIDTH:] = (y * sn_ref[...]).astype(BF16)


def _mixer_call(q, kt, v, gz, act, dt, bias, sinks, dt_bias, a_log, d_skip_x, attn_norm, ssm_norm, expand):
    b, s, _ = q.shape
    rows = MIX_ROWS
    row = lambda width: pl.BlockSpec((None, rows, width), lambda i, c: (i, c, 0))
    const = lambda shape: pl.BlockSpec(shape, lambda i, c: (0,) * len(shape))
    prev_blk = lambda i, c: jnp.maximum(c * MIX_CHUNKS - 1, 0)
    return pl.pallas_call(
        _mixer_kernel,
        grid=(b, s // rows),
        in_specs=[
            row(ATTN_WIDTH),
            pl.BlockSpec((None, KV_WIDTH, rows), lambda i, c: (i, 0, c)),
            pl.BlockSpec((None, KV_WIDTH, BLOCK), lambda i, c: (i, 0, prev_blk(i, c))),
            row(KV_WIDTH),
            pl.BlockSpec((None, BLOCK, KV_WIDTH), lambda i, c: (i, prev_blk(i, c), 0)),
            row(SSM_WIDTH), row(XBC_WIDTH), row(LANES),
            const((2, N_Q_HEADS, BLOCK, BLOCK)),
            pl.BlockSpec(memory_space=pltpu.SMEM),
            const((1, LANES)), const((1, LANES)),
            const((1, SSM_WIDTH)), const((1, ATTN_WIDTH)), const((1, SSM_WIDTH)),
            const((LANES, SSM_WIDTH)),
        ],
        out_specs=row(ATTN_WIDTH + SSM_WIDTH),
        out_shape=jax.ShapeDtypeStruct((b, s, ATTN_WIDTH + SSM_WIDTH), BF16),
        scratch_shapes=[
            pltpu.VMEM((SSM_GROUPS, D_STATE, GROUP_WIDTH), F32),
            pltpu.VMEM((N_KV_HEADS, 2, KV_WIDTH, BLOCK + rows), BF16),
            pltpu.VMEM((N_KV_HEADS, BLOCK + rows, 2 * LANES), BF16),
            pltpu.VMEM((MIX_CHUNKS, SSM_GROUPS, HEADS_PER_GROUP * CHUNK, GROUP_WIDTH), BF16),
        ],
        compiler_params=pltpu.CompilerParams(
            dimension_semantics=("parallel", "arbitrary"), vmem_limit_bytes=VMEM_LIMIT_BYTES),
        name="mixer",
    )(q, kt, kt, v, v, gz, act, dt, bias, sinks, dt_bias, a_log, d_skip_x, attn_norm, ssm_norm, expand)


def _ffn_kernel(mix_ref, x_ref, mod_ref, wo_ref, n2_ref, wgu_ref, wd_ref, fn_ref, o_ref, h_ref, acc_ref):
    tm = x_ref.shape[0] // FFN_TILES
    tiles = [pl.ds(i * tm, tm) for i in range(FFN_TILES)]
    gain2 = n2_ref[...] * (1.0 + mod_ref[4:5, :])
    n_slices = D_FF // FFN_FC

    def prologue(rows):
        x1 = x_ref[rows, :] + mod_ref[2:3, :] * _dot(mix_ref[rows, :], wo_ref[...])
        h_ref[rows, :] = (_rms(x1) * gain2 + mod_ref[3:4, :]).astype(BF16)
        o_ref[rows, :] = x1

    def ffn_slice(rows, j):
        c0 = j * FFN_FC
        hb = h_ref[rows, :]
        g = _dot(hb, wgu_ref[:, c0:c0 + FFN_FC])
        u = _dot(hb, wgu_ref[:, D_FF + c0:D_FF + c0 + FFN_FC])
        d = _dot((_silu(g) * u).astype(BF16), wd_ref[c0:c0 + FFN_FC, :])
        if j == 0:
            acc_ref[rows, :] = d
        elif j < n_slices - 1:
            acc_ref[rows, :] += d
        else:
            o_ref[rows, :] = _rms(o_ref[rows, :] + mod_ref[5:6, :] * (acc_ref[rows, :] + d)) * fn_ref[...]

    prologue(tiles[0])
    for i, rows in enumerate(tiles):
        for j in range(n_slices):
            if j == FFN_NEXT_PROLOGUE_AT and i + 1 < FFN_TILES:
                prologue(tiles[i + 1])
            ffn_slice(rows, j)


def _ffn_call(mix, x, mod, w_o, norm2, w_gu, w_down, final_norm):
    b, s, _ = x.shape
    tm = FFN_TM * FFN_TILES
    row = lambda: pl.BlockSpec((None, tm, D_MODEL), lambda i, j: (i, j, 0))
    const = lambda shape: pl.BlockSpec(shape, lambda i, j: (0,) * len(shape), pipeline_mode=pl.Buffered(1))
    return pl.pallas_call(
        _ffn_kernel,
        grid=(b, s // tm),
        in_specs=[
            row(), row(),
            pl.BlockSpec((None, 6, D_MODEL), lambda i, j: (i, 0, 0)),
            const((D_MODEL, D_MODEL)), const((1, D_MODEL)),
            const((D_MODEL, 2 * D_FF)), const((D_FF, D_MODEL)), const((1, D_MODEL)),
        ],
        out_specs=row(),
        out_shape=jax.ShapeDtypeStruct((b, s, D_MODEL), F32),
        scratch_shapes=[pltpu.VMEM((tm, D_MODEL), BF16), pltpu.VMEM((tm, D_MODEL), F32)],
        compiler_params=pltpu.CompilerParams(
            dimension_semantics=("parallel", "parallel"), vmem_limit_bytes=VMEM_LIMIT_BYTES),
        name="ffn",
    )(mix, x, mod, w_o, norm2, w_gu, w_down, final_norm)


def _head_expand_matrix():
    e = np.zeros((LANES, SSM_WIDTH), np.float32)
    for h in range(SSM_HEADS):
        e[h, h * SSM_HEAD_DIM:(h + 1) * SSM_HEAD_DIM] = 1.0
    return e


def kernel(x, c, ada_w, ada_b, norm1, w_in, conv_w, conv_b, dt_bias, A_log, D_skip, sinks, attn_out_norm,
           ssm_out_norm, w_o, norm2, w_gate_up, w_down, rel_bias, final_norm):
    assert ada_w.shape[0] == 1, "the final RMSNorm is fused into the (single) layer's FFN call"
    b = x.shape[0]
    bias = _bias_call(rel_bias)
    expand = jnp.asarray(_head_expand_matrix(), BF16)
    pad_heads = lambda v: jnp.pad(v, (0, LANES - SSM_HEADS))[None, :]
    c_pad = jnp.pad(c, ((0, SUBLANES - b), (0, 0)))
    mod = _ada_call(c_pad, ada_w[0], ada_b[0][None, :])[:b].reshape(b, 6, D_MODEL)
    w_q = w_in[0][:, :Q_END] * (HEAD_DIM ** -0.5 * LOG2E)
    w_main = jnp.concatenate([w_q, w_in[0][:, Q_END:XBC_END]], axis=1).astype(BF16)
    w_dt = jnp.pad(w_in[0][:, XBC_END:], ((0, 0), (0, LANES - SSM_HEADS))).astype(BF16)
    q, kt, v, gz, act, dt = _inproj_call(x, mod, norm1[0][None, :], w_main, w_dt, conv_w[0], conv_b[0][None, :])
    mix = _mixer_call(
        q, kt, v, gz, act, dt, bias, sinks[0], pad_heads(dt_bias[0]), pad_heads(A_log[0]),
        jnp.repeat(D_skip[0], SSM_HEAD_DIM)[None, :], attn_out_norm[0][None, :], ssm_out_norm[0][None, :], expand)
    return _ffn_call(mix, x, mod, w_o[0].astype(BF16), norm2[0][None, :], w_gate_up[0].astype(BF16),
                     w_down[0].astype(BF16), final_norm[None, :])
```

```python
import itertools

import numpy as np
import jax
import jax.numpy as jnp
from jax import lax
from jax.experimental import pallas as pl
from jax.experimental.pallas import tpu as pltpu

F32 = jnp.float32
BF16 = jnp.bfloat16

D_MODEL = 1024
HEAD_DIM = 64
N_Q_HEADS = 8
N_KV_HEADS = 2
Q_PER_KV = N_Q_HEADS // N_KV_HEADS
ATTN_WIDTH = N_Q_HEADS * HEAD_DIM
KV_WIDTH = N_KV_HEADS * HEAD_DIM
WINDOW = 128
BLOCK = 128
N_BUCKETS = 32
MAX_DISTANCE = 128

SSM_HEAD_DIM = 64
SSM_HEADS = 8
SSM_GROUPS = 2
HEADS_PER_GROUP = SSM_HEADS // SSM_GROUPS
SSM_WIDTH = SSM_HEADS * SSM_HEAD_DIM
GROUP_WIDTH = SSM_WIDTH // SSM_GROUPS
D_STATE = 128
CONV_K = 4
CHUNK = 128
XBC_WIDTH = SSM_WIDTH + 2 * SSM_GROUPS * D_STATE
D_FF = 2816
EPS = 1e-6
LOG2E = float(np.log2(np.e))

LANES = 128
SUBLANES = 8
VMEM_LIMIT_BYTES = 56 * 1024 * 1024

ADA_TN = 1536
INPROJ_TM = 1024
FFN_FC = 256
MIX_CHUNKS = 4
MIXER_LEAD_PHASES = 2
MIX_ROWS = MIX_CHUNKS * CHUNK

Q_END = ATTN_WIDTH
K_END = Q_END + KV_WIDTH
V_END = K_END + KV_WIDTH
Z_END = V_END + SSM_WIDTH
XBC_END = Z_END + XBC_WIDTH

assert BLOCK == CHUNK == LANES and KV_WIDTH == LANES and 2 * HEAD_DIM == LANES


def _silu(x):
    return x * jax.nn.sigmoid(x)


def _rms(x):
    return x * lax.rsqrt(jnp.mean(x * x, axis=-1, keepdims=True) + EPS)


def _split2(x):
    hi = x.astype(BF16)
    lo = (x - hi.astype(F32)).astype(BF16)
    return hi, lo


def _dot(a, b):
    return jnp.dot(a, b, preferred_element_type=F32)


def _dot_nt(a, b):
    return lax.dot_general(a, b, (((1,), (1,)), ((), ())), preferred_element_type=F32)


def _dot_tn(a, b):
    return lax.dot_general(a, b, (((0,), (0,)), ((), ())), preferred_element_type=F32)


def _ada_kernel(c_ref, w_ref, b_ref, o_ref):
    cond = _silu(c_ref[...])
    o_ref[...] = _dot(cond.astype(BF16), w_ref[...].astype(BF16)) + b_ref[...]


def _ada_call(c_pad, ada_w, ada_b):
    n = ada_w.shape[1]
    return pl.pallas_call(
        _ada_kernel,
        grid=(n // ADA_TN,),
        in_specs=[
            pl.BlockSpec((SUBLANES, D_MODEL), lambda j: (0, 0)),
            pl.BlockSpec((D_MODEL, ADA_TN), lambda j: (0, j)),
            pl.BlockSpec((1, ADA_TN), lambda j: (0, j)),
        ],
        out_specs=pl.BlockSpec((SUBLANES, ADA_TN), lambda j: (0, j)),
        out_shape=jax.ShapeDtypeStruct((SUBLANES, n), F32),
        compiler_params=pltpu.CompilerParams(dimension_semantics=("parallel",)),
        name="ada_mod",
    )(c_pad, ada_w, ada_b)


def _t5_bucket_table():
    i = np.arange(BLOCK)[:, None]
    j = np.arange(BLOCK)[None, :]
    n = np.where(j <= i, i - j, i + BLOCK - j)
    assert WINDOW == BLOCK and n.min() >= 0 and n.max() < WINDOW
    max_exact = N_BUCKETS // 2
    large = max_exact + (np.log(np.maximum(n, 1) / max_exact) / np.log(MAX_DISTANCE / max_exact)
                         * (N_BUCKETS - max_exact)).astype(np.int32)
    large = np.minimum(large, N_BUCKETS - 1)
    bucket = np.where(n < max_exact, n, large).astype(np.int32)
    first = np.where(j <= i, bucket, -1)
    return np.stack([first, bucket]).astype(np.int32)


def _bias_kernel(idx_ref, rb_ref, o_ref):
    for v in range(2):
        idx = idx_ref[v]
        hits = [idx == b for b in range(N_BUCKETS)]
        for h in range(N_Q_HEADS):
            acc = jnp.full((BLOCK, BLOCK), -jnp.inf, F32)
            for b in range(N_BUCKETS):
                acc = jnp.where(hits[b], rb_ref[b, h] * LOG2E, acc)
            o_ref[v, h] = acc


def _bias_call(rel_bias):
    idx = jnp.asarray(_t5_bucket_table())
    return pl.pallas_call(
        _bias_kernel,
        in_specs=[
            pl.BlockSpec(memory_space=pltpu.VMEM),
            pl.BlockSpec(memory_space=pltpu.SMEM),
        ],
        out_specs=pl.BlockSpec(memory_space=pltpu.VMEM),
        out_shape=jax.ShapeDtypeStruct((2, N_Q_HEADS, BLOCK, BLOCK), F32),
        name="t5_bias",
    )(idx, rel_bias)


def _inproj_norm(x_ref, mod_ref, n1_ref, hb_ref):
    gain = n1_ref[...] * (1.0 + mod_ref[1:2, :])
    hb_ref[...] = (_rms(x_ref[...]) * gain + mod_ref[0:1, :]).astype(BF16)


def _inproj_project(hb_ref, w_ref, wdt_ref, cw_ref, cb_ref, q_ref, kt_ref, v_ref, gz_ref, act_ref, dt_ref, xe_ref):
    tm = hb_ref.shape[0]
    proj = lambda lo, hi: _dot(hb_ref[...], w_ref[:, lo:hi])
    tile = 2 * LANES
    for i, c0 in enumerate(range(0, XBC_WIDTH, tile)):
        cols = slice(c0, c0 + tile)
        xbc = proj(Z_END + c0, Z_END + c0 + tile)
        xe = jnp.concatenate([xe_ref[:, cols], xbc], axis=0)
        xe = xe.reshape(tm // SUBLANES + 1, SUBLANES, tile)
        first_row = lax.broadcasted_iota(jnp.int32, xe.shape, 1) == 0

        def shift_down_one(a):
            rot = pltpu.roll(a, 1, axis=1)
            return jnp.where(first_row, jnp.concatenate([rot[:1], rot[:-1]], axis=0), rot)

        taps = cw_ref[0:1, cols] * xe
        for k in range(1, CONV_K):
            taps = shift_down_one(taps) + cw_ref[k:k + 1, cols] * xe
        conv = taps[1:].reshape(tm, tile) + cb_ref[:, cols]
        xe_ref[:, cols] = xbc[tm - SUBLANES:, :]
        act_ref[:, cols] = _silu(conv).astype(BF16)
        if i < Q_END // tile:
            q_ref[:, i * tile:(i + 1) * tile] = proj(i * tile, (i + 1) * tile).astype(BF16)
        else:
            z0 = (i - Q_END // tile) * tile
            gz_ref[:, z0:z0 + tile] = _silu(proj(V_END + z0, V_END + z0 + tile)).astype(BF16)
    kt_ref[...] = proj(Q_END, K_END).T.astype(BF16)
    v_ref[...] = proj(K_END, V_END).astype(BF16)
    dt_ref[...] = _dot(hb_ref[...], wdt_ref[...])


def _inproj_kernel(x_ref, mod_ref, n1_ref, w_ref, wdt_ref, cw_ref, cb_ref, q_ref, kt_ref, v_ref, gz_ref, act_ref,
                   dt_ref, hb_ref, xe_ref):
    @pl.when(pl.program_id(1) == 0)
    def _():
        xe_ref[...] = jnp.zeros_like(xe_ref)

    _inproj_norm(x_ref, mod_ref, n1_ref, hb_ref)
    _inproj_project(hb_ref, w_ref, wdt_ref, cw_ref, cb_ref, q_ref, kt_ref, v_ref, gz_ref, act_ref, dt_ref, xe_ref)


def _inproj_call(x, mod, norm1, w_main, w_dt, conv_w, conv_b):
    b, s, _ = x.shape
    tm = INPROJ_TM
    row = lambda width: pl.BlockSpec((None, tm, width), lambda i, j: (i, j, 0))
    const = lambda shape: pl.BlockSpec(shape, lambda i, j: (0,) * len(shape))
    return pl.pallas_call(
        _inproj_kernel,
        grid=(b, s // tm),
        in_specs=[
            row(D_MODEL),
            pl.BlockSpec((None, 6, D_MODEL), lambda i, j: (i, 0, 0)),
            const((1, D_MODEL)),
            const((D_MODEL, XBC_END)),
            const((D_MODEL, LANES)),
            const((CONV_K, XBC_WIDTH)), const((1, XBC_WIDTH)),
        ],
        out_specs=[row(ATTN_WIDTH), pl.BlockSpec((None, KV_WIDTH, tm), lambda i, j: (i, 0, j)), row(KV_WIDTH),
                   row(SSM_WIDTH), row(XBC_WIDTH), row(LANES)],
        out_shape=[
            jax.ShapeDtypeStruct((b, s, ATTN_WIDTH), BF16),
            jax.ShapeDtypeStruct((b, KV_WIDTH, s), BF16),
            jax.ShapeDtypeStruct((b, s, KV_WIDTH), BF16),
            jax.ShapeDtypeStruct((b, s, SSM_WIDTH), BF16),
            jax.ShapeDtypeStruct((b, s, XBC_WIDTH), BF16),
            jax.ShapeDtypeStruct((b, s, LANES), F32),
        ],
        scratch_shapes=[
            pltpu.VMEM((tm, D_MODEL), BF16),
            pltpu.VMEM((SUBLANES, XBC_WIDTH), F32),
        ],
        compiler_params=pltpu.CompilerParams(
            dimension_semantics=("parallel", "arbitrary"), vmem_limit_bytes=VMEM_LIMIT_BYTES),
        name="in_proj",
    )(x, mod, norm1, w_main, w_dt, conv_w, conv_b)


def _stage_keys_values(kt_ref, ktp_ref, v_ref, vp_ref, ka_ref, vv_ref):
    zeros = jnp.zeros((HEAD_DIM, BLOCK + MIX_ROWS), BF16)
    for g in range(N_KV_HEADS):
        rows = slice(g * HEAD_DIM, (g + 1) * HEAD_DIM)
        for par in range(2):
            live = slice(par * HEAD_DIM, (par + 1) * HEAD_DIM)
            dead = slice((1 - par) * HEAD_DIM, (2 - par) * HEAD_DIM)
            ka_ref[g, par, live, 0:BLOCK] = ktp_ref[rows, :]
            ka_ref[g, par, live, BLOCK:] = kt_ref[rows, :]
            ka_ref[g, par, dead, :] = zeros

    def duplicate_halves(v):
        v32 = pltpu.bitcast(v, jnp.uint32)
        swapped = pltpu.roll(v32, HEAD_DIM, axis=1)
        lane = lax.broadcasted_iota(jnp.int32, v32.shape, 1)
        first = lane < HEAD_DIM
        return (pltpu.bitcast(jnp.where(first, v32, swapped), BF16),
                pltpu.bitcast(jnp.where(first, swapped, v32), BF16))

    for src, r0, r1 in ((vp_ref, 0, BLOCK), (v_ref, BLOCK, BLOCK + MIX_ROWS)):
        for g, dup in enumerate(duplicate_halves(src[...])):
            vv_ref[g, r0:r1, 0:LANES] = dup


def _scores(q_ref, ka_ref, j):
    rows = pl.ds(j * BLOCK, BLOCK)
    win = pl.ds(j * BLOCK, 2 * BLOCK)
    out = []
    for h in range(N_Q_HEADS):
        t, par = h // 2, h % 2
        out.append(_dot(q_ref[rows, t * LANES:(t + 1) * LANES], ka_ref[t // 2, par, :, win]))
    return out


def _softmax(scores, bias_ref, sinks_ref, own):
    s = [jnp.where(own, sc[:, BLOCK:], sc[:, :BLOCK]) + bias_ref[h] for h, sc in enumerate(scores)]
    sink = [sinks_ref[h] * LOG2E for h in range(N_Q_HEADS)]
    m = [jnp.maximum(jnp.max(s[h], axis=-1, keepdims=True), sink[h]) for h in range(N_Q_HEADS)]
    p = [jnp.exp2(s[h] - m[h]) for h in range(N_Q_HEADS)]
    pcat = [jnp.concatenate([jnp.where(own, 0.0, ph), jnp.where(own, ph, 0.0)], axis=1).astype(BF16) for ph in p]
    sink_term = [jnp.exp2(sink[h] - m[h]) for h in range(N_Q_HEADS)]
    return pcat, sink_term


def _values(pcat, vv_ref, j):
    win = pl.ds(j * BLOCK, 2 * BLOCK)
    return [_dot(pcat[h], vv_ref[h // Q_PER_KV, win, :]) for h in range(N_Q_HEADS)]


def _finish_attention(o2, sink_term, an_ref, o_ref, j):
    rows = pl.ds(j * BLOCK, BLOCK)
    heads = [o2[h][:, :LANES] / (o2[h][:, LANES:] + sink_term[h]) for h in range(N_Q_HEADS)]
    lane = lax.broadcasted_iota(jnp.int32, (BLOCK, LANES), 1)
    tiles = [jnp.where(lane < HEAD_DIM, heads[2 * t], heads[2 * t + 1]) for t in range(N_Q_HEADS // 2)]
    ssq = sum(jnp.sum(tl * tl, axis=-1, keepdims=True) for tl in tiles)
    inv = lax.rsqrt(ssq * (1.0 / ATTN_WIDTH) + EPS)
    for t, tl in enumerate(tiles):
        cols = slice(t * LANES, (t + 1) * LANES)
        o_ref[rows, cols] = (tl * inv * an_ref[:, cols]).astype(BF16)


def _expand_heads(vals, expand):
    hi, lo = _split2(vals)
    return _dot(hi, expand) + _dot(lo, expand)


def _mixer_steps(first_tile, q_ref, kt_ref, ktp_ref, v_ref, vp_ref, gz_ref, act_ref, dt_ref, bias_ref, sinks_ref,
                 dtb_ref, alog_ref, dskip_ref, an_ref, sn_ref, expand_ref, o_ref, state_ref, ka_ref, vv_ref,
                 xbd_ref):
    nch = MIX_CHUNKS
    row_i = lax.broadcasted_iota(jnp.int32, (BLOCK, BLOCK), 0)
    col_i = lax.broadcasted_iota(jnp.int32, (BLOCK, BLOCK), 1)
    own = col_i <= row_i
    bias_of = lambda j: bias_ref.at[0 if (first_tile and j == 0) else 1]

    grp = lambda a, j, g: a[j * CHUNK:(j + 1) * CHUNK, g * GROUP_WIDTH:(g + 1) * GROUP_WIDTH]
    _stage_keys_values(kt_ref, ktp_ref, v_ref, vp_ref, ka_ref, vv_ref)
    scores = {0: _scores(q_ref, ka_ref, 0), 1: _scores(q_ref, ka_ref, 1)}
    blocks_done = [0]

    def attend(count):
        for j in range(blocks_done[0], blocks_done[0] + count):
            if j + 2 < nch:
                scores[j + 2] = _scores(q_ref, ka_ref, j + 2)
            pcat, sink_term = _softmax(scores.pop(j), bias_of(j), sinks_ref, own)
            _finish_attention(_values(pcat, vv_ref, j), sink_term, an_ref, o_ref, j)
        blocks_done[0] += count

    xs = act_ref[:, :SSM_WIDTH].astype(F32)
    bc_b = act_ref[:, SSM_WIDTH:]

    dt = jax.nn.softplus(dt_ref[...] + dtb_ref[...])
    dta = dt * (-LOG2E * jnp.exp(alog_ref[...]))
    tri = own.astype(BF16)
    dta_hi, dta_lo = _split2(dta)
    acs = [_dot(tri, dta_hi[j * CHUNK:(j + 1) * CHUNK]) + _dot(tri, dta_lo[j * CHUNK:(j + 1) * CHUNK])
           for j in range(nch)]
    yield
    attend(nch // 4)
    yield

    acs_t =[a.T for a in acs]
    dt_t = [dt[j * CHUNK:(j + 1) * CHUNK].T for j in range(nch)]
    dec_in = [jnp.exp2(a) for a in acs]
    dec_out = jnp.concatenate([jnp.exp2(a[CHUNK - 1:CHUNK, :] - a) for a in acs], axis=0)
    expand = expand_ref[...]
    state_in_x = _dot((dt * dec_out).astype(BF16), expand)
    decay_in_x = _dot(jnp.concatenate(dec_in, axis=0).astype(BF16), expand)
    chunk_decay = [_expand_heads(d[CHUNK - SUBLANES:, :], expand)[SUBLANES - 1:SUBLANES, :] for d in dec_in]
    yield
    attend(nch // 4)
    yield

    xdec_b = (xs * state_in_x).astype(BF16)
    lane_head = lax.broadcasted_iota(jnp.int32, (CHUNK, GROUP_WIDTH), 1) // SSM_HEAD_DIM
    for j in range(nch):
        for g in range(SSM_GROUPS):
            xg = act_ref[j * CHUNK:(j + 1) * CHUNK, g * GROUP_WIDTH:(g + 1) * GROUP_WIDTH]
            for r in range(HEADS_PER_GROUP):
                xbd_ref[j, g, r * CHUNK:(r + 1) * CHUNK, :] = jnp.where(lane_head == r, xg, jnp.zeros_like(xg))
    bm = [[bc_b[j * CHUNK:(j + 1) * CHUNK, g * D_STATE:(g + 1) * D_STATE] for g in range(SSM_GROUPS)]
          for j in range(nch)]
    cm = [[bc_b[j * CHUNK:(j + 1) * CHUNK, (SSM_GROUPS + g) * D_STATE:(SSM_GROUPS + g + 1) * D_STATE]
           for g in range(SSM_GROUPS)] for j in range(nch)]
    cbm = [[_dot_nt(cm[j][g], bm[j][g]) for g in range(SSM_GROUPS)] for j in range(nch)]
    d_state = [[_dot_tn(bm[j][g], grp(xdec_b, j, g)) for g in range(SSM_GROUPS)] for j in range(nch)]
    yield
    attend(nch // 4)
    yield

    def intra_chunk(j):
        per_group = []
        for g in range(SSM_GROUPS):
            w = []
            for r in range(HEADS_PER_GROUP):
                h = g * HEADS_PER_GROUP + r
                seg = acs[j][:, h:h + 1] - acs_t[j][h:h + 1, :]
                lmat = jnp.exp2(jnp.where(own, seg, -jnp.inf))
                w.append(cbm[j][g] * lmat * dt_t[j][h:h + 1, :])
            per_group.append(_dot(jnp.concatenate(w[:2], axis=1).astype(BF16), xbd_ref[j, g, 0:2 * CHUNK, :])
                             + _dot(jnp.concatenate(w[2:], axis=1).astype(BF16), xbd_ref[j, g, 2 * CHUNK:, :]))
        return per_group

    y_diag = []
    for j in range(nch):
        y_diag.append(intra_chunk(j))
        yield
    attend(nch - blocks_done[0])
    yield

    y_rows = []
    for j in range(nch):
        per_group = []
        for g in range(SSM_GROUPS):
            prev = state_ref[g]
            y_off = _dot(cm[j][g], prev.astype(BF16)) * grp(decay_in_x, j, g)
            state_ref[g] = prev * chunk_decay[j][:, g * GROUP_WIDTH:(g + 1) * GROUP_WIDTH] + d_state[j][g]
            per_group.append(y_diag[j][g] + y_off)
        y_rows.append(jnp.concatenate(per_group, axis=-1))
    yield
    y = jnp.concatenate(y_rows, axis=0) + dskip_ref[...] * xs
    y = y * gz_ref[...].astype(F32)
    y = jnp.concatenate([_rms(y[:, :GROUP_WIDTH]), _rms(y[:, GROUP_WIDTH:])], axis=-1)
    o_ref[:, ATTN_WIDTH:] = (y * sn_ref[...]).astype(BF16)


def _ffn_steps(mix_ref, x_ref, mod_ref, wo_ref, n2_ref, wgu_ref, wd_ref, fn_ref, o_ref, h_ref, acc_ref):
    gain2 = n2_ref[...] * (1.0 + mod_ref[4:5, :])
    x1 = x_ref[...] + mod_ref[2:3, :] * _dot(mix_ref[...], wo_ref[...])
    h_ref[...] = (_rms(x1) * gain2 + mod_ref[3:4, :]).astype(BF16)
    o_ref[...] = x1
    yield
    n_slices = D_FF // FFN_FC
    for j in range(n_slices):
        c0 = j * FFN_FC
        hb = h_ref[...]
        g = _dot(hb, wgu_ref[:, c0:c0 + FFN_FC])
        u = _dot(hb, wgu_ref[:, D_FF + c0:D_FF + c0 + FFN_FC])
        d = _dot((_silu(g) * u).astype(BF16), wd_ref[c0:c0 + FFN_FC, :])
        if j == 0:
            acc_ref[...] = d
        elif j < n_slices - 1:
            acc_ref[...] += d
        else:
            o_ref[...] = _rms(o_ref[...] + mod_ref[5:6, :] * (acc_ref[...] + d)) * fn_ref[...]
        yield


def _mixer_ffn_kernel(q_ref, kt_ref, ktp_ref, v_ref, vp_ref, gz_ref, act_ref, dt_ref, bias_ref, sinks_ref, dtb_ref,
                      alog_ref, dskip_ref, an_ref, sn_ref, expand_ref, x_ref, mod_ref, wo_ref, n2_ref, wgu_ref,
                      wd_ref, fn_ref, o_ref, state_ref, ka_ref, vv_ref, xbd_ref, mix_ref, h_ref, acc_ref):
    s = pl.program_id(1)
    last = pl.num_programs(1) - 1
    mixer = lambda first: _mixer_steps(first, q_ref, kt_ref, ktp_ref, v_ref, vp_ref, gz_ref, act_ref, dt_ref,
                                       bias_ref, sinks_ref, dtb_ref, alog_ref, dskip_ref, an_ref, sn_ref,
                                       expand_ref, mix_ref, state_ref, ka_ref, vv_ref, xbd_ref)
    ffn = lambda: _ffn_steps(mix_ref, x_ref, mod_ref, wo_ref, n2_ref, wgu_ref, wd_ref, fn_ref, o_ref, h_ref, acc_ref)

    @pl.when(s == 0)
    def _():
        state_ref[...] = jnp.zeros_like(state_ref)
        vv_ref[:, :, LANES:] = jnp.ones((N_KV_HEADS, BLOCK + MIX_ROWS, LANES), BF16)
        for _ in mixer(True):
            pass

    @pl.when((s > 0) & (s < last))
    def _():
        ffn_steps, mixer_steps = ffn(), mixer(False)
        next(ffn_steps)
        for _ in range(MIXER_LEAD_PHASES):
            next(mixer_steps)
        slices_left = D_FF // FFN_FC
        for _ in mixer_steps:
            if slices_left > 1:
                next(ffn_steps)
                slices_left -= 1
        for _ in ffn_steps:
            pass

    @pl.when(s == last)
    def _():
        for _ in ffn():
            pass


def _mixer_ffn_call(q, kt, v, gz, act, dt, bias, sinks, dt_bias, a_log, d_skip_x, attn_norm, ssm_norm, expand,
                    x, mod, w_o, norm2, w_gu, w_down, final_norm):
    b, s, _ = q.shape
    rows = MIX_ROWS
    n_tiles = s // rows
    mix_tile = lambda i, c: jnp.minimum(c, n_tiles - 1)
    ffn_tile = lambda i, c: jnp.maximum(c - 1, 0)
    prev_blk = lambda i, c: jnp.maximum(mix_tile(i, c) * MIX_CHUNKS - 1, 0)
    mrow = lambda width: pl.BlockSpec((None, rows, width), lambda i, c: (i, mix_tile(i, c), 0))
    frow = lambda: pl.BlockSpec((None, rows, D_MODEL), lambda i, c: (i, ffn_tile(i, c), 0))
    const = lambda shape: pl.BlockSpec(shape, lambda i, c: (0,) * len(shape))
    resident = lambda shape: pl.BlockSpec(shape, lambda i, c: (0,) * len(shape), pipeline_mode=pl.Buffered(1))
    return pl.pallas_call(
        _mixer_ffn_kernel,
        grid=(b, n_tiles + 1),
        in_specs=[
            mrow(ATTN_WIDTH),
            pl.BlockSpec((None, KV_WIDTH, rows), lambda i, c: (i, 0, mix_tile(i, c))),
            pl.BlockSpec((None, KV_WIDTH, BLOCK), lambda i, c: (i, 0, prev_blk(i, c))),
            mrow(KV_WIDTH),
            pl.BlockSpec((None, BLOCK, KV_WIDTH), lambda i, c: (i, prev_blk(i, c), 0)),
            mrow(SSM_WIDTH), mrow(XBC_WIDTH), mrow(LANES),
            const((2, N_Q_HEADS, BLOCK, BLOCK)),
            pl.BlockSpec(memory_space=pltpu.SMEM),
            const((1, LANES)), const((1, LANES)),
            const((1, SSM_WIDTH)), const((1, ATTN_WIDTH)), const((1, SSM_WIDTH)),
            const((LANES, SSM_WIDTH)),
            frow(),
            pl.BlockSpec((None, 6, D_MODEL), lambda i, c: (i, 0, 0)),
            resident((D_MODEL, D_MODEL)), const((1, D_MODEL)),
            resident((D_MODEL, 2 * D_FF)), resident((D_FF, D_MODEL)), const((1, D_MODEL)),
        ],
        out_specs=frow(),
        out_shape=jax.ShapeDtypeStruct((b, s, D_MODEL), F32),
        scratch_shapes=[
            pltpu.VMEM((SSM_GROUPS, D_STATE, GROUP_WIDTH), F32),
            pltpu.VMEM((N_KV_HEADS, 2, KV_WIDTH, BLOCK + rows), BF16),
            pltpu.VMEM((N_KV_HEADS, BLOCK + rows, 2 * LANES), BF16),
            pltpu.VMEM((MIX_CHUNKS, SSM_GROUPS, HEADS_PER_GROUP * CHUNK, GROUP_WIDTH), BF16),
            pltpu.VMEM((rows, ATTN_WIDTH + SSM_WIDTH), BF16),
            pltpu.VMEM((rows, D_MODEL), BF16),
            pltpu.VMEM((rows, D_MODEL), F32),
        ],
        compiler_params=pltpu.CompilerParams(
            dimension_semantics=("parallel", "arbitrary"), vmem_limit_bytes=VMEM_LIMIT_BYTES),
        name="mixer_ffn",
    )(q, kt, kt, v, v, gz, act, dt, bias, sinks, dt_bias, a_log, d_skip_x, attn_norm, ssm_norm, expand,
      x, mod, w_o, norm2, w_gu, w_down, final_norm)


def _head_expand_matrix():
    e = np.zeros((LANES, SSM_WIDTH), np.float32)
    for h in range(SSM_HEADS):
        e[h, h * SSM_HEAD_DIM:(h + 1) * SSM_HEAD_DIM] = 1.0
    return e


def kernel(x, c, ada_w, ada_b, norm1, w_in, conv_w, conv_b, dt_bias, A_log, D_skip, sinks, attn_out_norm,
           ssm_out_norm, w_o, norm2, w_gate_up, w_down, rel_bias, final_norm):
    assert ada_w.shape[0] == 1, "the final RMSNorm is fused into the (single) layer's FFN call"
    b = x.shape[0]
    bias = _bias_call(rel_bias)
    expand = jnp.asarray(_head_expand_matrix(), BF16)
    pad_heads = lambda v: jnp.pad(v, (0, LANES - SSM_HEADS))[None, :]
    c_pad = jnp.pad(c, ((0, SUBLANES - b), (0, 0)))
    mod = _ada_call(c_pad, ada_w[0], ada_b[0][None, :])[:b].reshape(b, 6, D_MODEL)
    w_q = w_in[0][:, :Q_END] * (HEAD_DIM ** -0.5 * LOG2E)
    w_main = jnp.concatenate([w_q, w_in[0][:, Q_END:XBC_END]], axis=1).astype(BF16)
    w_dt = jnp.pad(w_in[0][:, XBC_END:], ((0, 0), (0, LANES - SSM_HEADS))).astype(BF16)
    q, kt, v, gz, act, dt = _inproj_call(x, mod, norm1[0][None, :], w_main, w_dt, conv_w[0], conv_b[0][None, :])
    return _mixer_ffn_call(
        q, kt, v, gz, act, dt, bias, sinks[0], pad_heads(dt_bias[0]), pad_heads(A_log[0]),
        jnp.repeat(D_skip[0], SSM_HEAD_DIM)[None, :], attn_out_norm[0][None, :], ssm_out_norm[0][None, :], expand,
        x, mod, w_o[0].astype(BF16), norm2[0][None, :], w_gate_up[0].astype(BF16), w_down[0].astype(BF16),
        final_norm[None, :])
```

```python
import numpy as np
import jax
import jax.numpy as jnp
from jax import lax
from jax.experimental import pallas as pl
from jax.experimental.pallas import tpu as pltpu

F32 = jnp.float32
BF16 = jnp.bfloat16

D_MODEL = 1024
HEAD_DIM = 64
N_Q_HEADS = 8
N_KV_HEADS = 2
Q_PER_KV = N_Q_HEADS // N_KV_HEADS
ATTN_WIDTH = N_Q_HEADS * HEAD_DIM
KV_WIDTH = N_KV_HEADS * HEAD_DIM
WINDOW = 128
BLOCK = 128
N_BUCKETS = 32
MAX_DISTANCE = 128

SSM_HEAD_DIM = 64
SSM_HEADS = 8
SSM_GROUPS = 2
HEADS_PER_GROUP = SSM_HEADS // SSM_GROUPS
SSM_WIDTH = SSM_HEADS * SSM_HEAD_DIM
GROUP_WIDTH = SSM_WIDTH // SSM_GROUPS
D_STATE = 128
CONV_K = 4
CHUNK = 128
XBC_WIDTH = SSM_WIDTH + 2 * SSM_GROUPS * D_STATE
D_FF = 2816
EPS = 1e-6
LOG2E = float(np.log2(np.e))

LANES = 128
SUBLANES = 8
VMEM_LIMIT_BYTES = 56 * 1024 * 1024

ADA_TN = 1536
INPROJ_TM = 1024
FFN_FC = 256
MIX_CHUNKS = 4
MIX_ROWS = MIX_CHUNKS * CHUNK

Q_END = ATTN_WIDTH
K_END = Q_END + KV_WIDTH
V_END = K_END + KV_WIDTH
Z_END = V_END + SSM_WIDTH
XBC_END = Z_END + XBC_WIDTH

assert BLOCK == CHUNK == LANES and KV_WIDTH == LANES and 2 * HEAD_DIM == LANES


def _silu(x):
    return x * jax.nn.sigmoid(x)


def _rms(x):
    return x * lax.rsqrt(jnp.mean(x * x, axis=-1, keepdims=True) + EPS)


def _split2(x):
    hi = x.astype(BF16)
    lo = (x - hi.astype(F32)).astype(BF16)
    return hi, lo


def _dot(a, b):
    return jnp.dot(a, b, preferred_element_type=F32)


def _dot_nt(a, b):
    return lax.dot_general(a, b, (((1,), (1,)), ((), ())), preferred_element_type=F32)


def _dot_tn(a, b):
    return lax.dot_general(a, b, (((0,), (0,)), ((), ())), preferred_element_type=F32)


def _ada_kernel(c_ref, w_ref, b_ref, o_ref):
    cond = _silu(c_ref[...])
    o_ref[...] = _dot(cond.astype(BF16), w_ref[...].astype(BF16)) + b_ref[...]


def _ada_call(c_pad, ada_w, ada_b):
    n = ada_w.shape[1]
    return pl.pallas_call(
        _ada_kernel,
        grid=(n // ADA_TN,),
        in_specs=[
            pl.BlockSpec((SUBLANES, D_MODEL), lambda j: (0, 0)),
            pl.BlockSpec((D_MODEL, ADA_TN), lambda j: (0, j)),
            pl.BlockSpec((1, ADA_TN), lambda j: (0, j)),
        ],
        out_specs=pl.BlockSpec((SUBLANES, ADA_TN), lambda j: (0, j)),
        out_shape=jax.ShapeDtypeStruct((SUBLANES, n), F32),
        compiler_params=pltpu.CompilerParams(dimension_semantics=("parallel",)),
        name="ada_mod",
    )(c_pad, ada_w, ada_b)


def _t5_bucket_table():
    i = np.arange(BLOCK)[:, None]
    j = np.arange(BLOCK)[None, :]
    n = np.where(j <= i, i - j, i + BLOCK - j)
    assert WINDOW == BLOCK and n.min() >= 0 and n.max() < WINDOW
    max_exact = N_BUCKETS // 2
    large = max_exact + (np.log(np.maximum(n, 1) / max_exact) / np.log(MAX_DISTANCE / max_exact)
                         * (N_BUCKETS - max_exact)).astype(np.int32)
    large = np.minimum(large, N_BUCKETS - 1)
    bucket = np.where(n < max_exact, n, large).astype(np.int32)
    first = np.where(j <= i, bucket, -1)
    return np.stack([first, bucket]).astype(np.int32)


def _bias_kernel(idx_ref, rb_ref, o_ref):
    for v in range(2):
        idx = idx_ref[v]
        hits = [idx == b for b in range(N_BUCKETS)]
        for h in range(N_Q_HEADS):
            acc = jnp.full((BLOCK, BLOCK), -jnp.inf, F32)
            for b in range(N_BUCKETS):
                acc = jnp.where(hits[b], rb_ref[b, h] * LOG2E, acc)
            o_ref[v, h] = acc


def _bias_call(rel_bias):
    idx = jnp.asarray(_t5_bucket_table())
    return pl.pallas_call(
        _bias_kernel,
        in_specs=[
            pl.BlockSpec(memory_space=pltpu.VMEM),
            pl.BlockSpec(memory_space=pltpu.SMEM),
        ],
        out_specs=pl.BlockSpec(memory_space=pltpu.VMEM),
        out_shape=jax.ShapeDtypeStruct((2, N_Q_HEADS, BLOCK, BLOCK), F32),
        name="t5_bias",
    )(idx, rel_bias)


def _inproj_norm(x_ref, mod_ref, n1_ref, hb_ref):
    gain = n1_ref[...] * (1.0 + mod_ref[1:2, :])
    hb_ref[...] = (_rms(x_ref[...]) * gain + mod_ref[0:1, :]).astype(BF16)


def _inproj_project(hb_ref, w_ref, wdt_ref, cw_ref, cb_ref, q_ref, kt_ref, v_ref, gz_ref, act_ref, dt_ref, xe_ref):
    tm = hb_ref.shape[0]
    proj = lambda lo, hi: _dot(hb_ref[...], w_ref[:, lo:hi])
    tile = 2 * LANES
    for i, c0 in enumerate(range(0, XBC_WIDTH, tile)):
        cols = slice(c0, c0 + tile)
        xbc = proj(Z_END + c0, Z_END + c0 + tile)
        xe = jnp.concatenate([xe_ref[:, cols], xbc], axis=0)
        xe = xe.reshape(tm // SUBLANES + 1, SUBLANES, tile)
        first_row = lax.broadcasted_iota(jnp.int32, xe.shape, 1) == 0

        def shift_down_one(a):
            rot = pltpu.roll(a, 1, axis=1)
            return jnp.where(first_row, jnp.concatenate([rot[:1], rot[:-1]], axis=0), rot)

        taps = cw_ref[0:1, cols] * xe
        for k in range(1, CONV_K):
            taps = shift_down_one(taps) + cw_ref[k:k + 1, cols] * xe
        conv = taps[1:].reshape(tm, tile) + cb_ref[:, cols]
        xe_ref[:, cols] = xbc[tm - SUBLANES:, :]
        act_ref[:, cols] = _silu(conv).astype(BF16)
        if i < Q_END // tile:
            q_ref[:, i * tile:(i + 1) * tile] = proj(i * tile, (i + 1) * tile).astype(BF16)
        else:
            z0 = (i - Q_END // tile) * tile
            gz_ref[:, z0:z0 + tile] = _silu(proj(V_END + z0, V_END + z0 + tile)).astype(BF16)
    kt_ref[...] = proj(Q_END, K_END).T.astype(BF16)
    v_ref[...] = proj(K_END, V_END).astype(BF16)
    dt_ref[...] = _dot(hb_ref[...], wdt_ref[...])


def _inproj_kernel(x_ref, mod_ref, n1_ref, w_ref, wdt_ref, cw_ref, cb_ref, q_ref, kt_ref, v_ref, gz_ref, act_ref,
                   dt_ref, hb_ref, xe_ref):
    @pl.when(pl.program_id(1) == 0)
    def _():
        xe_ref[...] = jnp.zeros_like(xe_ref)

    _inproj_norm(x_ref, mod_ref, n1_ref, hb_ref)
    _inproj_project(hb_ref, w_ref, wdt_ref, cw_ref, cb_ref, q_ref, kt_ref, v_ref, gz_ref, act_ref, dt_ref, xe_ref)


def _inproj_call(x, mod, norm1, w_main, w_dt, conv_w, conv_b):
    b, s, _ = x.shape
    tm = INPROJ_TM
    row = lambda width: pl.BlockSpec((None, tm, width), lambda i, j: (i, j, 0))
    const = lambda shape: pl.BlockSpec(shape, lambda i, j: (0,) * len(shape))
    return pl.pallas_call(
        _inproj_kernel,
        grid=(b, s // tm),
        in_specs=[
            row(D_MODEL),
            pl.BlockSpec((None, 6, D_MODEL), lambda i, j: (i, 0, 0)),
            const((1, D_MODEL)),
            const((D_MODEL, XBC_END)),
            const((D_MODEL, LANES)),
            const((CONV_K, XBC_WIDTH)), const((1, XBC_WIDTH)),
        ],
        out_specs=[row(ATTN_WIDTH), pl.BlockSpec((None, KV_WIDTH, tm), lambda i, j: (i, 0, j)), row(KV_WIDTH),
                   row(SSM_WIDTH), row(XBC_WIDTH), row(LANES)],
        out_shape=[
            jax.ShapeDtypeStruct((b, s, ATTN_WIDTH), BF16),
            jax.ShapeDtypeStruct((b, KV_WIDTH, s), BF16),
            jax.ShapeDtypeStruct((b, s, KV_WIDTH), BF16),
            jax.ShapeDtypeStruct((b, s, SSM_WIDTH), BF16),
            jax.ShapeDtypeStruct((b, s, XBC_WIDTH), BF16),
            jax.ShapeDtypeStruct((b, s, LANES), F32),
        ],
        scratch_shapes=[
            pltpu.VMEM((tm, D_MODEL), BF16),
            pltpu.VMEM((SUBLANES, XBC_WIDTH), F32),
        ],
        compiler_params=pltpu.CompilerParams(
            dimension_semantics=("parallel", "arbitrary"), vmem_limit_bytes=VMEM_LIMIT_BYTES),
        name="in_proj",
    )(x, mod, norm1, w_main, w_dt, conv_w, conv_b)


def _stage_keys_values(kt_ref, ktp_ref, v_ref, vp_ref, ka_ref, vv_ref):
    zeros = jnp.zeros((HEAD_DIM, BLOCK + MIX_ROWS), BF16)
    for g in range(N_KV_HEADS):
        rows = slice(g * HEAD_DIM, (g + 1) * HEAD_DIM)
        for par in range(2):
            live = slice(par * HEAD_DIM, (par + 1) * HEAD_DIM)
            dead = slice((1 - par) * HEAD_DIM, (2 - par) * HEAD_DIM)
            ka_ref[g, par, live, 0:BLOCK] = ktp_ref[rows, :]
            ka_ref[g, par, live, BLOCK:] = kt_ref[rows, :]
            ka_ref[g, par, dead, :] = zeros

    def duplicate_halves(v):
        v32 = pltpu.bitcast(v, jnp.uint32)
        swapped = pltpu.roll(v32, HEAD_DIM, axis=1)
        lane = lax.broadcasted_iota(jnp.int32, v32.shape, 1)
        first = lane < HEAD_DIM
        return (pltpu.bitcast(jnp.where(first, v32, swapped), BF16),
                pltpu.bitcast(jnp.where(first, swapped, v32), BF16))

    for src, r0, r1 in ((vp_ref, 0, BLOCK), (v_ref, BLOCK, BLOCK + MIX_ROWS)):
        for g, dup in enumerate(duplicate_halves(src[...])):
            vv_ref[g, r0:r1, 0:LANES] = dup


def _scores(q_ref, ka_ref, j):
    rows = pl.ds(j * BLOCK, BLOCK)
    win = pl.ds(j * BLOCK, 2 * BLOCK)
    out = []
    for h in range(N_Q_HEADS):
        t, par = h // 2, h % 2
        out.append(_dot(q_ref[rows, t * LANES:(t + 1) * LANES], ka_ref[t // 2, par, :, win]))
    return out


def _softmax(scores, bias_ref, sinks_ref, own):
    s = [jnp.where(own, sc[:, BLOCK:], sc[:, :BLOCK]) + bias_ref[h] for h, sc in enumerate(scores)]
    sink = [sinks_ref[h] * LOG2E for h in range(N_Q_HEADS)]
    m = [jnp.maximum(jnp.max(s[h], axis=-1, keepdims=True), sink[h]) for h in range(N_Q_HEADS)]
    p = [jnp.exp2(s[h] - m[h]) for h in range(N_Q_HEADS)]
    pcat = [jnp.concatenate([jnp.where(own, 0.0, ph), jnp.where(own, ph, 0.0)], axis=1).astype(BF16) for ph in p]
    sink_term = [jnp.exp2(sink[h] - m[h]) for h in range(N_Q_HEADS)]
    return pcat, sink_term


def _values(pcat, vv_ref, j):
    win = pl.ds(j * BLOCK, 2 * BLOCK)
    return [_dot(pcat[h], vv_ref[h // Q_PER_KV, win, :]) for h in range(N_Q_HEADS)]


def _finish_attention(o2, sink_term, an_ref, o_ref, j):
    rows = pl.ds(j * BLOCK, BLOCK)
    heads = [o2[h][:, :LANES] / (o2[h][:, LANES:] + sink_term[h]) for h in range(N_Q_HEADS)]
    lane = lax.broadcasted_iota(jnp.int32, (BLOCK, LANES), 1)
    tiles = [jnp.where(lane < HEAD_DIM, heads[2 * t], heads[2 * t + 1]) for t in range(N_Q_HEADS // 2)]
    ssq = sum(jnp.sum(tl * tl, axis=-1, keepdims=True) for tl in tiles)
    inv = lax.rsqrt(ssq * (1.0 / ATTN_WIDTH) + EPS)
    for t, tl in enumerate(tiles):
        cols = slice(t * LANES, (t + 1) * LANES)
        o_ref[rows, cols] = (tl * inv * an_ref[:, cols]).astype(BF16)


def _mixer_steps(first_tile, q_ref, kt_ref, ktp_ref, v_ref, vp_ref, gz_ref, act_ref, dt_ref, bias_ref, sinks_ref,
                 dtb_ref, alog_ref, dskip_ref, an_ref, sn_ref, expand_ref, o_ref, state_ref, ka_ref, vv_ref,
                 xbd_ref):
    nch = MIX_CHUNKS
    row_i = lax.broadcasted_iota(jnp.int32, (BLOCK, BLOCK), 0)
    col_i = lax.broadcasted_iota(jnp.int32, (BLOCK, BLOCK), 1)
    own = col_i <= row_i
    bias_of = lambda j: bias_ref.at[0 if (first_tile and j == 0) else 1]
    grp = lambda a, j, g: a[j * CHUNK:(j + 1) * CHUNK, g * GROUP_WIDTH:(g + 1) * GROUP_WIDTH]
    softmax = lambda j, sc: _softmax(sc, bias_of(j), sinks_ref, own)
    finish = lambda j, vals, soft: _finish_attention(vals, soft[1], an_ref, o_ref, j)

    _stage_keys_values(kt_ref, ktp_ref, v_ref, vp_ref, ka_ref, vv_ref)
    dt = jax.nn.softplus(dt_ref[...] + dtb_ref[...])
    dta = dt * (-LOG2E * jnp.exp(alog_ref[...]))
    tri = own.astype(BF16)
    dta_hi, dta_lo = _split2(dta)
    scores = {0: _scores(q_ref, ka_ref, 0), 1: _scores(q_ref, ka_ref, 1)}
    acs = [_dot(tri, dta_hi[j * CHUNK:(j + 1) * CHUNK]) + _dot(tri, dta_lo[j * CHUNK:(j + 1) * CHUNK])
           for j in range(nch)]
    yield

    soft = {0: softmax(0, scores.pop(0))}
    acs_t = [a.T for a in acs]
    dt_t = [dt[j * CHUNK:(j + 1) * CHUNK].T for j in range(nch)]
    dec_in = [jnp.exp2(a) for a in acs]
    dec_out = jnp.concatenate([jnp.exp2(a[CHUNK - 1:CHUNK, :] - a) for a in acs], axis=0)
    state_in = (dt * dec_out).astype(BF16)
    decay_in = jnp.concatenate(dec_in, axis=0).astype(BF16)
    decay_tail = [_split2(d[CHUNK - SUBLANES:, :]) for d in dec_in]
    yield

    vals = {0: _values(soft[0][0], vv_ref, 0)}
    expand = expand_ref[...]
    state_in_x = _dot(state_in, expand)
    decay_in_x = _dot(decay_in, expand)
    chunk_decay = [(_dot(hi, expand) + _dot(lo, expand))[SUBLANES - 1:SUBLANES, :] for hi, lo in decay_tail]
    scores[2] = _scores(q_ref, ka_ref, 2)
    soft[1] = softmax(1, scores.pop(1))
    finish(0, vals.pop(0), soft.pop(0))
    yield

    vals[1] = _values(soft[1][0], vv_ref, 1)
    scores[3] = _scores(q_ref, ka_ref, 3)
    xs = act_ref[:, :SSM_WIDTH].astype(F32)
    xdec_b = (xs * state_in_x).astype(BF16)
    lane_head = lax.broadcasted_iota(jnp.int32, (CHUNK, GROUP_WIDTH), 1) // SSM_HEAD_DIM
    for j in range(nch):
        for g in range(SSM_GROUPS):
            xg = act_ref[j * CHUNK:(j + 1) * CHUNK, g * GROUP_WIDTH:(g + 1) * GROUP_WIDTH]
            for r in range(HEADS_PER_GROUP):
                xbd_ref[j, g, r * CHUNK:(r + 1) * CHUNK, :] = jnp.where(lane_head == r, xg, jnp.zeros_like(xg))
    soft[2] = softmax(2, scores.pop(2))
    finish(1, vals.pop(1), soft.pop(1))
    yield

    bc_b = act_ref[:, SSM_WIDTH:]
    bm = [[bc_b[j * CHUNK:(j + 1) * CHUNK, g * D_STATE:(g + 1) * D_STATE] for g in range(SSM_GROUPS)]
          for j in range(nch)]
    cm = [[bc_b[j * CHUNK:(j + 1) * CHUNK, (SSM_GROUPS + g) * D_STATE:(SSM_GROUPS + g + 1) * D_STATE]
           for g in range(SSM_GROUPS)] for j in range(nch)]
    cbm = [[_dot_nt(cm[j][g], bm[j][g]) for g in range(SSM_GROUPS)] for j in range(nch)]
    d_state = [[_dot_tn(bm[j][g], grp(xdec_b, j, g)) for g in range(SSM_GROUPS)] for j in range(nch)]
    vals[2] = _values(soft[2][0], vv_ref, 2)
    soft[3] = softmax(3, scores.pop(3))

    def intra_weights(j):
        per_group = []
        for g in range(SSM_GROUPS):
            w = []
            for r in range(HEADS_PER_GROUP):
                h = g * HEADS_PER_GROUP + r
                seg = acs[j][:, h:h + 1] - acs_t[j][h:h + 1, :]
                lmat = jnp.exp2(jnp.where(own, seg, -jnp.inf))
                w.append(cbm[j][g] * lmat * dt_t[j][h:h + 1, :])
            per_group.append((jnp.concatenate(w[:2], axis=1).astype(BF16),
                              jnp.concatenate(w[2:], axis=1).astype(BF16)))
        return per_group

    def intra_product(j, w):
        return [_dot(w[g][0], xbd_ref[j, g, 0:2 * CHUNK, :]) + _dot(w[g][1], xbd_ref[j, g, 2 * CHUNK:, :])
                for g in range(SSM_GROUPS)]

    yield
    w = {0: intra_weights(0)}
    finish(2, vals.pop(2), soft.pop(2))
    yield

    vals[3] = _values(soft[3][0], vv_ref, 3)
    y_diag = [intra_product(0, w.pop(0))]
    w[1] = intra_weights(1)
    yield

    y_diag.append(intra_product(1, w.pop(1)))
    w[2] = intra_weights(2)
    finish(3, vals.pop(3), soft.pop(3))
    yield

    y_diag.append(intra_product(2, w.pop(2)))
    w[3] = intra_weights(3)
    yield

    y_diag.append(intra_product(3, w.pop(3)))
    entering = []
    for g in range(SSM_GROUPS):
        state = state_ref[g]
        per_chunk = []
        for j in range(nch):
            per_chunk.append(state.astype(BF16))
            state = state * chunk_decay[j][:, g * GROUP_WIDTH:(g + 1) * GROUP_WIDTH] + d_state[j][g]
        state_ref[g] = state
        entering.append(per_chunk)
    yield

    y_rows = [jnp.concatenate([y_diag[j][g] + _dot(cm[j][g], entering[g][j]) * grp(decay_in_x, j, g)
                               for g in range(SSM_GROUPS)], axis=-1) for j in range(nch)]
    y = jnp.concatenate(y_rows, axis=0) + dskip_ref[...] * xs
    y = y * gz_ref[...].astype(F32)
    y = jnp.concatenate([_rms(y[:, :GROUP_WIDTH]), _rms(y[:, GROUP_WIDTH:])], axis=-1)
    o_ref[:, ATTN_WIDTH:] = (y * sn_ref[...]).astype(BF16)


def _ffn_steps(mix_ref, x_ref, mod_ref, wo_ref, n2_ref, wgu_ref, wd_ref, fn_ref, o_ref, h_ref, acc_ref):
    gain2 = n2_ref[...] * (1.0 + mod_ref[4:5, :])
    x1 = x_ref[...] + mod_ref[2:3, :] * _dot(mix_ref[...], wo_ref[...])
    h_ref[...] = (_rms(x1) * gain2 + mod_ref[3:4, :]).astype(BF16)
    o_ref[...] = x1
    yield
    n_slices = D_FF // FFN_FC
    for j in range(n_slices):
        c0 = j * FFN_FC
        hb = h_ref[...]
        g = _dot(hb, wgu_ref[:, c0:c0 + FFN_FC])
        u = _dot(hb, wgu_ref[:, D_FF + c0:D_FF + c0 + FFN_FC])
        d = _dot((_silu(g) * u).astype(BF16), wd_ref[c0:c0 + FFN_FC, :])
        if j == 0:
            acc_ref[...] = d
        elif j < n_slices - 1:
            acc_ref[...] += d
        else:
            o_ref[...] = _rms(o_ref[...] + mod_ref[5:6, :] * (acc_ref[...] + d)) * fn_ref[...]
        yield


def _mixer_ffn_kernel(q_ref, kt_ref, ktp_ref, v_ref, vp_ref, gz_ref, act_ref, dt_ref, bias_ref, sinks_ref, dtb_ref,
                      alog_ref, dskip_ref, an_ref, sn_ref, expand_ref, x_ref, mod_ref, wo_ref, n2_ref, wgu_ref,
                      wd_ref, fn_ref, o_ref, state_ref, ka_ref, vv_ref, xbd_ref, mix_ref, h_ref, acc_ref):
    s = pl.program_id(1)
    last = pl.num_programs(1) - 1
    mixer = lambda first: _mixer_steps(first, q_ref, kt_ref, ktp_ref, v_ref, vp_ref, gz_ref, act_ref, dt_ref,
                                       bias_ref, sinks_ref, dtb_ref, alog_ref, dskip_ref, an_ref, sn_ref,
                                       expand_ref, mix_ref, state_ref, ka_ref, vv_ref, xbd_ref)
    ffn = lambda: _ffn_steps(mix_ref, x_ref, mod_ref, wo_ref, n2_ref, wgu_ref, wd_ref, fn_ref, o_ref, h_ref, acc_ref)

    @pl.when(s == 0)
    def _():
        state_ref[...] = jnp.zeros_like(state_ref)
        vv_ref[:, :, LANES:] = jnp.ones((N_KV_HEADS, BLOCK + MIX_ROWS, LANES), BF16)
        for _ in mixer(True):
            pass

    @pl.when((s > 0) & (s < last))
    def _():
        ffn_steps, mixer_steps = ffn(), mixer(False)
        next(ffn_steps)
        for _ in mixer_steps:
            next(ffn_steps)
        for _ in ffn_steps:
            pass

    @pl.when(s == last)
    def _():
        for _ in ffn():
            pass


def _mixer_ffn_call(q, kt, v, gz, act, dt, bias, sinks, dt_bias, a_log, d_skip_x, attn_norm, ssm_norm, expand,
                    x, mod, w_o, norm2, w_gu, w_down, final_norm):
    b, s, _ = q.shape
    rows = MIX_ROWS
    n_tiles = s // rows
    mix_tile = lambda i, c: jnp.minimum(c, n_tiles - 1)
    ffn_tile = lambda i, c: jnp.maximum(c - 1, 0)
    prev_blk = lambda i, c: jnp.maximum(mix_tile(i, c) * MIX_CHUNKS - 1, 0)
    mrow = lambda width: pl.BlockSpec((None, rows, width), lambda i, c: (i, mix_tile(i, c), 0))
    frow = lambda: pl.BlockSpec((None, rows, D_MODEL), lambda i, c: (i, ffn_tile(i, c), 0))
    const = lambda shape: pl.BlockSpec(shape, lambda i, c: (0,) * len(shape))
    resident = lambda shape: pl.BlockSpec(shape, lambda i, c: (0,) * len(shape), pipeline_mode=pl.Buffered(1))
    return pl.pallas_call(
        _mixer_ffn_kernel,
        grid=(b, n_tiles + 1),
        in_specs=[
            mrow(ATTN_WIDTH),
            pl.BlockSpec((None, KV_WIDTH, rows), lambda i, c: (i, 0, mix_tile(i, c))),
            pl.BlockSpec((None, KV_WIDTH, BLOCK), lambda i, c: (i, 0, prev_blk(i, c))),
            mrow(KV_WIDTH),
            pl.BlockSpec((None, BLOCK, KV_WIDTH), lambda i, c: (i, prev_blk(i, c), 0)),
            mrow(SSM_WIDTH), mrow(XBC_WIDTH), mrow(LANES),
            const((2, N_Q_HEADS, BLOCK, BLOCK)),
            pl.BlockSpec(memory_space=pltpu.SMEM),
            const((1, LANES)), const((1, LANES)),
            const((1, SSM_WIDTH)), const((1, ATTN_WIDTH)), const((1, SSM_WIDTH)),
            const((LANES, SSM_WIDTH)),
            frow(),
            pl.BlockSpec((None, 6, D_MODEL), lambda i, c: (i, 0, 0)),
            resident((D_MODEL, D_MODEL)), const((1, D_MODEL)),
            resident((D_MODEL, 2 * D_FF)), resident((D_FF, D_MODEL)), const((1, D_MODEL)),
        ],
        out_specs=frow(),
        out_shape=jax.ShapeDtypeStruct((b, s, D_MODEL), F32),
        scratch_shapes=[
            pltpu.VMEM((SSM_GROUPS, D_STATE, GROUP_WIDTH), F32),
            pltpu.VMEM((N_KV_HEADS, 2, KV_WIDTH, BLOCK + rows), BF16),
            pltpu.VMEM((N_KV_HEADS, BLOCK + rows, 2 * LANES), BF16),
            pltpu.VMEM((MIX_CHUNKS, SSM_GROUPS, HEADS_PER_GROUP * CHUNK, GROUP_WIDTH), BF16),
            pltpu.VMEM((rows, ATTN_WIDTH + SSM_WIDTH), BF16),
            pltpu.VMEM((rows, D_MODEL), BF16),
            pltpu.VMEM((rows, D_MODEL), F32),
        ],
        compiler_params=pltpu.CompilerParams(
            dimension_semantics=("parallel", "arbitrary"), vmem_limit_bytes=VMEM_LIMIT_BYTES),
        name="mixer_ffn",
    )(q, kt, kt, v, v, gz, act, dt, bias, sinks, dt_bias, a_log, d_skip_x, attn_norm, ssm_norm, expand,
      x, mod, w_o, norm2, w_gu, w_down, final_norm)


def _head_expand_matrix():
    e = np.zeros((LANES, SSM_WIDTH), np.float32)
    for h in range(SSM_HEADS):
        e[h, h * SSM_HEAD_DIM:(h + 1) * SSM_HEAD_DIM] = 1.0
    return e


def kernel(x, c, ada_w, ada_b, norm1, w_in, conv_w, conv_b, dt_bias, A_log, D_skip, sinks, attn_out_norm,
           ssm_out_norm, w_o, norm2, w_gate_up, w_down, rel_bias, final_norm):
    assert ada_w.shape[0] == 1, "the final RMSNorm is fused into the (single) layer's FFN call"
    b = x.shape[0]
    bias = _bias_call(rel_bias)
    expand = jnp.asarray(_head_expand_matrix(), BF16)
    pad_heads = lambda v: jnp.pad(v, (0, LANES - SSM_HEADS))[None, :]
    c_pad = jnp.pad(c, ((0, SUBLANES - b), (0, 0)))
    mod = _ada_call(c_pad, ada_w[0], ada_b[0][None, :])[:b].reshape(b, 6, D_MODEL)
    w_q = w_in[0][:, :Q_END] * (HEAD_DIM ** -0.5 * LOG2E)
    w_main = jnp.concatenate([w_q, w_in[0][:, Q_END:XBC_END]], axis=1).astype(BF16)
    w_dt = jnp.pad(w_in[0][:, XBC_END:], ((0, 0), (0, LANES - SSM_HEADS))).astype(BF16)
    q, kt, v, gz, act, dt = _inproj_call(x, mod, norm1[0][None, :], w_main, w_dt, conv_w[0], conv_b[0][None, :])
    return _mixer_ffn_call(
        q, kt, v, gz, act, dt, bias, sinks[0], pad_heads(dt_bias[0]), pad_heads(A_log[0]),
        jnp.repeat(D_skip[0], SSM_HEAD_DIM)[None, :], attn_out_norm[0][None, :], ssm_out_norm[0][None, :], expand,
        x, mod, w_o[0].astype(BF16), norm2[0][None, :], w_gate_up[0].astype(BF16), w_down[0].astype(BF16),
        final_norm[None, :])
```

```python
import numpy as np
import jax
import jax.numpy as jnp
from jax import lax
from jax.experimental import pallas as pl
from jax.experimental.pallas import tpu as pltpu

F32 = jnp.float32
BF16 = jnp.bfloat16

D_MODEL = 1024
HEAD_DIM = 64
N_Q_HEADS = 8
N_KV_HEADS = 2
Q_PER_KV = N_Q_HEADS // N_KV_HEADS
ATTN_WIDTH = N_Q_HEADS * HEAD_DIM
KV_WIDTH = N_KV_HEADS * HEAD_DIM
WINDOW = 128
BLOCK = 128
N_BUCKETS = 32
MAX_DISTANCE = 128

SSM_HEAD_DIM = 64
SSM_HEADS = 8
SSM_GROUPS = 2
HEADS_PER_GROUP = SSM_HEADS // SSM_GROUPS
SSM_WIDTH = SSM_HEADS * SSM_HEAD_DIM
GROUP_WIDTH = SSM_WIDTH // SSM_GROUPS
D_STATE = 128
CONV_K = 4
CHUNK = 128
XBC_WIDTH = SSM_WIDTH + 2 * SSM_GROUPS * D_STATE
D_FF = 2816
EPS = 1e-6
LOG2E = float(np.log2(np.e))

LANES = 128
SUBLANES = 8
VMEM_LIMIT_BYTES = 56 * 1024 * 1024

ADA_TN = 1536
INPROJ_TM = 1024
FFN_FC = 256
MIX_CHUNKS = 4
MIXER_YIELDS = 10
MIX_ROWS = MIX_CHUNKS * CHUNK

Q_END = ATTN_WIDTH
K_END = Q_END + KV_WIDTH
V_END = K_END + KV_WIDTH
Z_END = V_END + SSM_WIDTH
XBC_END = Z_END + XBC_WIDTH

assert BLOCK == CHUNK == LANES and KV_WIDTH == LANES and 2 * HEAD_DIM == LANES


def _silu(x):
    return x * jax.nn.sigmoid(x)


def _rms(x):
    return x * lax.rsqrt(jnp.mean(x * x, axis=-1, keepdims=True) + EPS)


def _split2(x):
    hi = x.astype(BF16)
    lo = (x - hi.astype(F32)).astype(BF16)
    return hi, lo


def _dot(a, b):
    return jnp.dot(a, b, preferred_element_type=F32)


def _dot_nt(a, b):
    return lax.dot_general(a, b, (((1,), (1,)), ((), ())), preferred_element_type=F32)


def _dot_tn(a, b):
    return lax.dot_general(a, b, (((0,), (0,)), ((), ())), preferred_element_type=F32)


def _ada_kernel(c_ref, w_ref, b_ref, o_ref):
    cond = _silu(c_ref[...])
    o_ref[...] = _dot(cond.astype(BF16), w_ref[...].astype(BF16)) + b_ref[...]


def _ada_call(c_pad, ada_w, ada_b):
    n = ada_w.shape[1]
    return pl.pallas_call(
        _ada_kernel,
        grid=(n // ADA_TN,),
        in_specs=[
            pl.BlockSpec((SUBLANES, D_MODEL), lambda j: (0, 0)),
            pl.BlockSpec((D_MODEL, ADA_TN), lambda j: (0, j)),
            pl.BlockSpec((1, ADA_TN), lambda j: (0, j)),
        ],
        out_specs=pl.BlockSpec((SUBLANES, ADA_TN), lambda j: (0, j)),
        out_shape=jax.ShapeDtypeStruct((SUBLANES, n), F32),
        compiler_params=pltpu.CompilerParams(dimension_semantics=("parallel",)),
        name="ada_mod",
    )(c_pad, ada_w, ada_b)


def _t5_bucket_table():
    i = np.arange(BLOCK)[:, None]
    j = np.arange(BLOCK)[None, :]
    n = np.where(j <= i, i - j, i + BLOCK - j)
    assert WINDOW == BLOCK and n.min() >= 0 and n.max() < WINDOW
    max_exact = N_BUCKETS // 2
    large = max_exact + (np.log(np.maximum(n, 1) / max_exact) / np.log(MAX_DISTANCE / max_exact)
                         * (N_BUCKETS - max_exact)).astype(np.int32)
    large = np.minimum(large, N_BUCKETS - 1)
    bucket = np.where(n < max_exact, n, large).astype(np.int32)
    first = np.where(j <= i, bucket, -1)
    return np.stack([first, bucket]).astype(np.int32)


def _bias_kernel(idx_ref, rb_ref, o_ref):
    for v in range(2):
        idx = idx_ref[v]
        hits = [idx == b for b in range(N_BUCKETS)]
        for h in range(N_Q_HEADS):
            acc = jnp.full((BLOCK, BLOCK), -jnp.inf, F32)
            for b in range(N_BUCKETS):
                acc = jnp.where(hits[b], rb_ref[b, h] * LOG2E, acc)
            o_ref[v, h] = acc


def _bias_call(rel_bias):
    idx = jnp.asarray(_t5_bucket_table())
    return pl.pallas_call(
        _bias_kernel,
        in_specs=[
            pl.BlockSpec(memory_space=pltpu.VMEM),
            pl.BlockSpec(memory_space=pltpu.SMEM),
        ],
        out_specs=pl.BlockSpec(memory_space=pltpu.VMEM),
        out_shape=jax.ShapeDtypeStruct((2, N_Q_HEADS, BLOCK, BLOCK), F32),
        name="t5_bias",
    )(idx, rel_bias)


def _inproj_norm(x_ref, mod_ref, n1_ref, hb_ref):
    gain = n1_ref[...] * (1.0 + mod_ref[1:2, :])
    hb_ref[...] = (_rms(x_ref[...]) * gain + mod_ref[0:1, :]).astype(BF16)


def _inproj_project(hb_ref, w_ref, wdt_ref, cw_ref, cb_ref, q_ref, kt_ref, v_ref, gz_ref, act_ref, dt_ref, xe_ref):
    tm = hb_ref.shape[0]
    proj = lambda lo, hi: _dot(hb_ref[...], w_ref[:, lo:hi])
    tile = 2 * LANES
    for i, c0 in enumerate(range(0, XBC_WIDTH, tile)):
        cols = slice(c0, c0 + tile)
        xbc = proj(Z_END + c0, Z_END + c0 + tile)
        xe = jnp.concatenate([xe_ref[:, cols], xbc], axis=0)
        xe = xe.reshape(tm // SUBLANES + 1, SUBLANES, tile)
        first_row = lax.broadcasted_iota(jnp.int32, xe.shape, 1) == 0

        def shift_down_one(a):
            rot = pltpu.roll(a, 1, axis=1)
            return jnp.where(first_row, jnp.concatenate([rot[:1], rot[:-1]], axis=0), rot)

        taps = cw_ref[0:1, cols] * xe
        for k in range(1, CONV_K):
            taps = shift_down_one(taps) + cw_ref[k:k + 1, cols] * xe
        conv = taps[1:].reshape(tm, tile) + cb_ref[:, cols]
        xe_ref[:, cols] = xbc[tm - SUBLANES:, :]
        act_ref[:, cols] = _silu(conv).astype(BF16)
        if i < Q_END // tile:
            q_ref[:, i * tile:(i + 1) * tile] = proj(i * tile, (i + 1) * tile).astype(BF16)
        else:
            z0 = (i - Q_END // tile) * tile
            gz_ref[:, z0:z0 + tile] = _silu(proj(V_END + z0, V_END + z0 + tile)).astype(BF16)
    kt_ref[...] = proj(Q_END, K_END).T.astype(BF16)
    v_ref[...] = proj(K_END, V_END).astype(BF16)
    dt_ref[...] = _dot(hb_ref[...], wdt_ref[...])


def _inproj_kernel(x_ref, mod_ref, n1_ref, w_ref, wdt_ref, cw_ref, cb_ref, q_ref, kt_ref, v_ref, gz_ref, act_ref,
                   dt_ref, hb_ref, xe_ref):
    @pl.when(pl.program_id(1) == 0)
    def _():
        xe_ref[...] = jnp.zeros_like(xe_ref)

    _inproj_norm(x_ref, mod_ref, n1_ref, hb_ref)
    _inproj_project(hb_ref, w_ref, wdt_ref, cw_ref, cb_ref, q_ref, kt_ref, v_ref, gz_ref, act_ref, dt_ref, xe_ref)


def _inproj_call(x, mod, norm1, w_main, w_dt, conv_w, conv_b):
    b, s, _ = x.shape
    tm = INPROJ_TM
    row = lambda width: pl.BlockSpec((None, tm, width), lambda i, j: (i, j, 0))
    const = lambda shape: pl.BlockSpec(shape, lambda i, j: (0,) * len(shape))
    return pl.pallas_call(
        _inproj_kernel,
        grid=(b, s // tm),
        in_specs=[
            row(D_MODEL),
            pl.BlockSpec((None, 6, D_MODEL), lambda i, j: (i, 0, 0)),
            const((1, D_MODEL)),
            const((D_MODEL, XBC_END)),
            const((D_MODEL, LANES)),
            const((CONV_K, XBC_WIDTH)), const((1, XBC_WIDTH)),
        ],
        out_specs=[row(ATTN_WIDTH), pl.BlockSpec((None, KV_WIDTH, tm), lambda i, j: (i, 0, j)), row(KV_WIDTH),
                   row(SSM_WIDTH), row(XBC_WIDTH), row(LANES)],
        out_shape=[
            jax.ShapeDtypeStruct((b, s, ATTN_WIDTH), BF16),
            jax.ShapeDtypeStruct((b, KV_WIDTH, s), BF16),
            jax.ShapeDtypeStruct((b, s, KV_WIDTH), BF16),
            jax.ShapeDtypeStruct((b, s, SSM_WIDTH), BF16),
            jax.ShapeDtypeStruct((b, s, XBC_WIDTH), BF16),
            jax.ShapeDtypeStruct((b, s, LANES), F32),
        ],
        scratch_shapes=[
            pltpu.VMEM((tm, D_MODEL), BF16),
            pltpu.VMEM((SUBLANES, XBC_WIDTH), F32),
        ],
        compiler_params=pltpu.CompilerParams(
            dimension_semantics=("parallel", "arbitrary"), vmem_limit_bytes=VMEM_LIMIT_BYTES),
        name="in_proj",
    )(x, mod, norm1, w_main, w_dt, conv_w, conv_b)


def _stage_keys_values(kt_ref, ktp_ref, v_ref, vp_ref, ka_ref, vv_ref):
    zeros = jnp.zeros((HEAD_DIM, BLOCK + MIX_ROWS), BF16)
    for g in range(N_KV_HEADS):
        rows = slice(g * HEAD_DIM, (g + 1) * HEAD_DIM)
        for par in range(2):
            live = slice(par * HEAD_DIM, (par + 1) * HEAD_DIM)
            dead = slice((1 - par) * HEAD_DIM, (2 - par) * HEAD_DIM)
            ka_ref[g, par, live, 0:BLOCK] = ktp_ref[rows, :]
            ka_ref[g, par, live, BLOCK:] = kt_ref[rows, :]
            ka_ref[g, par, dead, :] = zeros

    def duplicate_halves(v):
        v32 = pltpu.bitcast(v, jnp.uint32)
        swapped = pltpu.roll(v32, HEAD_DIM, axis=1)
        lane = lax.broadcasted_iota(jnp.int32, v32.shape, 1)
        first = lane < HEAD_DIM
        return (pltpu.bitcast(jnp.where(first, v32, swapped), BF16),
                pltpu.bitcast(jnp.where(first, swapped, v32), BF16))

    for src, r0, r1 in ((vp_ref, 0, BLOCK), (v_ref, BLOCK, BLOCK + MIX_ROWS)):
        for g, dup in enumerate(duplicate_halves(src[...])):
            vv_ref[g, r0:r1, 0:LANES] = dup


def _scores(q_ref, ka_ref, j):
    rows = pl.ds(j * BLOCK, BLOCK)
    win = pl.ds(j * BLOCK, 2 * BLOCK)
    out = []
    for h in range(N_Q_HEADS):
        t, par = h // 2, h % 2
        out.append(_dot(q_ref[rows, t * LANES:(t + 1) * LANES], ka_ref[t // 2, par, :, win]))
    return out


def _softmax(scores, bias_ref, sinks_ref, own):
    s = [jnp.where(own, sc[:, BLOCK:], sc[:, :BLOCK]) + bias_ref[h] for h, sc in enumerate(scores)]
    sink = [sinks_ref[h] * LOG2E for h in range(N_Q_HEADS)]
    m = [jnp.maximum(jnp.max(s[h], axis=-1, keepdims=True), sink[h]) for h in range(N_Q_HEADS)]
    p = [jnp.exp2(s[h] - m[h]) for h in range(N_Q_HEADS)]
    pcat = [jnp.concatenate([jnp.where(own, 0.0, ph), jnp.where(own, ph, 0.0)], axis=1).astype(BF16) for ph in p]
    sink_term = [jnp.exp2(sink[h] - m[h]) for h in range(N_Q_HEADS)]
    return pcat, sink_term


def _values(pcat, vv_ref, j):
    win = pl.ds(j * BLOCK, 2 * BLOCK)
    return [_dot(pcat[h], vv_ref[h // Q_PER_KV, win, :]) for h in range(N_Q_HEADS)]


def _finish_attention(o2, sink_term, an_ref, o_ref, j):
    rows = pl.ds(j * BLOCK, BLOCK)
    heads = [o2[h][:, :LANES] / (o2[h][:, LANES:] + sink_term[h]) for h in range(N_Q_HEADS)]
    lane = lax.broadcasted_iota(jnp.int32, (BLOCK, LANES), 1)
    tiles = [jnp.where(lane < HEAD_DIM, heads[2 * t], heads[2 * t + 1]) for t in range(N_Q_HEADS // 2)]
    ssq = sum(jnp.sum(tl * tl, axis=-1, keepdims=True) for tl in tiles)
    inv = lax.rsqrt(ssq * (1.0 / ATTN_WIDTH) + EPS)
    for t, tl in enumerate(tiles):
        cols = slice(t * LANES, (t + 1) * LANES)
        o_ref[rows, cols] = (tl * inv * an_ref[:, cols]).astype(BF16)


def _mixer_steps(first_tile, q_ref, kt_ref, ktp_ref, v_ref, vp_ref, gz_ref, act_ref, dt_ref, bias_ref, sinks_ref,
                 dtb_ref, alog_ref, dskip_ref, an_ref, sn_ref, expand_ref, o_ref, state_ref, ka_ref, vv_ref,
                 xbd_ref):
    nch = MIX_CHUNKS
    row_i = lax.broadcasted_iota(jnp.int32, (BLOCK, BLOCK), 0)
    col_i = lax.broadcasted_iota(jnp.int32, (BLOCK, BLOCK), 1)
    own = col_i <= row_i
    bias_of = lambda j: bias_ref.at[0 if (first_tile and j == 0) else 1]
    grp = lambda a, j, g: a[j * CHUNK:(j + 1) * CHUNK, g * GROUP_WIDTH:(g + 1) * GROUP_WIDTH]
    softmax = lambda j, sc: _softmax(sc, bias_of(j), sinks_ref, own)
    finish = lambda j, vals, soft: _finish_attention(vals, soft[1], an_ref, o_ref, j)

    _stage_keys_values(kt_ref, ktp_ref, v_ref, vp_ref, ka_ref, vv_ref)
    dt = jax.nn.softplus(dt_ref[...] + dtb_ref[...])
    dta = dt * (-LOG2E * jnp.exp(alog_ref[...]))
    tri = own.astype(BF16)
    dta_hi, dta_lo = _split2(dta)
    scores = {0: _scores(q_ref, ka_ref, 0), 1: _scores(q_ref, ka_ref, 1)}
    acs = [_dot(tri, dta_hi[j * CHUNK:(j + 1) * CHUNK]) + _dot(tri, dta_lo[j * CHUNK:(j + 1) * CHUNK])
           for j in range(nch)]
    yield

    soft = {0: softmax(0, scores.pop(0))}
    acs_t = [a.T for a in acs]
    dt_t = [dt[j * CHUNK:(j + 1) * CHUNK].T for j in range(nch)]
    dec_in = [jnp.exp2(a) for a in acs]
    dec_out = jnp.concatenate([jnp.exp2(a[CHUNK - 1:CHUNK, :] - a) for a in acs], axis=0)
    state_in = (dt * dec_out).astype(BF16)
    decay_in = jnp.concatenate(dec_in, axis=0).astype(BF16)
    decay_tail = [_split2(d[CHUNK - SUBLANES:, :]) for d in dec_in]
    yield

    vals = {0: _values(soft[0][0], vv_ref, 0)}
    expand = expand_ref[...]
    state_in_x = _dot(state_in, expand)
    decay_in_x = _dot(decay_in, expand)
    chunk_decay = [(_dot(hi, expand) + _dot(lo, expand))[SUBLANES - 1:SUBLANES, :] for hi, lo in decay_tail]
    scores[2] = _scores(q_ref, ka_ref, 2)
    soft[1] = softmax(1, scores.pop(1))
    finish(0, vals.pop(0), soft.pop(0))
    yield

    vals[1] = _values(soft[1][0], vv_ref, 1)
    scores[3] = _scores(q_ref, ka_ref, 3)
    xs = act_ref[:, :SSM_WIDTH].astype(F32)
    xdec_b = (xs * state_in_x).astype(BF16)
    lane_head = lax.broadcasted_iota(jnp.int32, (CHUNK, GROUP_WIDTH), 1) // SSM_HEAD_DIM
    for j in range(nch):
        for g in range(SSM_GROUPS):
            xg = act_ref[j * CHUNK:(j + 1) * CHUNK, g * GROUP_WIDTH:(g + 1) * GROUP_WIDTH]
            for r in range(HEADS_PER_GROUP):
                xbd_ref[j, g, r * CHUNK:(r + 1) * CHUNK, :] = jnp.where(lane_head == r, xg, jnp.zeros_like(xg))
    soft[2] = softmax(2, scores.pop(2))
    finish(1, vals.pop(1), soft.pop(1))
    yield

    bc_b = act_ref[:, SSM_WIDTH:]
    bm = [[bc_b[j * CHUNK:(j + 1) * CHUNK, g * D_STATE:(g + 1) * D_STATE] for g in range(SSM_GROUPS)]
          for j in range(nch)]
    cm = [[bc_b[j * CHUNK:(j + 1) * CHUNK, (SSM_GROUPS + g) * D_STATE:(SSM_GROUPS + g + 1) * D_STATE]
           for g in range(SSM_GROUPS)] for j in range(nch)]
    cbm = [[_dot_nt(cm[j][g], bm[j][g]) for g in range(SSM_GROUPS)] for j in range(nch)]
    d_state = [[_dot_tn(bm[j][g], grp(xdec_b, j, g)) for g in range(SSM_GROUPS)] for j in range(nch)]
    vals[2] = _values(soft[2][0], vv_ref, 2)
    soft[3] = softmax(3, scores.pop(3))

    def intra_weights(j):
        per_group = []
        for g in range(SSM_GROUPS):
            w = []
            for r in range(HEADS_PER_GROUP):
                h = g * HEADS_PER_GROUP + r
                seg = acs[j][:, h:h + 1] - acs_t[j][h:h + 1, :]
                lmat = jnp.exp2(jnp.where(own, seg, -jnp.inf))
                w.append(cbm[j][g] * lmat * dt_t[j][h:h + 1, :])
            per_group.append((jnp.concatenate(w[:2], axis=1).astype(BF16),
                              jnp.concatenate(w[2:], axis=1).astype(BF16)))
        return per_group

    def intra_product(j, w):
        return [_dot(w[g][0], xbd_ref[j, g, 0:2 * CHUNK, :]) + _dot(w[g][1], xbd_ref[j, g, 2 * CHUNK:, :])
                for g in range(SSM_GROUPS)]

    yield
    w = {0: intra_weights(0)}
    finish(2, vals.pop(2), soft.pop(2))
    yield

    vals[3] = _values(soft[3][0], vv_ref, 3)
    y_diag = [intra_product(0, w.pop(0))]
    w[1] = intra_weights(1)
    yield

    y_diag.append(intra_product(1, w.pop(1)))
    w[2] = intra_weights(2)
    finish(3, vals.pop(3), soft.pop(3))
    yield

    y_diag.append(intra_product(2, w.pop(2)))
    w[3] = intra_weights(3)
    yield

    y_diag.append(intra_product(3, w.pop(3)))
    entering = []
    for g in range(SSM_GROUPS):
        state = state_ref[g]
        per_chunk = []
        for j in range(nch):
            per_chunk.append(state.astype(BF16))
            state = state * chunk_decay[j][:, g * GROUP_WIDTH:(g + 1) * GROUP_WIDTH] + d_state[j][g]
        state_ref[g] = state
        entering.append(per_chunk)
    yield

    y_rows = [jnp.concatenate([y_diag[j][g] + _dot(cm[j][g], entering[g][j]) * grp(decay_in_x, j, g)
                               for g in range(SSM_GROUPS)], axis=-1) for j in range(nch)]
    y = jnp.concatenate(y_rows, axis=0) + dskip_ref[...] * xs
    y = y * gz_ref[...].astype(F32)
    y = jnp.concatenate([_rms(y[:, :GROUP_WIDTH]), _rms(y[:, GROUP_WIDTH:])], axis=-1)
    o_ref[:, ATTN_WIDTH:] = (y * sn_ref[...]).astype(BF16)


FFN_PHASES = D_FF // FFN_FC + D_MODEL // FFN_FC


def _ffn_steps(mix_ref, x_ref, mod_ref, wo_ref, n2_ref, wgu_ref, wd_ref, fn_ref, o_ref, h_ref, a_ref):
    gain2 = n2_ref[...] * (1.0 + mod_ref[4:5, :])
    x1 = x_ref[...] + mod_ref[2:3, :] * _dot(mix_ref[...], wo_ref[...])
    h_ref[...] = (_rms(x1) * gain2 + mod_ref[3:4, :]).astype(BF16)
    o_ref[...] = x1
    yield
    for c0 in range(0, D_FF, FFN_FC):
        hb = h_ref[...]
        g = _dot(hb, wgu_ref[:, c0:c0 + FFN_FC])
        u = _dot(hb, wgu_ref[:, D_FF + c0:D_FF + c0 + FFN_FC])
        a_ref[:, c0:c0 + FFN_FC] = (_silu(g) * u).astype(BF16)
        yield
    for c0 in range(0, D_MODEL, FFN_FC):
        cols = slice(c0, c0 + FFN_FC)
        o_ref[:, cols] = o_ref[:, cols] + mod_ref[5:6, cols] * _dot(a_ref[...], wd_ref[:, cols])
        if c0 + FFN_FC == D_MODEL:
            o_ref[...] = _rms(o_ref[...]) * fn_ref[...]
        yield


def _mixer_ffn_kernel(q_ref, kt_ref, ktp_ref, v_ref, vp_ref, gz_ref, act_ref, dt_ref, bias_ref, sinks_ref, dtb_ref,
                      alog_ref, dskip_ref, an_ref, sn_ref, expand_ref, x_ref, mod_ref, wo_ref, n2_ref, wgu_ref,
                      wd_ref, fn_ref, o_ref, state_ref, ka_ref, vv_ref, xbd_ref, mix_ref, h_ref, a_ref):
    s = pl.program_id(1)
    last = pl.num_programs(1) - 1
    mixer = lambda first: _mixer_steps(first, q_ref, kt_ref, ktp_ref, v_ref, vp_ref, gz_ref, act_ref, dt_ref,
                                       bias_ref, sinks_ref, dtb_ref, alog_ref, dskip_ref, an_ref, sn_ref,
                                       expand_ref, mix_ref, state_ref, ka_ref, vv_ref, xbd_ref)
    ffn = lambda: _ffn_steps(mix_ref, x_ref, mod_ref, wo_ref, n2_ref, wgu_ref, wd_ref, fn_ref, o_ref, h_ref, a_ref)

    @pl.when(s == 0)
    def _():
        state_ref[...] = jnp.zeros_like(state_ref)
        vv_ref[:, :, LANES:] = jnp.ones((N_KV_HEADS, BLOCK + MIX_ROWS, LANES), BF16)
        for _ in mixer(True):
            pass

    @pl.when((s > 0) & (s < last))
    def _():
        ffn_steps, issued = ffn(), 0
        next(ffn_steps)
        for k, _ in enumerate(mixer(False)):
            while issued < ((k + 1) * (FFN_PHASES - 1)) // MIXER_YIELDS:
                next(ffn_steps)
                issued += 1
        assert k == MIXER_YIELDS - 1
        for _ in ffn_steps:
            pass

    @pl.when(s == last)
    def _():
        for _ in ffn():
            pass


def _mixer_ffn_call(q, kt, v, gz, act, dt, bias, sinks, dt_bias, a_log, d_skip_x, attn_norm, ssm_norm, expand,
                    x, mod, w_o, norm2, w_gu, w_down, final_norm):
    b, s, _ = q.shape
    rows = MIX_ROWS
    n_tiles = s // rows
    mix_tile = lambda i, c: jnp.minimum(c, n_tiles - 1)
    ffn_tile = lambda i, c: jnp.maximum(c - 1, 0)
    prev_blk = lambda i, c: jnp.maximum(mix_tile(i, c) * MIX_CHUNKS - 1, 0)
    mrow = lambda width: pl.BlockSpec((None, rows, width), lambda i, c: (i, mix_tile(i, c), 0))
    frow = lambda: pl.BlockSpec((None, rows, D_MODEL), lambda i, c: (i, ffn_tile(i, c), 0))
    const = lambda shape: pl.BlockSpec(shape, lambda i, c: (0,) * len(shape))
    resident = lambda shape: pl.BlockSpec(shape, lambda i, c: (0,) * len(shape), pipeline_mode=pl.Buffered(1))
    return pl.pallas_call(
        _mixer_ffn_kernel,
        grid=(b, n_tiles + 1),
        in_specs=[
            mrow(ATTN_WIDTH),
            pl.BlockSpec((None, KV_WIDTH, rows), lambda i, c: (i, 0, mix_tile(i, c))),
            pl.BlockSpec((None, KV_WIDTH, BLOCK), lambda i, c: (i, 0, prev_blk(i, c))),
            mrow(KV_WIDTH),
            pl.BlockSpec((None, BLOCK, KV_WIDTH), lambda i, c: (i, prev_blk(i, c), 0)),
            mrow(SSM_WIDTH), mrow(XBC_WIDTH), mrow(LANES),
            const((2, N_Q_HEADS, BLOCK, BLOCK)),
            pl.BlockSpec(memory_space=pltpu.SMEM),
            const((1, LANES)), const((1, LANES)),
            const((1, SSM_WIDTH)), const((1, ATTN_WIDTH)), const((1, SSM_WIDTH)),
            const((LANES, SSM_WIDTH)),
            frow(),
            pl.BlockSpec((None, 6, D_MODEL), lambda i, c: (i, 0, 0)),
            resident((D_MODEL, D_MODEL)), const((1, D_MODEL)),
            resident((D_MODEL, 2 * D_FF)), resident((D_FF, D_MODEL)), const((1, D_MODEL)),
        ],
        out_specs=frow(),
        out_shape=jax.ShapeDtypeStruct((b, s, D_MODEL), F32),
        scratch_shapes=[
            pltpu.VMEM((SSM_GROUPS, D_STATE, GROUP_WIDTH), F32),
            pltpu.VMEM((N_KV_HEADS, 2, KV_WIDTH, BLOCK + rows), BF16),
            pltpu.VMEM((N_KV_HEADS, BLOCK + rows, 2 * LANES), BF16),
            pltpu.VMEM((MIX_CHUNKS, SSM_GROUPS, HEADS_PER_GROUP * CHUNK, GROUP_WIDTH), BF16),
            pltpu.VMEM((rows, ATTN_WIDTH + SSM_WIDTH), BF16),
            pltpu.VMEM((rows, D_MODEL), BF16),
            pltpu.VMEM((rows, D_FF), BF16),
        ],
        compiler_params=pltpu.CompilerParams(
            dimension_semantics=("parallel", "arbitrary"), vmem_limit_bytes=VMEM_LIMIT_BYTES),
        name="mixer_ffn",
    )(q, kt, kt, v, v, gz, act, dt, bias, sinks, dt_bias, a_log, d_skip_x, attn_norm, ssm_norm, expand,
      x, mod, w_o, norm2, w_gu, w_down, final_norm)


def _head_expand_matrix():
    e = np.zeros((LANES, SSM_WIDTH), np.float32)
    for h in range(SSM_HEADS):
        e[h, h * SSM_HEAD_DIM:(h + 1) * SSM_HEAD_DIM] = 1.0
    return e


def kernel(x, c, ada_w, ada_b, norm1, w_in, conv_w, conv_b, dt_bias, A_log, D_skip, sinks, attn_out_norm,
           ssm_out_norm, w_o, norm2, w_gate_up, w_down, rel_bias, final_norm):
    assert ada_w.shape[0] == 1, "the final RMSNorm is fused into the (single) layer's FFN call"
    b = x.shape[0]
    bias = _bias_call(rel_bias)
    expand = jnp.asarray(_head_expand_matrix(), BF16)
    pad_heads = lambda v: jnp.pad(v, (0, LANES - SSM_HEADS))[None, :]
    c_pad = jnp.pad(c, ((0, SUBLANES - b), (0, 0)))
    mod = _ada_call(c_pad, ada_w[0], ada_b[0][None, :])[:b].reshape(b, 6, D_MODEL)
    w_q = w_in[0][:, :Q_END] * (HEAD_DIM ** -0.5 * LOG2E)
    w_main = jnp.concatenate([w_q, w_in[0][:, Q_END:XBC_END]], axis=1).astype(BF16)
    w_dt = jnp.pad(w_in[0][:, XBC_END:], ((0, 0), (0, LANES - SSM_HEADS))).astype(BF16)
    q, kt, v, gz, act, dt = _inproj_call(x, mod, norm1[0][None, :], w_main, w_dt, conv_w[0], conv_b[0][None, :])
    return _mixer_ffn_call(
        q, kt, v, gz, act, dt, bias, sinks[0], pad_heads(dt_bias[0]), pad_heads(A_log[0]),
        jnp.repeat(D_skip[0], SSM_HEAD_DIM)[None, :], attn_out_norm[0][None, :], ssm_out_norm[0][None, :], expand,
        x, mod, w_o[0].astype(BF16), norm2[0][None, :], w_gate_up[0].astype(BF16), w_down[0].astype(BF16),
        final_norm[None, :])
```

```python
import numpy as np
import jax
import jax.numpy as jnp
from jax import lax
from jax.experimental import pallas as pl
from jax.experimental.pallas import tpu as pltpu

F32 = jnp.float32
BF16 = jnp.bfloat16

D_MODEL = 1024
HEAD_DIM = 64
N_Q_HEADS = 8
N_KV_HEADS = 2
Q_PER_KV = N_Q_HEADS // N_KV_HEADS
ATTN_WIDTH = N_Q_HEADS * HEAD_DIM
KV_WIDTH = N_KV_HEADS * HEAD_DIM
WINDOW = 128
BLOCK = 128
N_BUCKETS = 32
MAX_DISTANCE = 128

SSM_HEAD_DIM = 64
SSM_HEADS = 8
SSM_GROUPS = 2
HEADS_PER_GROUP = SSM_HEADS // SSM_GROUPS
SSM_WIDTH = SSM_HEADS * SSM_HEAD_DIM
GROUP_WIDTH = SSM_WIDTH // SSM_GROUPS
D_STATE = 128
CONV_K = 4
CHUNK = 128
XBC_WIDTH = SSM_WIDTH + 2 * SSM_GROUPS * D_STATE
D_FF = 2816
EPS = 1e-6
LOG2E = float(np.log2(np.e))

LANES = 128
SUBLANES = 8
VMEM_LIMIT_BYTES = 56 * 1024 * 1024

ADA_TN = 1536
INPROJ_TM = 1024
FFN_FC = 256
MIX_CHUNKS = 4
MIXER_YIELDS = 10
MIX_ROWS = MIX_CHUNKS * CHUNK

Q_END = ATTN_WIDTH
K_END = Q_END + KV_WIDTH
V_END = K_END + KV_WIDTH
Z_END = V_END + SSM_WIDTH
XBC_END = Z_END + XBC_WIDTH

assert BLOCK == CHUNK == LANES and KV_WIDTH == LANES and 2 * HEAD_DIM == LANES


def _silu(x):
    return x * jax.nn.sigmoid(x)


def _rms(x):
    return x * lax.rsqrt(jnp.mean(x * x, axis=-1, keepdims=True) + EPS)


def _split2(x):
    hi = x.astype(BF16)
    lo = (x - hi.astype(F32)).astype(BF16)
    return hi, lo


def _dot(a, b):
    return jnp.dot(a, b, preferred_element_type=F32)


def _dot_nt(a, b):
    return lax.dot_general(a, b, (((1,), (1,)), ((), ())), preferred_element_type=F32)


def _dot_tn(a, b):
    return lax.dot_general(a, b, (((0,), (0,)), ((), ())), preferred_element_type=F32)


def _ada_kernel(c_ref, w_ref, b_ref, o_ref):
    cond = _silu(c_ref[...])
    o_ref[...] = _dot(cond.astype(BF16), w_ref[...].astype(BF16)) + b_ref[...]


def _ada_call(c_pad, ada_w, ada_b):
    n = ada_w.shape[1]
    return pl.pallas_call(
        _ada_kernel,
        grid=(n // ADA_TN,),
        in_specs=[
            pl.BlockSpec((SUBLANES, D_MODEL), lambda j: (0, 0)),
            pl.BlockSpec((D_MODEL, ADA_TN), lambda j: (0, j)),
            pl.BlockSpec((1, ADA_TN), lambda j: (0, j)),
        ],
        out_specs=pl.BlockSpec((SUBLANES, ADA_TN), lambda j: (0, j)),
        out_shape=jax.ShapeDtypeStruct((SUBLANES, n), F32),
        compiler_params=pltpu.CompilerParams(dimension_semantics=("parallel",)),
        name="ada_mod",
    )(c_pad, ada_w, ada_b)


def _t5_bucket_table():
    i = np.arange(BLOCK)[:, None]
    j = np.arange(BLOCK)[None, :]
    n = np.where(j <= i, i - j, i + BLOCK - j)
    assert WINDOW == BLOCK and n.min() >= 0 and n.max() < WINDOW
    max_exact = N_BUCKETS // 2
    large = max_exact + (np.log(np.maximum(n, 1) / max_exact) / np.log(MAX_DISTANCE / max_exact)
                         * (N_BUCKETS - max_exact)).astype(np.int32)
    large = np.minimum(large, N_BUCKETS - 1)
    bucket = np.where(n < max_exact, n, large).astype(np.int32)
    first = np.where(j <= i, bucket, -1)
    return np.stack([first, bucket]).astype(np.int32)


def _bias_kernel(idx_ref, rb_ref, o_ref):
    for v in range(2):
        idx = idx_ref[v]
        hits = [idx == b for b in range(N_BUCKETS)]
        for h in range(N_Q_HEADS):
            acc = jnp.full((BLOCK, BLOCK), -jnp.inf, F32)
            for b in range(N_BUCKETS):
                acc = jnp.where(hits[b], rb_ref[b, h] * LOG2E, acc)
            o_ref[v, h] = acc


def _bias_call(rel_bias):
    idx = jnp.asarray(_t5_bucket_table())
    return pl.pallas_call(
        _bias_kernel,
        in_specs=[
            pl.BlockSpec(memory_space=pltpu.VMEM),
            pl.BlockSpec(memory_space=pltpu.SMEM),
        ],
        out_specs=pl.BlockSpec(memory_space=pltpu.VMEM),
        out_shape=jax.ShapeDtypeStruct((2, N_Q_HEADS, BLOCK, BLOCK), F32),
        name="t5_bias",
    )(idx, rel_bias)


def _inproj_norm(x_ref, mod_ref, n1_ref, hb_ref):
    gain = n1_ref[...] * (1.0 + mod_ref[1:2, :])
    hb_ref[...] = (_rms(x_ref[...]) * gain + mod_ref[0:1, :]).astype(BF16)


def _inproj_project(hb_ref, w_ref, wdt_ref, cw_ref, cb_ref, q_ref, kt_ref, v_ref, gz_ref, act_ref, dt_ref, xe_ref):
    tm = hb_ref.shape[0]
    proj = lambda lo, hi: _dot(hb_ref[...], w_ref[:, lo:hi])
    tile = 2 * LANES
    for i, c0 in enumerate(range(0, XBC_WIDTH, tile)):
        cols = slice(c0, c0 + tile)
        xbc = proj(Z_END + c0, Z_END + c0 + tile)
        xe = jnp.concatenate([xe_ref[:, cols], xbc], axis=0)
        xe = xe.reshape(tm // SUBLANES + 1, SUBLANES, tile)
        first_row = lax.broadcasted_iota(jnp.int32, xe.shape, 1) == 0

        def shift_down_one(a):
            rot = pltpu.roll(a, 1, axis=1)
            return jnp.where(first_row, jnp.concatenate([rot[:1], rot[:-1]], axis=0), rot)

        taps = cw_ref[0:1, cols] * xe
        for k in range(1, CONV_K):
            taps = shift_down_one(taps) + cw_ref[k:k + 1, cols] * xe
        conv = taps[1:].reshape(tm, tile) + cb_ref[:, cols]
        xe_ref[:, cols] = xbc[tm - SUBLANES:, :]
        act_ref[:, cols] = _silu(conv).astype(BF16)
        if i < Q_END // tile:
            q_ref[:, i * tile:(i + 1) * tile] = proj(i * tile, (i + 1) * tile).astype(BF16)
        else:
            z0 = (i - Q_END // tile) * tile
            gz_ref[:, z0:z0 + tile] = _silu(proj(V_END + z0, V_END + z0 + tile)).astype(BF16)
    kt_ref[...] = (proj(Q_END, K_END) * (HEAD_DIM ** -0.5 * LOG2E)).T.astype(BF16)
    v_ref[...] = proj(K_END, V_END).astype(BF16)
    dt_ref[...] = _dot(hb_ref[...], wdt_ref[...])


def _inproj_kernel(x_ref, mod_ref, n1_ref, w_ref, wdt_ref, cw_ref, cb_ref, q_ref, kt_ref, v_ref, gz_ref, act_ref,
                   dt_ref, hb_ref, xe_ref):
    @pl.when(pl.program_id(1) == 0)
    def _():
        xe_ref[...] = jnp.zeros_like(xe_ref)

    _inproj_norm(x_ref, mod_ref, n1_ref, hb_ref)
    _inproj_project(hb_ref, w_ref, wdt_ref, cw_ref, cb_ref, q_ref, kt_ref, v_ref, gz_ref, act_ref, dt_ref, xe_ref)


def _inproj_call(x, mod, norm1, w_main, w_dt, conv_w, conv_b):
    b, s, _ = x.shape
    tm = INPROJ_TM
    row = lambda width: pl.BlockSpec((None, tm, width), lambda i, j: (i, j, 0))
    const = lambda shape: pl.BlockSpec(shape, lambda i, j: (0,) * len(shape))
    return pl.pallas_call(
        _inproj_kernel,
        grid=(b, s // tm),
        in_specs=[
            row(D_MODEL),
            pl.BlockSpec((None, 6, D_MODEL), lambda i, j: (i, 0, 0)),
            const((1, D_MODEL)),
            const((D_MODEL, XBC_END + SSM_HEADS)),
            const((D_MODEL, LANES)),
            const((CONV_K, XBC_WIDTH)), const((1, XBC_WIDTH)),
        ],
        out_specs=[row(ATTN_WIDTH), pl.BlockSpec((None, KV_WIDTH, tm), lambda i, j: (i, 0, j)), row(KV_WIDTH),
                   row(SSM_WIDTH), row(XBC_WIDTH), row(LANES)],
        out_shape=[
            jax.ShapeDtypeStruct((b, s, ATTN_WIDTH), BF16),
            jax.ShapeDtypeStruct((b, KV_WIDTH, s), BF16),
            jax.ShapeDtypeStruct((b, s, KV_WIDTH), BF16),
            jax.ShapeDtypeStruct((b, s, SSM_WIDTH), BF16),
            jax.ShapeDtypeStruct((b, s, XBC_WIDTH), BF16),
            jax.ShapeDtypeStruct((b, s, LANES), F32),
        ],
        scratch_shapes=[
            pltpu.VMEM((tm, D_MODEL), BF16),
            pltpu.VMEM((SUBLANES, XBC_WIDTH), F32),
        ],
        compiler_params=pltpu.CompilerParams(
            dimension_semantics=("parallel", "arbitrary"), vmem_limit_bytes=VMEM_LIMIT_BYTES),
        name="in_proj",
    )(x, mod, norm1, w_main, w_dt, conv_w, conv_b)


def _stage_keys_values(kt_ref, ktp_ref, v_ref, vp_ref, ka_ref, vv_ref):
    zeros = jnp.zeros((HEAD_DIM, BLOCK + MIX_ROWS), BF16)
    for g in range(N_KV_HEADS):
        rows = slice(g * HEAD_DIM, (g + 1) * HEAD_DIM)
        for par in range(2):
            live = slice(par * HEAD_DIM, (par + 1) * HEAD_DIM)
            dead = slice((1 - par) * HEAD_DIM, (2 - par) * HEAD_DIM)
            ka_ref[g, par, live, 0:BLOCK] = ktp_ref[rows, :]
            ka_ref[g, par, live, BLOCK:] = kt_ref[rows, :]
            ka_ref[g, par, dead, :] = zeros

    def duplicate_halves(v):
        v32 = pltpu.bitcast(v, jnp.uint32)
        swapped = pltpu.roll(v32, HEAD_DIM, axis=1)
        lane = lax.broadcasted_iota(jnp.int32, v32.shape, 1)
        first = lane < HEAD_DIM
        return (pltpu.bitcast(jnp.where(first, v32, swapped), BF16),
                pltpu.bitcast(jnp.where(first, swapped, v32), BF16))

    for src, r0, r1 in ((vp_ref, 0, BLOCK), (v_ref, BLOCK, BLOCK + MIX_ROWS)):
        for g, dup in enumerate(duplicate_halves(src[...])):
            vv_ref[g, r0:r1, 0:LANES] = dup


def _scores(q_ref, ka_ref, j):
    rows = pl.ds(j * BLOCK, BLOCK)
    win = pl.ds(j * BLOCK, 2 * BLOCK)
    out = []
    for h in range(N_Q_HEADS):
        t, par = h // 2, h % 2
        out.append(_dot(q_ref[rows, t * LANES:(t + 1) * LANES], ka_ref[t // 2, par, :, win]))
    return out


def _softmax(scores, bias_ref, sinks_ref, own):
    s = [jnp.where(own, sc[:, BLOCK:], sc[:, :BLOCK]) + bias_ref[h] for h, sc in enumerate(scores)]
    sink = [sinks_ref[h] * LOG2E for h in range(N_Q_HEADS)]
    m = [jnp.maximum(jnp.max(s[h], axis=-1, keepdims=True), sink[h]) for h in range(N_Q_HEADS)]
    p = [jnp.exp2(s[h] - m[h]) for h in range(N_Q_HEADS)]
    pcat = [jnp.concatenate([jnp.where(own, 0.0, ph), jnp.where(own, ph, 0.0)], axis=1).astype(BF16) for ph in p]
    sink_term = [jnp.exp2(sink[h] - m[h]) for h in range(N_Q_HEADS)]
    return pcat, sink_term


def _values(pcat, vv_ref, j):
    win = pl.ds(j * BLOCK, 2 * BLOCK)
    return [_dot(pcat[h], vv_ref[h // Q_PER_KV, win, :]) for h in range(N_Q_HEADS)]


def _finish_attention(o2, sink_term, an_ref, o_ref, j):
    rows = pl.ds(j * BLOCK, BLOCK)
    heads = [o2[h][:, :LANES] / (o2[h][:, LANES:] + sink_term[h]) for h in range(N_Q_HEADS)]
    lane = lax.broadcasted_iota(jnp.int32, (BLOCK, LANES), 1)
    tiles = [jnp.where(lane < HEAD_DIM, heads[2 * t], heads[2 * t + 1]) for t in range(N_Q_HEADS // 2)]
    ssq = sum(jnp.sum(tl * tl, axis=-1, keepdims=True) for tl in tiles)
    inv = lax.rsqrt(ssq * (1.0 / ATTN_WIDTH) + EPS)
    for t, tl in enumerate(tiles):
        cols = slice(t * LANES, (t + 1) * LANES)
        o_ref[rows, cols] = (tl * inv * an_ref[:, cols]).astype(BF16)


def _mixer_steps(first_tile, q_ref, kt_ref, ktp_ref, v_ref, vp_ref, gz_ref, act_ref, dt_ref, bias_ref, sinks_ref,
                 dtb_ref, alog_ref, dskip_ref, an_ref, sn_ref, expand_ref, o_ref, state_ref, ka_ref, vv_ref,
                 xbd_ref):
    nch = MIX_CHUNKS
    row_i = lax.broadcasted_iota(jnp.int32, (BLOCK, BLOCK), 0)
    col_i = lax.broadcasted_iota(jnp.int32, (BLOCK, BLOCK), 1)
    own = col_i <= row_i
    bias_of = lambda j: bias_ref.at[0 if (first_tile and j == 0) else 1]
    grp = lambda a, j, g: a[j * CHUNK:(j + 1) * CHUNK, g * GROUP_WIDTH:(g + 1) * GROUP_WIDTH]
    softmax = lambda j, sc: _softmax(sc, bias_of(j), sinks_ref, own)
    finish = lambda j, vals, soft: _finish_attention(vals, soft[1], an_ref, o_ref, j)

    _stage_keys_values(kt_ref, ktp_ref, v_ref, vp_ref, ka_ref, vv_ref)
    dt = jax.nn.softplus(dt_ref[...] + dtb_ref[...])
    dta = dt * (-LOG2E * jnp.exp(alog_ref[...]))
    tri = own.astype(BF16)
    dta_hi, dta_lo = _split2(dta)
    scores = {0: _scores(q_ref, ka_ref, 0), 1: _scores(q_ref, ka_ref, 1)}
    acs = [_dot(tri, dta_hi[j * CHUNK:(j + 1) * CHUNK]) + _dot(tri, dta_lo[j * CHUNK:(j + 1) * CHUNK])
           for j in range(nch)]
    yield

    soft = {0: softmax(0, scores.pop(0))}
    acs_t = [a.T for a in acs]
    dt_t = [dt[j * CHUNK:(j + 1) * CHUNK].T for j in range(nch)]
    dec_in = [jnp.exp2(a) for a in acs]
    dec_out = jnp.concatenate([jnp.exp2(a[CHUNK - 1:CHUNK, :] - a) for a in acs], axis=0)
    state_in = (dt * dec_out).astype(BF16)
    decay_in = jnp.concatenate(dec_in, axis=0).astype(BF16)
    decay_tail = [_split2(d[CHUNK - SUBLANES:, :]) for d in dec_in]
    yield

    vals = {0: _values(soft[0][0], vv_ref, 0)}
    expand = expand_ref[...]
    state_in_x = _dot(state_in, expand)
    decay_in_x = _dot(decay_in, expand)
    chunk_decay = [(_dot(hi, expand) + _dot(lo, expand))[SUBLANES - 1:SUBLANES, :] for hi, lo in decay_tail]
    scores[2] = _scores(q_ref, ka_ref, 2)
    soft[1] = softmax(1, scores.pop(1))
    finish(0, vals.pop(0), soft.pop(0))
    yield

    vals[1] = _values(soft[1][0], vv_ref, 1)
    scores[3] = _scores(q_ref, ka_ref, 3)
    xs = act_ref[:, :SSM_WIDTH].astype(F32)
    xdec_b = (xs * state_in_x).astype(BF16)
    lane_head = lax.broadcasted_iota(jnp.int32, (CHUNK, GROUP_WIDTH), 1) // SSM_HEAD_DIM
    for j in range(nch):
        for g in range(SSM_GROUPS):
            xg = act_ref[j * CHUNK:(j + 1) * CHUNK, g * GROUP_WIDTH:(g + 1) * GROUP_WIDTH]
            for r in range(HEADS_PER_GROUP):
                xbd_ref[j, g, r * CHUNK:(r + 1) * CHUNK, :] = jnp.where(lane_head == r, xg, jnp.zeros_like(xg))
    soft[2] = softmax(2, scores.pop(2))
    finish(1, vals.pop(1), soft.pop(1))
    yield

    bc_b = act_ref[:, SSM_WIDTH:]
    bm = [[bc_b[j * CHUNK:(j + 1) * CHUNK, g * D_STATE:(g + 1) * D_STATE] for g in range(SSM_GROUPS)]
          for j in range(nch)]
    cm = [[bc_b[j * CHUNK:(j + 1) * CHUNK, (SSM_GROUPS + g) * D_STATE:(SSM_GROUPS + g + 1) * D_STATE]
           for g in range(SSM_GROUPS)] for j in range(nch)]
    cbm = [[_dot_nt(cm[j][g], bm[j][g]) for g in range(SSM_GROUPS)] for j in range(nch)]
    d_state = [[_dot_tn(bm[j][g], grp(xdec_b, j, g)) for g in range(SSM_GROUPS)] for j in range(nch)]
    vals[2] = _values(soft[2][0], vv_ref, 2)
    soft[3] = softmax(3, scores.pop(3))

    def intra_weights(j):
        per_group = []
        for g in range(SSM_GROUPS):
            w = []
            for r in range(HEADS_PER_GROUP):
                h = g * HEADS_PER_GROUP + r
                seg = acs[j][:, h:h + 1] - acs_t[j][h:h + 1, :]
                lmat = jnp.exp2(jnp.where(own, seg, -jnp.inf))
                w.append(cbm[j][g] * lmat * dt_t[j][h:h + 1, :])
            per_group.append((jnp.concatenate(w[:2], axis=1).astype(BF16),
                              jnp.concatenate(w[2:], axis=1).astype(BF16)))
        return per_group

    def intra_product(j, w):
        return [_dot(w[g][0], xbd_ref[j, g, 0:2 * CHUNK, :]) + _dot(w[g][1], xbd_ref[j, g, 2 * CHUNK:, :])
                for g in range(SSM_GROUPS)]

    yield
    w = {0: intra_weights(0)}
    finish(2, vals.pop(2), soft.pop(2))
    yield

    vals[3] = _values(soft[3][0], vv_ref, 3)
    y_diag = [intra_product(0, w.pop(0))]
    w[1] = intra_weights(1)
    yield

    y_diag.append(intra_product(1, w.pop(1)))
    w[2] = intra_weights(2)
    finish(3, vals.pop(3), soft.pop(3))
    yield

    y_diag.append(intra_product(2, w.pop(2)))
    w[3] = intra_weights(3)
    yield

    y_diag.append(intra_product(3, w.pop(3)))
    entering = []
    for g in range(SSM_GROUPS):
        state = state_ref[g]
        per_chunk = []
        for j in range(nch):
            per_chunk.append(state.astype(BF16))
            state = state * chunk_decay[j][:, g * GROUP_WIDTH:(g + 1) * GROUP_WIDTH] + d_state[j][g]
        state_ref[g] = state
        entering.append(per_chunk)
    yield

    y_rows = [jnp.concatenate([y_diag[j][g] + _dot(cm[j][g], entering[g][j]) * grp(decay_in_x, j, g)
                               for g in range(SSM_GROUPS)], axis=-1) for j in range(nch)]
    y = jnp.concatenate(y_rows, axis=0) + dskip_ref[...] * xs
    y = y * gz_ref[...].astype(F32)
    y = jnp.concatenate([_rms(y[:, :GROUP_WIDTH]), _rms(y[:, GROUP_WIDTH:])], axis=-1)
    o_ref[:, ATTN_WIDTH:] = (y * sn_ref[...]).astype(BF16)


FFN_PHASES = D_FF // FFN_FC + D_MODEL // FFN_FC


def _ffn_steps(mix_ref, x_ref, mod_ref, wo_ref, n2_ref, wgu_ref, wd_ref, fn_ref, o_ref, h_ref, a_ref):
    gain2 = n2_ref[...] * (1.0 + mod_ref[4:5, :])
    x1 = x_ref[...] + mod_ref[2:3, :] * _dot(mix_ref[...], wo_ref[...])
    h_ref[...] = (_rms(x1) * gain2 + mod_ref[3:4, :]).astype(BF16)
    o_ref[...] = x1
    yield
    for c0 in range(0, D_FF, FFN_FC):
        hb = h_ref[...]
        g = _dot(hb, wgu_ref[:, c0:c0 + FFN_FC])
        u = _dot(hb, wgu_ref[:, D_FF + c0:D_FF + c0 + FFN_FC])
        a_ref[:, c0:c0 + FFN_FC] = (_silu(g) * u).astype(BF16)
        yield
    for c0 in range(0, D_MODEL, FFN_FC):
        cols = slice(c0, c0 + FFN_FC)
        o_ref[:, cols] = o_ref[:, cols] + mod_ref[5:6, cols] * _dot(a_ref[...], wd_ref[:, cols])
        if c0 + FFN_FC == D_MODEL:
            o_ref[...] = _rms(o_ref[...]) * fn_ref[...]
        yield


def _mixer_ffn_kernel(q_ref, kt_ref, ktp_ref, v_ref, vp_ref, gz_ref, act_ref, dt_ref, bias_ref, sinks_ref, dtb_ref,
                      alog_ref, dskip_ref, an_ref, sn_ref, expand_ref, x_ref, mod_ref, wo_ref, n2_ref, wgu_ref,
                      wd_ref, fn_ref, o_ref, state_ref, ka_ref, vv_ref, xbd_ref, mix_ref, h_ref, a_ref):
    s = pl.program_id(1)
    last = pl.num_programs(1) - 1
    mixer = lambda first: _mixer_steps(first, q_ref, kt_ref, ktp_ref, v_ref, vp_ref, gz_ref, act_ref, dt_ref,
                                       bias_ref, sinks_ref, dtb_ref, alog_ref, dskip_ref, an_ref, sn_ref,
                                       expand_ref, mix_ref, state_ref, ka_ref, vv_ref, xbd_ref)
    ffn = lambda: _ffn_steps(mix_ref, x_ref, mod_ref, wo_ref, n2_ref, wgu_ref, wd_ref, fn_ref, o_ref, h_ref, a_ref)

    @pl.when(s == 0)
    def _():
        state_ref[...] = jnp.zeros_like(state_ref)
        vv_ref[:, :, LANES:] = jnp.ones((N_KV_HEADS, BLOCK + MIX_ROWS, LANES), BF16)
        for _ in mixer(True):
            pass

    @pl.when((s > 0) & (s < last))
    def _():
        ffn_steps, issued = ffn(), 0
        next(ffn_steps)
        for k, _ in enumerate(mixer(False)):
            while issued < ((k + 1) * (FFN_PHASES - 1)) // MIXER_YIELDS:
                next(ffn_steps)
                issued += 1
        assert k == MIXER_YIELDS - 1
        for _ in ffn_steps:
            pass

    @pl.when(s == last)
    def _():
        for _ in ffn():
            pass


def _mixer_ffn_call(q, kt, v, gz, act, dt, bias, sinks, dt_bias, a_log, d_skip_x, attn_norm, ssm_norm, expand,
                    x, mod, w_o, norm2, w_gu, w_down, final_norm):
    b, s, _ = q.shape
    rows = MIX_ROWS
    n_tiles = s // rows
    mix_tile = lambda i, c: jnp.minimum(c, n_tiles - 1)
    ffn_tile = lambda i, c: jnp.maximum(c - 1, 0)
    prev_blk = lambda i, c: jnp.maximum(mix_tile(i, c) * MIX_CHUNKS - 1, 0)
    mrow = lambda width: pl.BlockSpec((None, rows, width), lambda i, c: (i, mix_tile(i, c), 0))
    frow = lambda: pl.BlockSpec((None, rows, D_MODEL), lambda i, c: (i, ffn_tile(i, c), 0))
    const = lambda shape: pl.BlockSpec(shape, lambda i, c: (0,) * len(shape))
    resident = lambda shape: pl.BlockSpec(shape, lambda i, c: (0,) * len(shape), pipeline_mode=pl.Buffered(1))
    return pl.pallas_call(
        _mixer_ffn_kernel,
        grid=(b, n_tiles + 1),
        in_specs=[
            mrow(ATTN_WIDTH),
            pl.BlockSpec((None, KV_WIDTH, rows), lambda i, c: (i, 0, mix_tile(i, c))),
            pl.BlockSpec((None, KV_WIDTH, BLOCK), lambda i, c: (i, 0, prev_blk(i, c))),
            mrow(KV_WIDTH),
            pl.BlockSpec((None, BLOCK, KV_WIDTH), lambda i, c: (i, prev_blk(i, c), 0)),
            mrow(SSM_WIDTH), mrow(XBC_WIDTH), mrow(LANES),
            const((2, N_Q_HEADS, BLOCK, BLOCK)),
            pl.BlockSpec(memory_space=pltpu.SMEM),
            const((1, LANES)), const((1, LANES)),
            const((1, SSM_WIDTH)), const((1, ATTN_WIDTH)), const((1, SSM_WIDTH)),
            const((LANES, SSM_WIDTH)),
            frow(),
            pl.BlockSpec((None, 6, D_MODEL), lambda i, c: (i, 0, 0)),
            resident((D_MODEL, D_MODEL)), const((1, D_MODEL)),
            resident((D_MODEL, 2 * D_FF)), resident((D_FF, D_MODEL)), const((1, D_MODEL)),
        ],
        out_specs=frow(),
        out_shape=jax.ShapeDtypeStruct((b, s, D_MODEL), F32),
        scratch_shapes=[
            pltpu.VMEM((SSM_GROUPS, D_STATE, GROUP_WIDTH), F32),
            pltpu.VMEM((N_KV_HEADS, 2, KV_WIDTH, BLOCK + rows), BF16),
            pltpu.VMEM((N_KV_HEADS, BLOCK + rows, 2 * LANES), BF16),
            pltpu.VMEM((MIX_CHUNKS, SSM_GROUPS, HEADS_PER_GROUP * CHUNK, GROUP_WIDTH), BF16),
            pltpu.VMEM((rows, ATTN_WIDTH + SSM_WIDTH), BF16),
            pltpu.VMEM((rows, D_MODEL), BF16),
            pltpu.VMEM((rows, D_FF), BF16),
        ],
        compiler_params=pltpu.CompilerParams(
            dimension_semantics=("parallel", "arbitrary"), vmem_limit_bytes=VMEM_LIMIT_BYTES),
        name="mixer_ffn",
    )(q, kt, kt, v, v, gz, act, dt, bias, sinks, dt_bias, a_log, d_skip_x, attn_norm, ssm_norm, expand,
      x, mod, w_o, norm2, w_gu, w_down, final_norm)


def _head_expand_matrix():
    e = np.zeros((LANES, SSM_WIDTH), np.float32)
    for h in range(SSM_HEADS):
        e[h, h * SSM_HEAD_DIM:(h + 1) * SSM_HEAD_DIM] = 1.0
    return e


def kernel(x, c, ada_w, ada_b, norm1, w_in, conv_w, conv_b, dt_bias, A_log, D_skip, sinks, attn_out_norm,
           ssm_out_norm, w_o, norm2, w_gate_up, w_down, rel_bias, final_norm):
    assert ada_w.shape[0] == 1, "the final RMSNorm is fused into the (single) layer's FFN call"
    b = x.shape[0]
    bias = _bias_call(rel_bias)
    expand = jnp.asarray(_head_expand_matrix(), BF16)
    pad_heads = lambda v: jnp.pad(v, (0, LANES - SSM_HEADS))[None, :]
    c_pad = jnp.pad(c, ((0, SUBLANES - b), (0, 0)))
    mod = _ada_call(c_pad, ada_w[0], ada_b[0][None, :])[:b].reshape(b, 6, D_MODEL)
    w_main = w_in.astype(BF16)[0]
    w_dt = jnp.pad(w_main[:, XBC_END:], ((0, 0), (0, LANES - SSM_HEADS)))
    q, kt, v, gz, act, dt = _inproj_call(x, mod, norm1[0][None, :], w_main, w_dt, conv_w[0], conv_b[0][None, :])
    return _mixer_ffn_call(
        q, kt, v, gz, act, dt, bias, sinks[0], pad_heads(dt_bias[0]), pad_heads(A_log[0]),
        jnp.repeat(D_skip[0], SSM_HEAD_DIM)[None, :], attn_out_norm[0][None, :], ssm_out_norm[0][None, :], expand,
        x, mod, w_o[0].astype(BF16), norm2[0][None, :], w_gate_up[0].astype(BF16), w_down[0].astype(BF16),
        final_norm[None, :])
```

```python
import numpy as np
import jax
import jax.numpy as jnp
from jax import lax
from jax.experimental import pallas as pl
from jax.experimental.pallas import tpu as pltpu

F32 = jnp.float32
BF16 = jnp.bfloat16

D_MODEL = 1024
HEAD_DIM = 64
N_Q_HEADS = 8
N_KV_HEADS = 2
Q_PER_KV = N_Q_HEADS // N_KV_HEADS
ATTN_WIDTH = N_Q_HEADS * HEAD_DIM
KV_WIDTH = N_KV_HEADS * HEAD_DIM
WINDOW = 128
BLOCK = 128
N_BUCKETS = 32
MAX_DISTANCE = 128

SSM_HEAD_DIM = 64
SSM_HEADS = 8
SSM_GROUPS = 2
HEADS_PER_GROUP = SSM_HEADS // SSM_GROUPS
SSM_WIDTH = SSM_HEADS * SSM_HEAD_DIM
GROUP_WIDTH = SSM_WIDTH // SSM_GROUPS
D_STATE = 128
CONV_K = 4
CHUNK = 128
XBC_WIDTH = SSM_WIDTH + 2 * SSM_GROUPS * D_STATE
D_FF = 2816
EPS = 1e-6
LOG2E = float(np.log2(np.e))

LANES = 128
SUBLANES = 8
VMEM_LIMIT_BYTES = 56 * 1024 * 1024

ADA_TN = 1536
INPROJ_TM = 1024
FFN_FC = 256
MIX_CHUNKS = 4
MIXER_YIELDS = 10
MIX_ROWS = MIX_CHUNKS * CHUNK

Q_END = ATTN_WIDTH
K_END = Q_END + KV_WIDTH
V_END = K_END + KV_WIDTH
Z_END = V_END + SSM_WIDTH
XBC_END = Z_END + XBC_WIDTH

assert BLOCK == CHUNK == LANES and KV_WIDTH == LANES and 2 * HEAD_DIM == LANES


def _silu(x):
    return x * jax.nn.sigmoid(x)


def _rms(x):
    return x * lax.rsqrt(jnp.mean(x * x, axis=-1, keepdims=True) + EPS)


def _split2(x):
    hi = x.astype(BF16)
    lo = (x - hi.astype(F32)).astype(BF16)
    return hi, lo


def _dot(a, b):
    return jnp.dot(a, b, preferred_element_type=F32)


def _dot_nt(a, b):
    return lax.dot_general(a, b, (((1,), (1,)), ((), ())), preferred_element_type=F32)


def _dot_tn(a, b):
    return lax.dot_general(a, b, (((0,), (0,)), ((), ())), preferred_element_type=F32)


def _ada_kernel(c_ref, w_ref, b_ref, o_ref):
    cond = _silu(c_ref[...])
    o_ref[...] = _dot(cond.astype(BF16), w_ref[...].astype(BF16)) + b_ref[...]


def _ada_call(c_pad, ada_w, ada_b):
    n = ada_w.shape[1]
    return pl.pallas_call(
        _ada_kernel,
        grid=(n // ADA_TN,),
        in_specs=[
            pl.BlockSpec((SUBLANES, D_MODEL), lambda j: (0, 0)),
            pl.BlockSpec((D_MODEL, ADA_TN), lambda j: (0, j)),
            pl.BlockSpec((1, ADA_TN), lambda j: (0, j)),
        ],
        out_specs=pl.BlockSpec((SUBLANES, ADA_TN), lambda j: (0, j)),
        out_shape=jax.ShapeDtypeStruct((SUBLANES, n), F32),
        compiler_params=pltpu.CompilerParams(dimension_semantics=("parallel",)),
        name="ada_mod",
    )(c_pad, ada_w, ada_b)


def _t5_bucket_table():
    i = np.arange(BLOCK)[:, None]
    j = np.arange(BLOCK)[None, :]
    n = np.where(j <= i, i - j, i + BLOCK - j)
    assert WINDOW == BLOCK and n.min() >= 0 and n.max() < WINDOW
    max_exact = N_BUCKETS // 2
    large = max_exact + (np.log(np.maximum(n, 1) / max_exact) / np.log(MAX_DISTANCE / max_exact)
                         * (N_BUCKETS - max_exact)).astype(np.int32)
    large = np.minimum(large, N_BUCKETS - 1)
    bucket = np.where(n < max_exact, n, large).astype(np.int32)
    first = np.where(j <= i, bucket, -1)
    return np.stack([first, bucket]).astype(np.int32)


def _bias_kernel(idx_ref, rb_ref, o_ref):
    for v in range(2):
        idx = idx_ref[v]
        hits = [idx == b for b in range(N_BUCKETS)]
        for h in range(N_Q_HEADS):
            acc = jnp.full((BLOCK, BLOCK), -jnp.inf, F32)
            for b in range(N_BUCKETS):
                acc = jnp.where(hits[b], rb_ref[b, h] * LOG2E, acc)
            o_ref[v, h] = acc


def _bias_call(rel_bias):
    idx = jnp.asarray(_t5_bucket_table())
    return pl.pallas_call(
        _bias_kernel,
        in_specs=[
            pl.BlockSpec(memory_space=pltpu.VMEM),
            pl.BlockSpec(memory_space=pltpu.SMEM),
        ],
        out_specs=pl.BlockSpec(memory_space=pltpu.VMEM),
        out_shape=jax.ShapeDtypeStruct((2, N_Q_HEADS, BLOCK, BLOCK), F32),
        name="t5_bias",
    )(idx, rel_bias)


def _inproj_norm(x_ref, mod_ref, n1_ref, hb_ref):
    gain = n1_ref[...] * (1.0 + mod_ref[1:2, :])
    hb_ref[...] = (_rms(x_ref[...]) * gain + mod_ref[0:1, :]).astype(BF16)


def _inproj_project(hb_ref, w_ref, wdt_ref, cw_ref, cb_ref, q_ref, kt_ref, v_ref, gz_ref, act_ref, dt_ref, xe_ref):
    tm = hb_ref.shape[0]
    proj = lambda lo, hi: _dot(hb_ref[...], w_ref[:, lo:hi])
    tile = 2 * LANES
    for i, c0 in enumerate(range(0, XBC_WIDTH, tile)):
        cols = slice(c0, c0 + tile)
        xbc = proj(Z_END + c0, Z_END + c0 + tile)
        xe = jnp.concatenate([xe_ref[:, cols], xbc], axis=0)
        xe = xe.reshape(tm // SUBLANES + 1, SUBLANES, tile)
        first_row = lax.broadcasted_iota(jnp.int32, xe.shape, 1) == 0

        def shift_down_one(a):
            rot = pltpu.roll(a, 1, axis=1)
            return jnp.where(first_row, jnp.concatenate([rot[:1], rot[:-1]], axis=0), rot)

        taps = cw_ref[0:1, cols] * xe
        for k in range(1, CONV_K):
            taps = shift_down_one(taps) + cw_ref[k:k + 1, cols] * xe
        conv = taps[1:].reshape(tm, tile) + cb_ref[:, cols]
        xe_ref[:, cols] = xbc[tm - SUBLANES:, :]
        act_ref[:, cols] = _silu(conv).astype(BF16)
        if i < Q_END // tile:
            q_ref[:, i * tile:(i + 1) * tile] = proj(i * tile, (i + 1) * tile).astype(BF16)
        else:
            z0 = (i - Q_END // tile) * tile
            gz_ref[:, z0:z0 + tile] = _silu(proj(V_END + z0, V_END + z0 + tile)).astype(BF16)
    kt_ref[...] = (proj(Q_END, K_END) * (HEAD_DIM ** -0.5 * LOG2E)).T.astype(BF16)
    v_ref[...] = proj(K_END, V_END).astype(BF16)
    dt_ref[...] = _dot(hb_ref[...], wdt_ref[...])


def _inproj_kernel(x_ref, mod_ref, n1_ref, w_ref, wdt_ref, cw_ref, cb_ref, q_ref, kt_ref, v_ref, gz_ref, act_ref,
                   dt_ref, hb_ref, xe_ref):
    @pl.when(pl.program_id(1) == 0)
    def _():
        xe_ref[...] = jnp.zeros_like(xe_ref)

    _inproj_norm(x_ref, mod_ref, n1_ref, hb_ref)
    _inproj_project(hb_ref, w_ref, wdt_ref, cw_ref, cb_ref, q_ref, kt_ref, v_ref, gz_ref, act_ref, dt_ref, xe_ref)


def _inproj_call(x, mod, norm1, w_main, w_dt, conv_w, conv_b):
    b, s, _ = x.shape
    tm = INPROJ_TM
    row = lambda width: pl.BlockSpec((None, tm, width), lambda i, j: (i, j, 0))
    const = lambda shape: pl.BlockSpec(shape, lambda i, j: (0,) * len(shape))
    return pl.pallas_call(
        _inproj_kernel,
        grid=(b, s // tm),
        in_specs=[
            row(D_MODEL),
            pl.BlockSpec((None, 6, D_MODEL), lambda i, j: (i, 0, 0)),
            const((1, D_MODEL)),
            const((D_MODEL, XBC_END + SSM_HEADS)),
            const((D_MODEL, LANES)),
            const((CONV_K, XBC_WIDTH)), const((1, XBC_WIDTH)),
        ],
        out_specs=[row(ATTN_WIDTH), pl.BlockSpec((None, KV_WIDTH, tm), lambda i, j: (i, 0, j)), row(KV_WIDTH),
                   row(SSM_WIDTH), row(XBC_WIDTH), row(LANES)],
        out_shape=[
            jax.ShapeDtypeStruct((b, s, ATTN_WIDTH), BF16),
            jax.ShapeDtypeStruct((b, KV_WIDTH, s), BF16),
            jax.ShapeDtypeStruct((b, s, KV_WIDTH), BF16),
            jax.ShapeDtypeStruct((b, s, SSM_WIDTH), BF16),
            jax.ShapeDtypeStruct((b, s, XBC_WIDTH), BF16),
            jax.ShapeDtypeStruct((b, s, LANES), F32),
        ],
        scratch_shapes=[
            pltpu.VMEM((tm, D_MODEL), BF16),
            pltpu.VMEM((SUBLANES, XBC_WIDTH), F32),
        ],
        compiler_params=pltpu.CompilerParams(
            dimension_semantics=("parallel", "arbitrary"), vmem_limit_bytes=VMEM_LIMIT_BYTES),
        name="in_proj",
    )(x, mod, norm1, w_main, w_dt, conv_w, conv_b)


def _stage_keys_values(kt_ref, ktp_ref, v_ref, vp_ref, ka_ref, vv_ref):
    zeros = jnp.zeros((HEAD_DIM, BLOCK + MIX_ROWS), BF16)
    for g in range(N_KV_HEADS):
        rows = slice(g * HEAD_DIM, (g + 1) * HEAD_DIM)
        for par in range(2):
            live = slice(par * HEAD_DIM, (par + 1) * HEAD_DIM)
            dead = slice((1 - par) * HEAD_DIM, (2 - par) * HEAD_DIM)
            ka_ref[g, par, live, 0:BLOCK] = ktp_ref[rows, :]
            ka_ref[g, par, live, BLOCK:] = kt_ref[rows, :]
            ka_ref[g, par, dead, :] = zeros

    def duplicate_halves(v):
        v32 = pltpu.bitcast(v, jnp.uint32)
        swapped = pltpu.roll(v32, HEAD_DIM, axis=1)
        lane = lax.broadcasted_iota(jnp.int32, v32.shape, 1)
        first = lane < HEAD_DIM
        return (pltpu.bitcast(jnp.where(first, v32, swapped), BF16),
                pltpu.bitcast(jnp.where(first, swapped, v32), BF16))

    for src, r0, r1 in ((vp_ref, 0, BLOCK), (v_ref, BLOCK, BLOCK + MIX_ROWS)):
        for g, dup in enumerate(duplicate_halves(src[...])):
            vv_ref[g, r0:r1, 0:LANES] = dup


def _scores(q_ref, ka_ref, j):
    rows = pl.ds(j * BLOCK, BLOCK)
    win = pl.ds(j * BLOCK, 2 * BLOCK)
    out = []
    for h in range(N_Q_HEADS):
        t, par = h // 2, h % 2
        out.append(_dot(q_ref[rows, t * LANES:(t + 1) * LANES], ka_ref[t // 2, par, :, win]))
    return out


def _softmax(scores, bias_ref, sinks_ref, own):
    s = [jnp.where(own, sc[:, BLOCK:], sc[:, :BLOCK]) + bias_ref[h] for h, sc in enumerate(scores)]
    sink = [sinks_ref[h] * LOG2E for h in range(N_Q_HEADS)]
    m = [jnp.maximum(jnp.max(s[h], axis=-1, keepdims=True), sink[h]) for h in range(N_Q_HEADS)]
    p = [jnp.exp2(s[h] - m[h]) for h in range(N_Q_HEADS)]
    zero = jnp.zeros((BLOCK, BLOCK), BF16)
    pb = [ph.astype(BF16) for ph in p]
    pcat = [jnp.concatenate([jnp.where(own, zero, b), jnp.where(own, b, zero)], axis=1) for b in pb]
    sink_term = [jnp.exp2(sink[h] - m[h]) for h in range(N_Q_HEADS)]
    return pcat, sink_term


def _values(pcat, vv_ref, j):
    win = pl.ds(j * BLOCK, 2 * BLOCK)
    return [_dot(pcat[h], vv_ref[h // Q_PER_KV, win, :]) for h in range(N_Q_HEADS)]


def _finish_attention(o2, sink_term, an_ref, o_ref, j):
    rows = pl.ds(j * BLOCK, BLOCK)
    heads = [o2[h][:, :LANES] / (o2[h][:, LANES:] + sink_term[h]) for h in range(N_Q_HEADS)]
    lane = lax.broadcasted_iota(jnp.int32, (BLOCK, LANES), 1)
    tiles = [jnp.where(lane < HEAD_DIM, heads[2 * t], heads[2 * t + 1]) for t in range(N_Q_HEADS // 2)]
    ssq = sum(jnp.sum(tl * tl, axis=-1, keepdims=True) for tl in tiles)
    inv = lax.rsqrt(ssq * (1.0 / ATTN_WIDTH) + EPS)
    for t, tl in enumerate(tiles):
        cols = slice(t * LANES, (t + 1) * LANES)
        o_ref[rows, cols] = (tl * inv * an_ref[:, cols]).astype(BF16)


def _mixer_steps(first_tile, q_ref, kt_ref, ktp_ref, v_ref, vp_ref, gz_ref, act_ref, dt_ref, bias_ref, sinks_ref,
                 dtb_ref, alog_ref, dskip_ref, an_ref, sn_ref, expand_ref, o_ref, state_ref, ka_ref, vv_ref,
                 xbd_ref):
    nch = MIX_CHUNKS
    row_i = lax.broadcasted_iota(jnp.int32, (BLOCK, BLOCK), 0)
    col_i = lax.broadcasted_iota(jnp.int32, (BLOCK, BLOCK), 1)
    own = col_i <= row_i
    bias_of = lambda j: bias_ref.at[0 if (first_tile and j == 0) else 1]
    grp = lambda a, j, g: a[j * CHUNK:(j + 1) * CHUNK, g * GROUP_WIDTH:(g + 1) * GROUP_WIDTH]
    softmax = lambda j, sc: _softmax(sc, bias_of(j), sinks_ref, own)
    finish = lambda j, vals, soft: _finish_attention(vals, soft[1], an_ref, o_ref, j)

    _stage_keys_values(kt_ref, ktp_ref, v_ref, vp_ref, ka_ref, vv_ref)
    dt = jax.nn.softplus(dt_ref[...] + dtb_ref[...])
    dta = dt * (-LOG2E * jnp.exp(alog_ref[...]))
    tri = own.astype(BF16)
    dta_hi, dta_lo = _split2(dta)
    scores = {0: _scores(q_ref, ka_ref, 0), 1: _scores(q_ref, ka_ref, 1)}
    acs = [_dot(tri, dta_hi[j * CHUNK:(j + 1) * CHUNK]) + _dot(tri, dta_lo[j * CHUNK:(j + 1) * CHUNK])
           for j in range(nch)]
    yield

    soft = {0: softmax(0, scores.pop(0))}
    acs_dt_t = [(a - jnp.log2(dt[j * CHUNK:(j + 1) * CHUNK])).T for j, a in enumerate(acs)]
    dec_in = [jnp.exp2(a) for a in acs]
    dec_out = jnp.concatenate([jnp.exp2(a[CHUNK - 1:CHUNK, :] - a) for a in acs], axis=0)
    state_in = (dt * dec_out).astype(BF16)
    decay_in = jnp.concatenate(dec_in, axis=0).astype(BF16)
    decay_tail = [_split2(d[CHUNK - SUBLANES:, :]) for d in dec_in]
    yield

    vals = {0: _values(soft[0][0], vv_ref, 0)}
    expand = expand_ref[...]
    state_in_x = _dot(state_in, expand)
    decay_in_x = _dot(decay_in, expand)
    chunk_decay = [(_dot(hi, expand) + _dot(lo, expand))[SUBLANES - 1:SUBLANES, :] for hi, lo in decay_tail]
    scores[2] = _scores(q_ref, ka_ref, 2)
    soft[1] = softmax(1, scores.pop(1))
    finish(0, vals.pop(0), soft.pop(0))
    yield

    vals[1] = _values(soft[1][0], vv_ref, 1)
    scores[3] = _scores(q_ref, ka_ref, 3)
    xs = act_ref[:, :SSM_WIDTH].astype(F32)
    xdec_b = (xs * state_in_x).astype(BF16)
    lane_head = lax.broadcasted_iota(jnp.int32, (CHUNK, GROUP_WIDTH), 1) // SSM_HEAD_DIM
    for j in range(nch):
        for g in range(SSM_GROUPS):
            xg = act_ref[j * CHUNK:(j + 1) * CHUNK, g * GROUP_WIDTH:(g + 1) * GROUP_WIDTH]
            for r in range(HEADS_PER_GROUP):
                xbd_ref[j, g, r * CHUNK:(r + 1) * CHUNK, :] = jnp.where(lane_head == r, xg, jnp.zeros_like(xg))
    soft[2] = softmax(2, scores.pop(2))
    finish(1, vals.pop(1), soft.pop(1))
    yield

    bc_b = act_ref[:, SSM_WIDTH:]
    bm = [[bc_b[j * CHUNK:(j + 1) * CHUNK, g * D_STATE:(g + 1) * D_STATE] for g in range(SSM_GROUPS)]
          for j in range(nch)]
    cm = [[bc_b[j * CHUNK:(j + 1) * CHUNK, (SSM_GROUPS + g) * D_STATE:(SSM_GROUPS + g + 1) * D_STATE]
           for g in range(SSM_GROUPS)] for j in range(nch)]
    cbm = [[_dot_nt(cm[j][g], bm[j][g]) for g in range(SSM_GROUPS)] for j in range(nch)]
    d_state = [[_dot_tn(bm[j][g], grp(xdec_b, j, g)) for g in range(SSM_GROUPS)] for j in range(nch)]
    vals[2] = _values(soft[2][0], vv_ref, 2)
    soft[3] = softmax(3, scores.pop(3))

    def intra_weights(j):
        per_group = []
        for g in range(SSM_GROUPS):
            w = []
            for r in range(HEADS_PER_GROUP):
                h = g * HEADS_PER_GROUP + r
                seg = acs[j][:, h:h + 1] - acs_dt_t[j][h:h + 1, :]
                w.append(cbm[j][g] * jnp.exp2(jnp.where(own, seg, -jnp.inf)))
            per_group.append((jnp.concatenate(w[:2], axis=1).astype(BF16),
                              jnp.concatenate(w[2:], axis=1).astype(BF16)))
        return per_group

    def intra_product(j, w):
        return [_dot(w[g][0], xbd_ref[j, g, 0:2 * CHUNK, :]) + _dot(w[g][1], xbd_ref[j, g, 2 * CHUNK:, :])
                for g in range(SSM_GROUPS)]

    yield
    w = {0: intra_weights(0)}
    finish(2, vals.pop(2), soft.pop(2))
    yield

    vals[3] = _values(soft[3][0], vv_ref, 3)
    y_diag = [intra_product(0, w.pop(0))]
    w[1] = intra_weights(1)
    yield

    y_diag.append(intra_product(1, w.pop(1)))
    w[2] = intra_weights(2)
    finish(3, vals.pop(3), soft.pop(3))
    yield

    y_diag.append(intra_product(2, w.pop(2)))
    w[3] = intra_weights(3)
    yield

    y_diag.append(intra_product(3, w.pop(3)))
    entering = []
    for g in range(SSM_GROUPS):
        state = state_ref[g]
        per_chunk = []
        for j in range(nch):
            per_chunk.append(state.astype(BF16))
            state = state * chunk_decay[j][:, g * GROUP_WIDTH:(g + 1) * GROUP_WIDTH] + d_state[j][g]
        state_ref[g] = state
        entering.append(per_chunk)
    yield

    y_rows = [jnp.concatenate([y_diag[j][g] + _dot(cm[j][g], entering[g][j]) * grp(decay_in_x, j, g)
                               for g in range(SSM_GROUPS)], axis=-1) for j in range(nch)]
    y = jnp.concatenate(y_rows, axis=0) + dskip_ref[...] * xs
    y = y * gz_ref[...].astype(F32)
    y = jnp.concatenate([_rms(y[:, :GROUP_WIDTH]), _rms(y[:, GROUP_WIDTH:])], axis=-1)
    o_ref[:, ATTN_WIDTH:] = (y * sn_ref[...]).astype(BF16)


FFN_PHASES = D_FF // FFN_FC + D_MODEL // FFN_FC


def _ffn_steps(mix_ref, x_ref, mod_ref, wo_ref, n2_ref, wgu_ref, wd_ref, fn_ref, o_ref, h_ref, a_ref):
    gain2 = n2_ref[...] * (1.0 + mod_ref[4:5, :])
    x1 = x_ref[...] + mod_ref[2:3, :] * _dot(mix_ref[...], wo_ref[...])
    h_ref[...] = (_rms(x1) * gain2 + mod_ref[3:4, :]).astype(BF16)
    o_ref[...] = x1
    yield
    for c0 in range(0, D_FF, FFN_FC):
        hb = h_ref[...]
        g = _dot(hb, wgu_ref[:, c0:c0 + FFN_FC])
        u = _dot(hb, wgu_ref[:, D_FF + c0:D_FF + c0 + FFN_FC])
        a_ref[:, c0:c0 + FFN_FC] = (_silu(g) * u).astype(BF16)
        yield
    for c0 in range(0, D_MODEL, FFN_FC):
        cols = slice(c0, c0 + FFN_FC)
        o_ref[:, cols] = o_ref[:, cols] + mod_ref[5:6, cols] * _dot(a_ref[...], wd_ref[:, cols])
        if c0 + FFN_FC == D_MODEL:
            o_ref[...] = _rms(o_ref[...]) * fn_ref[...]
        yield


def _mixer_ffn_kernel(q_ref, kt_ref, ktp_ref, v_ref, vp_ref, gz_ref, act_ref, dt_ref, bias_ref, sinks_ref, dtb_ref,
                      alog_ref, dskip_ref, an_ref, sn_ref, expand_ref, x_ref, mod_ref, wo_ref, n2_ref, wgu_ref,
                      wd_ref, fn_ref, o_ref, state_ref, ka_ref, vv_ref, xbd_ref, mix_ref, h_ref, a_ref):
    s = pl.program_id(1)
    last = pl.num_programs(1) - 1
    mixer = lambda first: _mixer_steps(first, q_ref, kt_ref, ktp_ref, v_ref, vp_ref, gz_ref, act_ref, dt_ref,
                                       bias_ref, sinks_ref, dtb_ref, alog_ref, dskip_ref, an_ref, sn_ref,
                                       expand_ref, mix_ref, state_ref, ka_ref, vv_ref, xbd_ref)
    ffn = lambda: _ffn_steps(mix_ref, x_ref, mod_ref, wo_ref, n2_ref, wgu_ref, wd_ref, fn_ref, o_ref, h_ref, a_ref)

    @pl.when(s == 0)
    def _():
        state_ref[...] = jnp.zeros_like(state_ref)
        vv_ref[:, :, LANES:] = jnp.ones((N_KV_HEADS, BLOCK + MIX_ROWS, LANES), BF16)
        for _ in mixer(True):
            pass

    @pl.when((s > 0) & (s < last))
    def _():
        ffn_steps, issued = ffn(), 0
        next(ffn_steps)
        for k, _ in enumerate(mixer(False)):
            while issued < ((k + 1) * (FFN_PHASES - 1)) // MIXER_YIELDS:
                next(ffn_steps)
                issued += 1
        assert k == MIXER_YIELDS - 1
        for _ in ffn_steps:
            pass

    @pl.when(s == last)
    def _():
        for _ in ffn():
            pass


def _mixer_ffn_call(q, kt, v, gz, act, dt, bias, sinks, dt_bias, a_log, d_skip_x, attn_norm, ssm_norm, expand,
                    x, mod, w_o, norm2, w_gu, w_down, final_norm):
    b, s, _ = q.shape
    rows = MIX_ROWS
    n_tiles = s // rows
    mix_tile = lambda i, c: jnp.minimum(c, n_tiles - 1)
    ffn_tile = lambda i, c: jnp.maximum(c - 1, 0)
    prev_blk = lambda i, c: jnp.maximum(mix_tile(i, c) * MIX_CHUNKS - 1, 0)
    mrow = lambda width: pl.BlockSpec((None, rows, width), lambda i, c: (i, mix_tile(i, c), 0))
    frow = lambda: pl.BlockSpec((None, rows, D_MODEL), lambda i, c: (i, ffn_tile(i, c), 0))
    const = lambda shape: pl.BlockSpec(shape, lambda i, c: (0,) * len(shape))
    resident = lambda shape: pl.BlockSpec(shape, lambda i, c: (0,) * len(shape), pipeline_mode=pl.Buffered(1))
    return pl.pallas_call(
        _mixer_ffn_kernel,
        grid=(b, n_tiles + 1),
        in_specs=[
            mrow(ATTN_WIDTH),
            pl.BlockSpec((None, KV_WIDTH, rows), lambda i, c: (i, 0, mix_tile(i, c))),
            pl.BlockSpec((None, KV_WIDTH, BLOCK), lambda i, c: (i, 0, prev_blk(i, c))),
            mrow(KV_WIDTH),
            pl.BlockSpec((None, BLOCK, KV_WIDTH), lambda i, c: (i, prev_blk(i, c), 0)),
            mrow(SSM_WIDTH), mrow(XBC_WIDTH), mrow(LANES),
            const((2, N_Q_HEADS, BLOCK, BLOCK)),
            pl.BlockSpec(memory_space=pltpu.SMEM),
            const((1, LANES)), const((1, LANES)),
            const((1, SSM_WIDTH)), const((1, ATTN_WIDTH)), const((1, SSM_WIDTH)),
            const((LANES, SSM_WIDTH)),
            frow(),
            pl.BlockSpec((None, 6, D_MODEL), lambda i, c: (i, 0, 0)),
            resident((D_MODEL, D_MODEL)), const((1, D_MODEL)),
            resident((D_MODEL, 2 * D_FF)), resident((D_FF, D_MODEL)), const((1, D_MODEL)),
        ],
        out_specs=frow(),
        out_shape=jax.ShapeDtypeStruct((b, s, D_MODEL), F32),
        scratch_shapes=[
            pltpu.VMEM((SSM_GROUPS, D_STATE, GROUP_WIDTH), F32),
            pltpu.VMEM((N_KV_HEADS, 2, KV_WIDTH, BLOCK + rows), BF16),
            pltpu.VMEM((N_KV_HEADS, BLOCK + rows, 2 * LANES), BF16),
            pltpu.VMEM((MIX_CHUNKS, SSM_GROUPS, HEADS_PER_GROUP * CHUNK, GROUP_WIDTH), BF16),
            pltpu.VMEM((rows, ATTN_WIDTH + SSM_WIDTH), BF16),
            pltpu.VMEM((rows, D_MODEL), BF16),
            pltpu.VMEM((rows, D_FF), BF16),
        ],
        compiler_params=pltpu.CompilerParams(
            dimension_semantics=("parallel", "arbitrary"), vmem_limit_bytes=VMEM_LIMIT_BYTES),
        name="mixer_ffn",
    )(q, kt, kt, v, v, gz, act, dt, bias, sinks, dt_bias, a_log, d_skip_x, attn_norm, ssm_norm, expand,
      x, mod, w_o, norm2, w_gu, w_down, final_norm)


def _head_expand_matrix():
    e = np.zeros((LANES, SSM_WIDTH), np.float32)
    for h in range(SSM_HEADS):
        e[h, h * SSM_HEAD_DIM:(h + 1) * SSM_HEAD_DIM] = 1.0
    return e


def kernel(x, c, ada_w, ada_b, norm1, w_in, conv_w, conv_b, dt_bias, A_log, D_skip, sinks, attn_out_norm,
           ssm_out_norm, w_o, norm2, w_gate_up, w_down, rel_bias, final_norm):
    assert ada_w.shape[0] == 1, "the final RMSNorm is fused into the (single) layer's FFN call"
    b = x.shape[0]
    bias = _bias_call(rel_bias)
    expand = jnp.asarray(_head_expand_matrix(), BF16)
    pad_heads = lambda v: jnp.pad(v, (0, LANES - SSM_HEADS))[None, :]
    c_pad = jnp.pad(c, ((0, SUBLANES - b), (0, 0)))
    mod = _ada_call(c_pad, ada_w[0], ada_b[0][None, :])[:b].reshape(b, 6, D_MODEL)
    w_main = w_in.astype(BF16)[0]
    w_dt = jnp.pad(w_main[:, XBC_END:], ((0, 0), (0, LANES - SSM_HEADS)))
    q, kt, v, gz, act, dt = _inproj_call(x, mod, norm1[0][None, :], w_main, w_dt, conv_w[0], conv_b[0][None, :])
    return _mixer_ffn_call(
        q, kt, v, gz, act, dt, bias, sinks[0], pad_heads(dt_bias[0]), pad_heads(A_log[0]),
        jnp.repeat(D_skip[0], SSM_HEAD_DIM)[None, :], attn_out_norm[0][None, :], ssm_out_norm[0][None, :], expand,
        x, mod, w_o[0].astype(BF16), norm2[0][None, :], w_gate_up[0].astype(BF16), w_down[0].astype(BF16),
        final_norm[None, :])
```

```python
import numpy as np
import jax
import jax.numpy as jnp
from jax import lax
from jax.experimental import pallas as pl
from jax.experimental.pallas import tpu as pltpu

F32 = jnp.float32
BF16 = jnp.bfloat16

D_MODEL = 1024
HEAD_DIM = 64
N_Q_HEADS = 8
N_KV_HEADS = 2
Q_PER_KV = N_Q_HEADS // N_KV_HEADS
ATTN_WIDTH = N_Q_HEADS * HEAD_DIM
KV_WIDTH = N_KV_HEADS * HEAD_DIM
WINDOW = 128
BLOCK = 128
N_BUCKETS = 32
MAX_DISTANCE = 128

SSM_HEAD_DIM = 64
SSM_HEADS = 8
SSM_GROUPS = 2
HEADS_PER_GROUP = SSM_HEADS // SSM_GROUPS
SSM_WIDTH = SSM_HEADS * SSM_HEAD_DIM
GROUP_WIDTH = SSM_WIDTH // SSM_GROUPS
D_STATE = 128
CONV_K = 4
CHUNK = 128
XBC_WIDTH = SSM_WIDTH + 2 * SSM_GROUPS * D_STATE
D_FF = 2816
EPS = 1e-6
LOG2E = float(np.log2(np.e))

LANES = 128
SUBLANES = 8
VMEM_LIMIT_BYTES = 56 * 1024 * 1024

ADA_TN = 1536
INPROJ_TM = 1024
FFN_FC = 256
MIX_CHUNKS = 4
MIXER_YIELDS = 10
MIX_ROWS = MIX_CHUNKS * CHUNK

Q_END = ATTN_WIDTH
K_END = Q_END + KV_WIDTH
V_END = K_END + KV_WIDTH
Z_END = V_END + SSM_WIDTH
XBC_END = Z_END + XBC_WIDTH

assert BLOCK == CHUNK == LANES and KV_WIDTH == LANES and 2 * HEAD_DIM == LANES


def _silu(x):
    h = 0.5 * x
    return h * jnp.tanh(h) + h


def _rms(x):
    return x * lax.rsqrt(jnp.mean(x * x, axis=-1, keepdims=True) + EPS)


def _split2(x):
    hi = x.astype(BF16)
    lo = (x - hi.astype(F32)).astype(BF16)
    return hi, lo


def _dot(a, b):
    return jnp.dot(a, b, preferred_element_type=F32)


def _dot_nt(a, b):
    return lax.dot_general(a, b, (((1,), (1,)), ((), ())), preferred_element_type=F32)


def _dot_tn(a, b):
    return lax.dot_general(a, b, (((0,), (0,)), ((), ())), preferred_element_type=F32)


def _ada_kernel(c_ref, w_ref, b_ref, o_ref):
    cond = _silu(c_ref[...])
    o_ref[...] = _dot(cond.astype(BF16), w_ref[...].astype(BF16)) + b_ref[...]


def _ada_call(c_pad, ada_w, ada_b):
    n = ada_w.shape[1]
    return pl.pallas_call(
        _ada_kernel,
        grid=(n // ADA_TN,),
        in_specs=[
            pl.BlockSpec((SUBLANES, D_MODEL), lambda j: (0, 0)),
            pl.BlockSpec((D_MODEL, ADA_TN), lambda j: (0, j)),
            pl.BlockSpec((1, ADA_TN), lambda j: (0, j)),
        ],
        out_specs=pl.BlockSpec((SUBLANES, ADA_TN), lambda j: (0, j)),
        out_shape=jax.ShapeDtypeStruct((SUBLANES, n), F32),
        compiler_params=pltpu.CompilerParams(dimension_semantics=("parallel",)),
        name="ada_mod",
    )(c_pad, ada_w, ada_b)


def _t5_bucket_table():
    i = np.arange(BLOCK)[:, None]
    j = np.arange(BLOCK)[None, :]
    n = np.where(j <= i, i - j, i + BLOCK - j)
    assert WINDOW == BLOCK and n.min() >= 0 and n.max() < WINDOW
    max_exact = N_BUCKETS // 2
    large = max_exact + (np.log(np.maximum(n, 1) / max_exact) / np.log(MAX_DISTANCE / max_exact)
                         * (N_BUCKETS - max_exact)).astype(np.int32)
    large = np.minimum(large, N_BUCKETS - 1)
    bucket = np.where(n < max_exact, n, large).astype(np.int32)
    first = np.where(j <= i, bucket, -1)
    return np.stack([first, bucket]).astype(np.int32)


def _bias_kernel(idx_ref, rb_ref, o_ref):
    for v in range(2):
        idx = idx_ref[v]
        hits = [idx == b for b in range(N_BUCKETS)]
        for h in range(N_Q_HEADS):
            acc = jnp.full((BLOCK, BLOCK), -jnp.inf, F32)
            for b in range(N_BUCKETS):
                acc = jnp.where(hits[b], rb_ref[b, h] * LOG2E, acc)
            o_ref[v, h] = acc


def _bias_call(rel_bias):
    idx = jnp.asarray(_t5_bucket_table())
    return pl.pallas_call(
        _bias_kernel,
        in_specs=[
            pl.BlockSpec(memory_space=pltpu.VMEM),
            pl.BlockSpec(memory_space=pltpu.SMEM),
        ],
        out_specs=pl.BlockSpec(memory_space=pltpu.VMEM),
        out_shape=jax.ShapeDtypeStruct((2, N_Q_HEADS, BLOCK, BLOCK), F32),
        name="t5_bias",
    )(idx, rel_bias)


def _inproj_norm(x_ref, mod_ref, n1_ref, hb_ref):
    gain = n1_ref[...] * (1.0 + mod_ref[1:2, :])
    hb_ref[...] = (_rms(x_ref[...]) * gain + mod_ref[0:1, :]).astype(BF16)


def _inproj_project(hb_ref, w_ref, wdt_ref, cw_ref, cb_ref, q_ref, kt_ref, v_ref, gz_ref, act_ref, dt_ref, xe_ref):
    tm = hb_ref.shape[0]
    proj = lambda lo, hi: _dot(hb_ref[...], w_ref[:, lo:hi])
    tile = 2 * LANES
    for i, c0 in enumerate(range(0, XBC_WIDTH, tile)):
        cols = slice(c0, c0 + tile)
        xbc = proj(Z_END + c0, Z_END + c0 + tile)
        xe = jnp.concatenate([xe_ref[:, cols], xbc], axis=0)
        xe = xe.reshape(tm // SUBLANES + 1, SUBLANES, tile)
        first_row = lax.broadcasted_iota(jnp.int32, xe.shape, 1) == 0

        def shift_down_one(a):
            rot = pltpu.roll(a, 1, axis=1)
            return jnp.where(first_row, jnp.concatenate([rot[:1], rot[:-1]], axis=0), rot)

        taps = cw_ref[0:1, cols] * xe
        for k in range(1, CONV_K):
            taps = shift_down_one(taps) + cw_ref[k:k + 1, cols] * xe
        conv = taps[1:].reshape(tm, tile) + cb_ref[:, cols]
        xe_ref[:, cols] = xbc[tm - SUBLANES:, :]
        act_ref[:, cols] = _silu(conv).astype(BF16)
        if i < Q_END // tile:
            q_ref[:, i * tile:(i + 1) * tile] = proj(i * tile, (i + 1) * tile).astype(BF16)
        else:
            z0 = (i - Q_END // tile) * tile
            gz_ref[:, z0:z0 + tile] = _silu(proj(V_END + z0, V_END + z0 + tile)).astype(BF16)
    kt_ref[...] = (proj(Q_END, K_END) * (HEAD_DIM ** -0.5 * LOG2E)).T.astype(BF16)
    v_ref[...] = proj(K_END, V_END).astype(BF16)
    dt_ref[...] = _dot(hb_ref[...], wdt_ref[...])


def _inproj_kernel(x_ref, mod_ref, n1_ref, w_ref, wdt_ref, cw_ref, cb_ref, q_ref, kt_ref, v_ref, gz_ref, act_ref,
                   dt_ref, hb_ref, xe_ref):
    @pl.when(pl.program_id(1) == 0)
    def _():
        xe_ref[...] = jnp.zeros_like(xe_ref)

    _inproj_norm(x_ref, mod_ref, n1_ref, hb_ref)
    _inproj_project(hb_ref, w_ref, wdt_ref, cw_ref, cb_ref, q_ref, kt_ref, v_ref, gz_ref, act_ref, dt_ref, xe_ref)


def _inproj_call(x, mod, norm1, w_main, w_dt, conv_w, conv_b):
    b, s, _ = x.shape
    tm = INPROJ_TM
    row = lambda width: pl.BlockSpec((None, tm, width), lambda i, j: (i, j, 0))
    const = lambda shape: pl.BlockSpec(shape, lambda i, j: (0,) * len(shape))
    return pl.pallas_call(
        _inproj_kernel,
        grid=(b, s // tm),
        in_specs=[
            row(D_MODEL),
            pl.BlockSpec((None, 6, D_MODEL), lambda i, j: (i, 0, 0)),
            const((1, D_MODEL)),
            const((D_MODEL, XBC_END + SSM_HEADS)),
            const((D_MODEL, LANES)),
            const((CONV_K, XBC_WIDTH)), const((1, XBC_WIDTH)),
        ],
        out_specs=[row(ATTN_WIDTH), pl.BlockSpec((None, KV_WIDTH, tm), lambda i, j: (i, 0, j)), row(KV_WIDTH),
                   row(SSM_WIDTH), row(XBC_WIDTH), row(LANES)],
        out_shape=[
            jax.ShapeDtypeStruct((b, s, ATTN_WIDTH), BF16),
            jax.ShapeDtypeStruct((b, KV_WIDTH, s), BF16),
            jax.ShapeDtypeStruct((b, s, KV_WIDTH), BF16),
            jax.ShapeDtypeStruct((b, s, SSM_WIDTH), BF16),
            jax.ShapeDtypeStruct((b, s, XBC_WIDTH), BF16),
            jax.ShapeDtypeStruct((b, s, LANES), F32),
        ],
        scratch_shapes=[
            pltpu.VMEM((tm, D_MODEL), BF16),
            pltpu.VMEM((SUBLANES, XBC_WIDTH), F32),
        ],
        compiler_params=pltpu.CompilerParams(
            dimension_semantics=("parallel", "arbitrary"), vmem_limit_bytes=VMEM_LIMIT_BYTES),
        name="in_proj",
    )(x, mod, norm1, w_main, w_dt, conv_w, conv_b)


def _stage_keys_values(kt_ref, ktp_ref, v_ref, vp_ref, ka_ref, vv_ref):
    zeros = jnp.zeros((HEAD_DIM, BLOCK + MIX_ROWS), BF16)
    for g in range(N_KV_HEADS):
        rows = slice(g * HEAD_DIM, (g + 1) * HEAD_DIM)
        for par in range(2):
            live = slice(par * HEAD_DIM, (par + 1) * HEAD_DIM)
            dead = slice((1 - par) * HEAD_DIM, (2 - par) * HEAD_DIM)
            ka_ref[g, par, live, 0:BLOCK] = ktp_ref[rows, :]
            ka_ref[g, par, live, BLOCK:] = kt_ref[rows, :]
            ka_ref[g, par, dead, :] = zeros

    def duplicate_halves(v):
        v32 = pltpu.bitcast(v, jnp.uint32)
        swapped = pltpu.roll(v32, HEAD_DIM, axis=1)
        lane = lax.broadcasted_iota(jnp.int32, v32.shape, 1)
        first = lane < HEAD_DIM
        return (pltpu.bitcast(jnp.where(first, v32, swapped), BF16),
                pltpu.bitcast(jnp.where(first, swapped, v32), BF16))

    for src, r0, r1 in ((vp_ref, 0, BLOCK), (v_ref, BLOCK, BLOCK + MIX_ROWS)):
        for g, dup in enumerate(duplicate_halves(src[...])):
            vv_ref[g, r0:r1, 0:LANES] = dup


def _scores(q_ref, ka_ref, j):
    rows = pl.ds(j * BLOCK, BLOCK)
    win = pl.ds(j * BLOCK, 2 * BLOCK)
    out = []
    for h in range(N_Q_HEADS):
        t, par = h // 2, h % 2
        out.append(_dot(q_ref[rows, t * LANES:(t + 1) * LANES], ka_ref[t // 2, par, :, win]))
    return out


def _softmax(scores, bias_ref, sinks_ref, own):
    s = [jnp.where(own, sc[:, BLOCK:], sc[:, :BLOCK]) + bias_ref[h] for h, sc in enumerate(scores)]
    sink = [sinks_ref[h] * LOG2E for h in range(N_Q_HEADS)]
    m = [jnp.maximum(jnp.max(s[h], axis=-1, keepdims=True), sink[h]) for h in range(N_Q_HEADS)]
    p = [jnp.exp2(s[h] - m[h]) for h in range(N_Q_HEADS)]
    zero = jnp.zeros((BLOCK, BLOCK), BF16)
    pb = [ph.astype(BF16) for ph in p]
    pcat = [jnp.concatenate([jnp.where(own, zero, b), jnp.where(own, b, zero)], axis=1) for b in pb]
    sink_term = [jnp.exp2(sink[h] - m[h]) for h in range(N_Q_HEADS)]
    return pcat, sink_term


def _values(pcat, vv_ref, j):
    win = pl.ds(j * BLOCK, 2 * BLOCK)
    return [_dot(pcat[h], vv_ref[h // Q_PER_KV, win, :]) for h in range(N_Q_HEADS)]


def _finish_attention(o2, sink_term, an_ref, o_ref, j):
    rows = pl.ds(j * BLOCK, BLOCK)
    heads = [o2[h][:, :LANES] / (o2[h][:, LANES:] + sink_term[h]) for h in range(N_Q_HEADS)]
    lane = lax.broadcasted_iota(jnp.int32, (BLOCK, LANES), 1)
    tiles = [jnp.where(lane < HEAD_DIM, heads[2 * t], heads[2 * t + 1]) for t in range(N_Q_HEADS // 2)]
    ssq = sum(jnp.sum(tl * tl, axis=-1, keepdims=True) for tl in tiles)
    inv = lax.rsqrt(ssq * (1.0 / ATTN_WIDTH) + EPS)
    for t, tl in enumerate(tiles):
        cols = slice(t * LANES, (t + 1) * LANES)
        o_ref[rows, cols] = (tl * inv * an_ref[:, cols]).astype(BF16)


def _mixer_steps(first_tile, q_ref, kt_ref, ktp_ref, v_ref, vp_ref, gz_ref, act_ref, dt_ref, bias_ref, sinks_ref,
                 dtb_ref, alog_ref, dskip_ref, an_ref, sn_ref, expand_ref, o_ref, state_ref, ka_ref, vv_ref,
                 xbd_ref):
    nch = MIX_CHUNKS
    row_i = lax.broadcasted_iota(jnp.int32, (BLOCK, BLOCK), 0)
    col_i = lax.broadcasted_iota(jnp.int32, (BLOCK, BLOCK), 1)
    own = col_i <= row_i
    bias_of = lambda j: bias_ref.at[0 if (first_tile and j == 0) else 1]
    grp = lambda a, j, g: a[j * CHUNK:(j + 1) * CHUNK, g * GROUP_WIDTH:(g + 1) * GROUP_WIDTH]
    softmax = lambda j, sc: _softmax(sc, bias_of(j), sinks_ref, own)
    finish = lambda j, vals, soft: _finish_attention(vals, soft[1], an_ref, o_ref, j)

    _stage_keys_values(kt_ref, ktp_ref, v_ref, vp_ref, ka_ref, vv_ref)
    dt = jax.nn.softplus(dt_ref[...] + dtb_ref[...])
    dta = dt * (-LOG2E * jnp.exp(alog_ref[...]))
    tri = own.astype(BF16)
    dta_hi, dta_lo = _split2(dta)
    scores = {0: _scores(q_ref, ka_ref, 0), 1: _scores(q_ref, ka_ref, 1)}
    acs = [_dot(tri, dta_hi[j * CHUNK:(j + 1) * CHUNK]) + _dot(tri, dta_lo[j * CHUNK:(j + 1) * CHUNK])
           for j in range(nch)]
    yield

    soft = {0: softmax(0, scores.pop(0))}
    acs_dt_t = [(a - jnp.log2(dt[j * CHUNK:(j + 1) * CHUNK])).T for j, a in enumerate(acs)]
    dec_in = [jnp.exp2(a) for a in acs]
    dec_out = jnp.concatenate([jnp.exp2(a[CHUNK - 1:CHUNK, :] - a) for a in acs], axis=0)
    state_in = (dt * dec_out).astype(BF16)
    decay_in = jnp.concatenate(dec_in, axis=0).astype(BF16)
    decay_tail = [_split2(d[CHUNK - SUBLANES:, :]) for d in dec_in]
    yield

    vals = {0: _values(soft[0][0], vv_ref, 0)}
    expand = expand_ref[...]
    state_in_x = _dot(state_in, expand)
    decay_in_x = _dot(decay_in, expand)
    chunk_decay = [(_dot(hi, expand) + _dot(lo, expand))[SUBLANES - 1:SUBLANES, :] for hi, lo in decay_tail]
    scores[2] = _scores(q_ref, ka_ref, 2)
    soft[1] = softmax(1, scores.pop(1))
    finish(0, vals.pop(0), soft.pop(0))
    yield

    vals[1] = _values(soft[1][0], vv_ref, 1)
    scores[3] = _scores(q_ref, ka_ref, 3)
    xs = act_ref[:, :SSM_WIDTH].astype(F32)
    xdec_b = (xs * state_in_x).astype(BF16)
    lane_head = lax.broadcasted_iota(jnp.int32, (CHUNK, GROUP_WIDTH), 1) // SSM_HEAD_DIM
    for j in range(nch):
        for g in range(SSM_GROUPS):
            xg = act_ref[j * CHUNK:(j + 1) * CHUNK, g * GROUP_WIDTH:(g + 1) * GROUP_WIDTH]
            for r in range(HEADS_PER_GROUP):
                xbd_ref[j, g, r * CHUNK:(r + 1) * CHUNK, :] = jnp.where(lane_head == r, xg, jnp.zeros_like(xg))
    soft[2] = softmax(2, scores.pop(2))
    finish(1, vals.pop(1), soft.pop(1))
    yield

    bc_b = act_ref[:, SSM_WIDTH:]
    bm = [[bc_b[j * CHUNK:(j + 1) * CHUNK, g * D_STATE:(g + 1) * D_STATE] for g in range(SSM_GROUPS)]
          for j in range(nch)]
    cm = [[bc_b[j * CHUNK:(j + 1) * CHUNK, (SSM_GROUPS + g) * D_STATE:(SSM_GROUPS + g + 1) * D_STATE]
           for g in range(SSM_GROUPS)] for j in range(nch)]
    cbm = [[_dot_nt(cm[j][g], bm[j][g]) for g in range(SSM_GROUPS)] for j in range(nch)]
    d_state = [[_dot_tn(bm[j][g], grp(xdec_b, j, g)) for g in range(SSM_GROUPS)] for j in range(nch)]
    vals[2] = _values(soft[2][0], vv_ref, 2)
    soft[3] = softmax(3, scores.pop(3))

    def intra_weights(j):
        per_group = []
        for g in range(SSM_GROUPS):
            w = []
            for r in range(HEADS_PER_GROUP):
                h = g * HEADS_PER_GROUP + r
                seg = acs[j][:, h:h + 1] - acs_dt_t[j][h:h + 1, :]
                w.append(cbm[j][g] * jnp.exp2(jnp.where(own, seg, -jnp.inf)))
            per_group.append((jnp.concatenate(w[:2], axis=1).astype(BF16),
                              jnp.concatenate(w[2:], axis=1).astype(BF16)))
        return per_group

    def intra_product(j, w):
        return [_dot(w[g][0], xbd_ref[j, g, 0:2 * CHUNK, :]) + _dot(w[g][1], xbd_ref[j, g, 2 * CHUNK:, :])
                for g in range(SSM_GROUPS)]

    yield
    w = {0: intra_weights(0)}
    finish(2, vals.pop(2), soft.pop(2))
    yield

    vals[3] = _values(soft[3][0], vv_ref, 3)
    y_diag = [intra_product(0, w.pop(0))]
    w[1] = intra_weights(1)
    yield

    y_diag.append(intra_product(1, w.pop(1)))
    w[2] = intra_weights(2)
    finish(3, vals.pop(3), soft.pop(3))
    yield

    y_diag.append(intra_product(2, w.pop(2)))
    w[3] = intra_weights(3)
    yield

    y_diag.append(intra_product(3, w.pop(3)))
    entering = []
    for g in range(SSM_GROUPS):
        state = state_ref[g]
        per_chunk = []
        for j in range(nch):
            per_chunk.append(state.astype(BF16))
            state = state * chunk_decay[j][:, g * GROUP_WIDTH:(g + 1) * GROUP_WIDTH] + d_state[j][g]
        state_ref[g] = state
        entering.append(per_chunk)
    yield

    y_rows = [jnp.concatenate([y_diag[j][g] + _dot(cm[j][g], entering[g][j]) * grp(decay_in_x, j, g)
                               for g in range(SSM_GROUPS)], axis=-1) for j in range(nch)]
    y = jnp.concatenate(y_rows, axis=0) + dskip_ref[...] * xs
    y = y * gz_ref[...].astype(F32)
    y = jnp.concatenate([_rms(y[:, :GROUP_WIDTH]), _rms(y[:, GROUP_WIDTH:])], axis=-1)
    o_ref[:, ATTN_WIDTH:] = (y * sn_ref[...]).astype(BF16)


FFN_PHASES = D_FF // FFN_FC + D_MODEL // FFN_FC


def _ffn_steps(mix_ref, x_ref, mod_ref, wo_ref, n2_ref, wgu_ref, wd_ref, fn_ref, o_ref, h_ref, a_ref):
    gain2 = n2_ref[...] * (1.0 + mod_ref[4:5, :])
    x1 = x_ref[...] + mod_ref[2:3, :] * _dot(mix_ref[...], wo_ref[...])
    h_ref[...] = (_rms(x1) * gain2 + mod_ref[3:4, :]).astype(BF16)
    o_ref[...] = x1
    yield
    for c0 in range(0, D_FF, FFN_FC):
        hb = h_ref[...]
        g = _dot(hb, wgu_ref[:, c0:c0 + FFN_FC])
        u = _dot(hb, wgu_ref[:, D_FF + c0:D_FF + c0 + FFN_FC])
        a_ref[:, c0:c0 + FFN_FC] = (_silu(g) * u).astype(BF16)
        yield
    for c0 in range(0, D_MODEL, FFN_FC):
        cols = slice(c0, c0 + FFN_FC)
        o_ref[:, cols] = o_ref[:, cols] + mod_ref[5:6, cols] * _dot(a_ref[...], wd_ref[:, cols])
        if c0 + FFN_FC == D_MODEL:
            o_ref[...] = _rms(o_ref[...]) * fn_ref[...]
        yield


def _mixer_ffn_kernel(q_ref, kt_ref, ktp_ref, v_ref, vp_ref, gz_ref, act_ref, dt_ref, bias_ref, sinks_ref, dtb_ref,
                      alog_ref, dskip_ref, an_ref, sn_ref, expand_ref, x_ref, mod_ref, wo_ref, n2_ref, wgu_ref,
                      wd_ref, fn_ref, o_ref, state_ref, ka_ref, vv_ref, xbd_ref, mix_ref, h_ref, a_ref):
    s = pl.program_id(1)
    last = pl.num_programs(1) - 1
    mixer = lambda first: _mixer_steps(first, q_ref, kt_ref, ktp_ref, v_ref, vp_ref, gz_ref, act_ref, dt_ref,
                                       bias_ref, sinks_ref, dtb_ref, alog_ref, dskip_ref, an_ref, sn_ref,
                                       expand_ref, mix_ref, state_ref, ka_ref, vv_ref, xbd_ref)
    ffn = lambda: _ffn_steps(mix_ref, x_ref, mod_ref, wo_ref, n2_ref, wgu_ref, wd_ref, fn_ref, o_ref, h_ref, a_ref)

    @pl.when(s == 0)
    def _():
        state_ref[...] = jnp.zeros_like(state_ref)
        vv_ref[:, :, LANES:] = jnp.ones((N_KV_HEADS, BLOCK + MIX_ROWS, LANES), BF16)
        for _ in mixer(True):
            pass

    @pl.when((s > 0) & (s < last))
    def _():
        ffn_steps, issued = ffn(), 0
        next(ffn_steps)
        for k, _ in enumerate(mixer(False)):
            while issued < ((k + 1) * (FFN_PHASES - 1)) // MIXER_YIELDS:
                next(ffn_steps)
                issued += 1
        assert k == MIXER_YIELDS - 1
        for _ in ffn_steps:
            pass

    @pl.when(s == last)
    def _():
        for _ in ffn():
            pass


def _mixer_ffn_call(q, kt, v, gz, act, dt, bias, sinks, dt_bias, a_log, d_skip_x, attn_norm, ssm_norm, expand,
                    x, mod, w_o, norm2, w_gu, w_down, final_norm):
    b, s, _ = q.shape
    rows = MIX_ROWS
    n_tiles = s // rows
    mix_tile = lambda i, c: jnp.minimum(c, n_tiles - 1)
    ffn_tile = lambda i, c: jnp.maximum(c - 1, 0)
    prev_blk = lambda i, c: jnp.maximum(mix_tile(i, c) * MIX_CHUNKS - 1, 0)
    mrow = lambda width: pl.BlockSpec((None, rows, width), lambda i, c: (i, mix_tile(i, c), 0))
    frow = lambda: pl.BlockSpec((None, rows, D_MODEL), lambda i, c: (i, ffn_tile(i, c), 0))
    const = lambda shape: pl.BlockSpec(shape, lambda i, c: (0,) * len(shape))
    resident = lambda shape: pl.BlockSpec(shape, lambda i, c: (0,) * len(shape), pipeline_mode=pl.Buffered(1))
    return pl.pallas_call(
        _mixer_ffn_kernel,
        grid=(b, n_tiles + 1),
        in_specs=[
            mrow(ATTN_WIDTH),
            pl.BlockSpec((None, KV_WIDTH, rows), lambda i, c: (i, 0, mix_tile(i, c))),
            pl.BlockSpec((None, KV_WIDTH, BLOCK), lambda i, c: (i, 0, prev_blk(i, c))),
            mrow(KV_WIDTH),
            pl.BlockSpec((None, BLOCK, KV_WIDTH), lambda i, c: (i, prev_blk(i, c), 0)),
            mrow(SSM_WIDTH), mrow(XBC_WIDTH), mrow(LANES),
            const((2, N_Q_HEADS, BLOCK, BLOCK)),
            pl.BlockSpec(memory_space=pltpu.SMEM),
            const((1, LANES)), const((1, LANES)),
            const((1, SSM_WIDTH)), const((1, ATTN_WIDTH)), const((1, SSM_WIDTH)),
            const((LANES, SSM_WIDTH)),
            frow(),
            pl.BlockSpec((None, 6, D_MODEL), lambda i, c: (i, 0, 0)),
            resident((D_MODEL, D_MODEL)), const((1, D_MODEL)),
            resident((D_MODEL, 2 * D_FF)), resident((D_FF, D_MODEL)), const((1, D_MODEL)),
        ],
        out_specs=frow(),
        out_shape=jax.ShapeDtypeStruct((b, s, D_MODEL), F32),
        scratch_shapes=[
            pltpu.VMEM((SSM_GROUPS, D_STATE, GROUP_WIDTH), F32),
            pltpu.VMEM((N_KV_HEADS, 2, KV_WIDTH, BLOCK + rows), BF16),
            pltpu.VMEM((N_KV_HEADS, BLOCK + rows, 2 * LANES), BF16),
            pltpu.VMEM((MIX_CHUNKS, SSM_GROUPS, HEADS_PER_GROUP * CHUNK, GROUP_WIDTH), BF16),
            pltpu.VMEM((rows, ATTN_WIDTH + SSM_WIDTH), BF16),
            pltpu.VMEM((rows, D_MODEL), BF16),
            pltpu.VMEM((rows, D_FF), BF16),
        ],
        compiler_params=pltpu.CompilerParams(
            dimension_semantics=("parallel", "arbitrary"), vmem_limit_bytes=VMEM_LIMIT_BYTES),
        name="mixer_ffn",
    )(q, kt, kt, v, v, gz, act, dt, bias, sinks, dt_bias, a_log, d_skip_x, attn_norm, ssm_norm, expand,
      x, mod, w_o, norm2, w_gu, w_down, final_norm)


def _head_expand_matrix():
    e = np.zeros((LANES, SSM_WIDTH), np.float32)
    for h in range(SSM_HEADS):
        e[h, h * SSM_HEAD_DIM:(h + 1) * SSM_HEAD_DIM] = 1.0
    return e


def kernel(x, c, ada_w, ada_b, norm1, w_in, conv_w, conv_b, dt_bias, A_log, D_skip, sinks, attn_out_norm,
           ssm_out_norm, w_o, norm2, w_gate_up, w_down, rel_bias, final_norm):
    assert ada_w.shape[0] == 1, "the final RMSNorm is fused into the (single) layer's FFN call"
    b = x.shape[0]
    bias = _bias_call(rel_bias)
    expand = jnp.asarray(_head_expand_matrix(), BF16)
    pad_heads = lambda v: jnp.pad(v, (0, LANES - SSM_HEADS))[None, :]
    c_pad = jnp.pad(c, ((0, SUBLANES - b), (0, 0)))
    mod = _ada_call(c_pad, ada_w[0], ada_b[0][None, :])[:b].reshape(b, 6, D_MODEL)
    w_main = w_in.astype(BF16)[0]
    w_dt = jnp.pad(w_main[:, XBC_END:], ((0, 0), (0, LANES - SSM_HEADS)))
    q, kt, v, gz, act, dt = _inproj_call(x, mod, norm1[0][None, :], w_main, w_dt, conv_w[0], conv_b[0][None, :])
    return _mixer_ffn_call(
        q, kt, v, gz, act, dt, bias, sinks[0], pad_heads(dt_bias[0]), pad_heads(A_log[0]),
        jnp.repeat(D_skip[0], SSM_HEAD_DIM)[None, :], attn_out_norm[0][None, :], ssm_out_norm[0][None, :], expand,
        x, mod, w_o[0].astype(BF16), norm2[0][None, :], w_gate_up[0].astype(BF16), w_down[0].astype(BF16),
        final_norm[None, :])
```

```python
import numpy as np
import jax
import jax.numpy as jnp
from jax import lax
from jax.experimental import pallas as pl
from jax.experimental.pallas import tpu as pltpu

F32 = jnp.float32
BF16 = jnp.bfloat16

D_MODEL = 1024
HEAD_DIM = 64
N_Q_HEADS = 8
N_KV_HEADS = 2
Q_PER_KV = N_Q_HEADS // N_KV_HEADS
ATTN_WIDTH = N_Q_HEADS * HEAD_DIM
KV_WIDTH = N_KV_HEADS * HEAD_DIM
WINDOW = 128
BLOCK = 128
N_BUCKETS = 32
MAX_DISTANCE = 128

SSM_HEAD_DIM = 64
SSM_HEADS = 8
SSM_GROUPS = 2
HEADS_PER_GROUP = SSM_HEADS // SSM_GROUPS
SSM_WIDTH = SSM_HEADS * SSM_HEAD_DIM
GROUP_WIDTH = SSM_WIDTH // SSM_GROUPS
D_STATE = 128
CONV_K = 4
CHUNK = 128
XBC_WIDTH = SSM_WIDTH + 2 * SSM_GROUPS * D_STATE
D_FF = 2816
EPS = 1e-6
LOG2E = float(np.log2(np.e))

LANES = 128
SUBLANES = 8
VMEM_LIMIT_BYTES = 56 * 1024 * 1024

ADA_TN = 1536
INPROJ_TM = 1024
FFN_FC = 256
MIX_CHUNKS = 4
MIXER_YIELDS = 10
MIX_ROWS = MIX_CHUNKS * CHUNK

Q_END = ATTN_WIDTH
K_END = Q_END + KV_WIDTH
V_END = K_END + KV_WIDTH
Z_END = V_END + SSM_WIDTH
XBC_END = Z_END + XBC_WIDTH

assert BLOCK == CHUNK == LANES and KV_WIDTH == LANES and 2 * HEAD_DIM == LANES


def _silu(x):
    h = 0.5 * x
    return h * jnp.tanh(h) + h


def _rms(x):
    return x * lax.rsqrt(jnp.mean(x * x, axis=-1, keepdims=True) + EPS)


def _split2(x):
    hi = x.astype(BF16)
    lo = (x - hi.astype(F32)).astype(BF16)
    return hi, lo


def _dot(a, b):
    return jnp.dot(a, b, preferred_element_type=F32)


def _dot_nt(a, b):
    return lax.dot_general(a, b, (((1,), (1,)), ((), ())), preferred_element_type=F32)


def _dot_tn(a, b):
    return lax.dot_general(a, b, (((0,), (0,)), ((), ())), preferred_element_type=F32)


def _ada_kernel(c_ref, w_ref, b_ref, o_ref):
    cond = _silu(c_ref[...])
    o_ref[...] = _dot(cond.astype(BF16), w_ref[...].astype(BF16)) + b_ref[...]


def _ada_call(c_pad, ada_w, ada_b):
    n = ada_w.shape[1]
    return pl.pallas_call(
        _ada_kernel,
        grid=(n // ADA_TN,),
        in_specs=[
            pl.BlockSpec((SUBLANES, D_MODEL), lambda j: (0, 0)),
            pl.BlockSpec((D_MODEL, ADA_TN), lambda j: (0, j)),
            pl.BlockSpec((1, ADA_TN), lambda j: (0, j)),
        ],
        out_specs=pl.BlockSpec((SUBLANES, ADA_TN), lambda j: (0, j)),
        out_shape=jax.ShapeDtypeStruct((SUBLANES, n), F32),
        compiler_params=pltpu.CompilerParams(dimension_semantics=("parallel",)),
        name="ada_mod",
    )(c_pad, ada_w, ada_b)


def _t5_bucket_table():
    i = np.arange(BLOCK)[:, None]
    j = np.arange(BLOCK)[None, :]
    n = np.where(j <= i, i - j, i + BLOCK - j)
    assert WINDOW == BLOCK and n.min() >= 0 and n.max() < WINDOW
    max_exact = N_BUCKETS // 2
    large = max_exact + (np.log(np.maximum(n, 1) / max_exact) / np.log(MAX_DISTANCE / max_exact)
                         * (N_BUCKETS - max_exact)).astype(np.int32)
    large = np.minimum(large, N_BUCKETS - 1)
    bucket = np.where(n < max_exact, n, large).astype(np.int32)
    first = np.where(j <= i, bucket, -1)
    return np.stack([first, bucket]).astype(np.int32)


def _bias_kernel(idx_ref, rb_ref, o_ref):
    for v in range(2):
        idx = idx_ref[v]
        hits = [idx == b for b in range(N_BUCKETS)]
        for h in range(N_Q_HEADS):
            acc = jnp.full((BLOCK, BLOCK), -jnp.inf, F32)
            for b in range(N_BUCKETS):
                acc = jnp.where(hits[b], rb_ref[b, h] * LOG2E, acc)
            o_ref[v, h] = acc


def _bias_call(rel_bias):
    idx = jnp.asarray(_t5_bucket_table())
    return pl.pallas_call(
        _bias_kernel,
        in_specs=[
            pl.BlockSpec(memory_space=pltpu.VMEM),
            pl.BlockSpec(memory_space=pltpu.SMEM),
        ],
        out_specs=pl.BlockSpec(memory_space=pltpu.VMEM),
        out_shape=jax.ShapeDtypeStruct((2, N_Q_HEADS, BLOCK, BLOCK), F32),
        name="t5_bias",
    )(idx, rel_bias)


def _inproj_norm(x_ref, mod_ref, n1_ref, hb_ref):
    gain = n1_ref[...] * (1.0 + mod_ref[1:2, :])
    hb_ref[...] = (_rms(x_ref[...]) * gain + mod_ref[0:1, :]).astype(BF16)


def _inproj_project(hb_ref, w_ref, wdt_ref, cw_ref, cb_ref, q_ref, kt_ref, v_ref, gz_ref, act_ref, dt_ref, xe_ref):
    tm = hb_ref.shape[0]
    proj = lambda lo, hi: _dot(hb_ref[...], w_ref[:, lo:hi])
    tile = 2 * LANES
    for i, c0 in enumerate(range(0, XBC_WIDTH, tile)):
        cols = slice(c0, c0 + tile)
        xbc = proj(Z_END + c0, Z_END + c0 + tile)
        xe = jnp.concatenate([xe_ref[:, cols], xbc], axis=0)
        xe = xe.reshape(tm // SUBLANES + 1, SUBLANES, tile)
        slab_row = lax.broadcasted_iota(jnp.int32, xe.shape, 1)

        def shift_down(a, n):
            rot = pltpu.roll(a, n, axis=1)
            return jnp.where(slab_row < n, jnp.concatenate([rot[:1], rot[:-1]], axis=0), rot)

        w0, w1, w2, w3 = (cw_ref[k:k + 1, cols] for k in range(CONV_K))
        x1 = shift_down(xe, 1)
        taps = shift_down(w0 * x1 + w1 * xe, 2) + (w2 * x1 + w3 * xe)
        conv = taps[1:].reshape(tm, tile) + cb_ref[:, cols]
        xe_ref[:, cols] = xbc[tm - SUBLANES:, :]
        act_ref[:, cols] = _silu(conv).astype(BF16)
        if i < Q_END // tile:
            q_ref[:, i * tile:(i + 1) * tile] = proj(i * tile, (i + 1) * tile).astype(BF16)
        else:
            z0 = (i - Q_END // tile) * tile
            gz_ref[:, z0:z0 + tile] = _silu(proj(V_END + z0, V_END + z0 + tile)).astype(BF16)
    kt_ref[...] = (proj(Q_END, K_END) * (HEAD_DIM ** -0.5 * LOG2E)).T.astype(BF16)
    v_ref[...] = proj(K_END, V_END).astype(BF16)
    dt_ref[...] = _dot(hb_ref[...], wdt_ref[...])


def _inproj_kernel(x_ref, mod_ref, n1_ref, w_ref, wdt_ref, cw_ref, cb_ref, q_ref, kt_ref, v_ref, gz_ref, act_ref,
                   dt_ref, hb_ref, xe_ref):
    @pl.when(pl.program_id(1) == 0)
    def _():
        xe_ref[...] = jnp.zeros_like(xe_ref)

    _inproj_norm(x_ref, mod_ref, n1_ref, hb_ref)
    _inproj_project(hb_ref, w_ref, wdt_ref, cw_ref, cb_ref, q_ref, kt_ref, v_ref, gz_ref, act_ref, dt_ref, xe_ref)


def _inproj_call(x, mod, norm1, w_main, w_dt, conv_w, conv_b):
    b, s, _ = x.shape
    tm = INPROJ_TM
    row = lambda width: pl.BlockSpec((None, tm, width), lambda i, j: (i, j, 0))
    const = lambda shape: pl.BlockSpec(shape, lambda i, j: (0,) * len(shape))
    return pl.pallas_call(
        _inproj_kernel,
        grid=(b, s // tm),
        in_specs=[
            row(D_MODEL),
            pl.BlockSpec((None, 6, D_MODEL), lambda i, j: (i, 0, 0)),
            const((1, D_MODEL)),
            const((D_MODEL, XBC_END + SSM_HEADS)),
            const((D_MODEL, LANES)),
            const((CONV_K, XBC_WIDTH)), const((1, XBC_WIDTH)),
        ],
        out_specs=[row(ATTN_WIDTH), pl.BlockSpec((None, KV_WIDTH, tm), lambda i, j: (i, 0, j)), row(KV_WIDTH),
                   row(SSM_WIDTH), row(XBC_WIDTH), row(LANES)],
        out_shape=[
            jax.ShapeDtypeStruct((b, s, ATTN_WIDTH), BF16),
            jax.ShapeDtypeStruct((b, KV_WIDTH, s), BF16),
            jax.ShapeDtypeStruct((b, s, KV_WIDTH), BF16),
            jax.ShapeDtypeStruct((b, s, SSM_WIDTH), BF16),
            jax.ShapeDtypeStruct((b, s, XBC_WIDTH), BF16),
            jax.ShapeDtypeStruct((b, s, LANES), F32),
        ],
        scratch_shapes=[
            pltpu.VMEM((tm, D_MODEL), BF16),
            pltpu.VMEM((SUBLANES, XBC_WIDTH), F32),
        ],
        compiler_params=pltpu.CompilerParams(
            dimension_semantics=("parallel", "arbitrary"), vmem_limit_bytes=VMEM_LIMIT_BYTES),
        name="in_proj",
    )(x, mod, norm1, w_main, w_dt, conv_w, conv_b)


def _stage_keys_values(kt_ref, ktp_ref, v_ref, vp_ref, ka_ref, vv_ref):
    zeros = jnp.zeros((HEAD_DIM, BLOCK + MIX_ROWS), BF16)
    for g in range(N_KV_HEADS):
        rows = slice(g * HEAD_DIM, (g + 1) * HEAD_DIM)
        for par in range(2):
            live = slice(par * HEAD_DIM, (par + 1) * HEAD_DIM)
            dead = slice((1 - par) * HEAD_DIM, (2 - par) * HEAD_DIM)
            ka_ref[g, par, live, 0:BLOCK] = ktp_ref[rows, :]
            ka_ref[g, par, live, BLOCK:] = kt_ref[rows, :]
            ka_ref[g, par, dead, :] = zeros

    def duplicate_halves(v):
        v32 = pltpu.bitcast(v, jnp.uint32)
        swapped = pltpu.roll(v32, HEAD_DIM, axis=1)
        lane = lax.broadcasted_iota(jnp.int32, v32.shape, 1)
        first = lane < HEAD_DIM
        return (pltpu.bitcast(jnp.where(first, v32, swapped), BF16),
                pltpu.bitcast(jnp.where(first, swapped, v32), BF16))

    for src, r0, r1 in ((vp_ref, 0, BLOCK), (v_ref, BLOCK, BLOCK + MIX_ROWS)):
        for g, dup in enumerate(duplicate_halves(src[...])):
            vv_ref[g, r0:r1, 0:LANES] = dup


def _scores(q_ref, ka_ref, j):
    rows = pl.ds(j * BLOCK, BLOCK)
    win = pl.ds(j * BLOCK, 2 * BLOCK)
    out = []
    for h in range(N_Q_HEADS):
        t, par = h // 2, h % 2
        out.append(_dot(q_ref[rows, t * LANES:(t + 1) * LANES], ka_ref[t // 2, par, :, win]))
    return out


def _softmax(scores, bias_ref, sinks_ref, own):
    s = [jnp.where(own, sc[:, BLOCK:], sc[:, :BLOCK]) + bias_ref[h] for h, sc in enumerate(scores)]
    sink = [sinks_ref[h] * LOG2E for h in range(N_Q_HEADS)]
    m = [jnp.maximum(jnp.max(s[h], axis=-1, keepdims=True), sink[h]) for h in range(N_Q_HEADS)]
    p = [jnp.exp2(s[h] - m[h]) for h in range(N_Q_HEADS)]
    zero = jnp.zeros((BLOCK, BLOCK), BF16)
    pb = [ph.astype(BF16) for ph in p]
    pcat = [jnp.concatenate([jnp.where(own, zero, b), jnp.where(own, b, zero)], axis=1) for b in pb]
    sink_term = [jnp.exp2(sink[h] - m[h]) for h in range(N_Q_HEADS)]
    return pcat, sink_term


def _values(pcat, vv_ref, j):
    win = pl.ds(j * BLOCK, 2 * BLOCK)
    return [_dot(pcat[h], vv_ref[h // Q_PER_KV, win, :]) for h in range(N_Q_HEADS)]


def _finish_attention(o2, sink_term, an_ref, o_ref, j):
    rows = pl.ds(j * BLOCK, BLOCK)
    heads = [o2[h][:, :LANES] / (o2[h][:, LANES:] + sink_term[h]) for h in range(N_Q_HEADS)]
    lane = lax.broadcasted_iota(jnp.int32, (BLOCK, LANES), 1)
    tiles = [jnp.where(lane < HEAD_DIM, heads[2 * t], heads[2 * t + 1]) for t in range(N_Q_HEADS // 2)]
    ssq = sum(jnp.sum(tl * tl, axis=-1, keepdims=True) for tl in tiles)
    inv = lax.rsqrt(ssq * (1.0 / ATTN_WIDTH) + EPS)
    for t, tl in enumerate(tiles):
        cols = slice(t * LANES, (t + 1) * LANES)
        o_ref[rows, cols] = (tl * inv * an_ref[:, cols]).astype(BF16)


def _mixer_steps(first_tile, q_ref, kt_ref, ktp_ref, v_ref, vp_ref, gz_ref, act_ref, dt_ref, bias_ref, sinks_ref,
                 dtb_ref, alog_ref, dskip_ref, an_ref, sn_ref, expand_ref, o_ref, state_ref, ka_ref, vv_ref,
                 xbd_ref):
    nch = MIX_CHUNKS
    row_i = lax.broadcasted_iota(jnp.int32, (BLOCK, BLOCK), 0)
    col_i = lax.broadcasted_iota(jnp.int32, (BLOCK, BLOCK), 1)
    own = col_i <= row_i
    bias_of = lambda j: bias_ref.at[0 if (first_tile and j == 0) else 1]
    grp = lambda a, j, g: a[j * CHUNK:(j + 1) * CHUNK, g * GROUP_WIDTH:(g + 1) * GROUP_WIDTH]
    softmax = lambda j, sc: _softmax(sc, bias_of(j), sinks_ref, own)
    finish = lambda j, vals, soft: _finish_attention(vals, soft[1], an_ref, o_ref, j)

    _stage_keys_values(kt_ref, ktp_ref, v_ref, vp_ref, ka_ref, vv_ref)
    dt = jax.nn.softplus(dt_ref[...] + dtb_ref[...])
    dta = dt * (-LOG2E * jnp.exp(alog_ref[...]))
    tri = own.astype(BF16)
    dta_hi, dta_lo = _split2(dta)
    scores = {0: _scores(q_ref, ka_ref, 0), 1: _scores(q_ref, ka_ref, 1)}
    acs = [_dot(tri, dta_hi[j * CHUNK:(j + 1) * CHUNK]) + _dot(tri, dta_lo[j * CHUNK:(j + 1) * CHUNK])
           for j in range(nch)]
    yield

    soft = {0: softmax(0, scores.pop(0))}
    acs_dt_t = [(a - jnp.log2(dt[j * CHUNK:(j + 1) * CHUNK])).T for j, a in enumerate(acs)]
    dec_in = [jnp.exp2(a) for a in acs]
    dec_out = jnp.concatenate([jnp.exp2(a[CHUNK - 1:CHUNK, :] - a) for a in acs], axis=0)
    state_in = (dt * dec_out).astype(BF16)
    decay_in = jnp.concatenate(dec_in, axis=0).astype(BF16)
    decay_tail = [_split2(d[CHUNK - SUBLANES:, :]) for d in dec_in]
    yield

    vals = {0: _values(soft[0][0], vv_ref, 0)}
    expand = expand_ref[...]
    state_in_x = _dot(state_in, expand)
    decay_in_x = _dot(decay_in, expand)
    chunk_decay = [(_dot(hi, expand) + _dot(lo, expand))[SUBLANES - 1:SUBLANES, :] for hi, lo in decay_tail]
    scores[2] = _scores(q_ref, ka_ref, 2)
    soft[1] = softmax(1, scores.pop(1))
    finish(0, vals.pop(0), soft.pop(0))
    yield

    vals[1] = _values(soft[1][0], vv_ref, 1)
    scores[3] = _scores(q_ref, ka_ref, 3)
    xs = act_ref[:, :SSM_WIDTH].astype(F32)
    xdec_b = (xs * state_in_x).astype(BF16)
    lane_head = lax.broadcasted_iota(jnp.int32, (CHUNK, GROUP_WIDTH), 1) // SSM_HEAD_DIM
    for j in range(nch):
        for g in range(SSM_GROUPS):
            xg = act_ref[j * CHUNK:(j + 1) * CHUNK, g * GROUP_WIDTH:(g + 1) * GROUP_WIDTH]
            for r in range(HEADS_PER_GROUP):
                xbd_ref[j, g, r * CHUNK:(r + 1) * CHUNK, :] = jnp.where(lane_head == r, xg, jnp.zeros_like(xg))
    soft[2] = softmax(2, scores.pop(2))
    finish(1, vals.pop(1), soft.pop(1))
    yield

    bc_b = act_ref[:, SSM_WIDTH:]
    bm = [[bc_b[j * CHUNK:(j + 1) * CHUNK, g * D_STATE:(g + 1) * D_STATE] for g in range(SSM_GROUPS)]
          for j in range(nch)]
    cm = [[bc_b[j * CHUNK:(j + 1) * CHUNK, (SSM_GROUPS + g) * D_STATE:(SSM_GROUPS + g + 1) * D_STATE]
           for g in range(SSM_GROUPS)] for j in range(nch)]
    cbm = [[_dot_nt(cm[j][g], bm[j][g]) for g in range(SSM_GROUPS)] for j in range(nch)]
    d_state = [[_dot_tn(bm[j][g], grp(xdec_b, j, g)) for g in range(SSM_GROUPS)] for j in range(nch)]
    vals[2] = _values(soft[2][0], vv_ref, 2)
    soft[3] = softmax(3, scores.pop(3))

    def intra_weights(j):
        per_group = []
        for g in range(SSM_GROUPS):
            w = []
            for r in range(HEADS_PER_GROUP):
                h = g * HEADS_PER_GROUP + r
                seg = acs[j][:, h:h + 1] - acs_dt_t[j][h:h + 1, :]
                w.append(cbm[j][g] * jnp.exp2(jnp.where(own, seg, -jnp.inf)))
            per_group.append((jnp.concatenate(w[:2], axis=1).astype(BF16),
                              jnp.concatenate(w[2:], axis=1).astype(BF16)))
        return per_group

    def intra_product(j, w):
        return [_dot(w[g][0], xbd_ref[j, g, 0:2 * CHUNK, :]) + _dot(w[g][1], xbd_ref[j, g, 2 * CHUNK:, :])
                for g in range(SSM_GROUPS)]

    yield
    w = {0: intra_weights(0)}
    finish(2, vals.pop(2), soft.pop(2))
    yield

    vals[3] = _values(soft[3][0], vv_ref, 3)
    y_diag = [intra_product(0, w.pop(0))]
    w[1] = intra_weights(1)
    yield

    y_diag.append(intra_product(1, w.pop(1)))
    w[2] = intra_weights(2)
    finish(3, vals.pop(3), soft.pop(3))
    yield

    y_diag.append(intra_product(2, w.pop(2)))
    w[3] = intra_weights(3)
    yield

    y_diag.append(intra_product(3, w.pop(3)))
    entering = []
    for g in range(SSM_GROUPS):
        state = state_ref[g]
        per_chunk = []
        for j in range(nch):
            per_chunk.append(state.astype(BF16))
            state = state * chunk_decay[j][:, g * GROUP_WIDTH:(g + 1) * GROUP_WIDTH] + d_state[j][g]
        state_ref[g] = state
        entering.append(per_chunk)
    yield

    y_rows = [jnp.concatenate([y_diag[j][g] + _dot(cm[j][g], entering[g][j]) * grp(decay_in_x, j, g)
                               for g in range(SSM_GROUPS)], axis=-1) for j in range(nch)]
    y = jnp.concatenate(y_rows, axis=0) + dskip_ref[...] * xs
    y = y * gz_ref[...].astype(F32)
    y = jnp.concatenate([_rms(y[:, :GROUP_WIDTH]), _rms(y[:, GROUP_WIDTH:])], axis=-1)
    o_ref[:, ATTN_WIDTH:] = (y * sn_ref[...]).astype(BF16)


FFN_PHASES = D_FF // FFN_FC + D_MODEL // FFN_FC


def _ffn_steps(mix_ref, x_ref, mod_ref, wo_ref, n2_ref, wgu_ref, wd_ref, fn_ref, o_ref, h_ref, a_ref):
    gain2 = n2_ref[...] * (1.0 + mod_ref[4:5, :])
    x1 = x_ref[...] + mod_ref[2:3, :] * _dot(mix_ref[...], wo_ref[...])
    h_ref[...] = (_rms(x1) * gain2 + mod_ref[3:4, :]).astype(BF16)
    o_ref[...] = x1
    yield
    for c0 in range(0, D_FF, FFN_FC):
        hb = h_ref[...]
        g = _dot(hb, wgu_ref[:, c0:c0 + FFN_FC])
        u = _dot(hb, wgu_ref[:, D_FF + c0:D_FF + c0 + FFN_FC])
        a_ref[:, c0:c0 + FFN_FC] = (_silu(g) * u).astype(BF16)
        yield
    for c0 in range(0, D_MODEL, FFN_FC):
        cols = slice(c0, c0 + FFN_FC)
        o_ref[:, cols] = o_ref[:, cols] + mod_ref[5:6, cols] * _dot(a_ref[...], wd_ref[:, cols])
        if c0 + FFN_FC == D_MODEL:
            o_ref[...] = _rms(o_ref[...]) * fn_ref[...]
        yield


def _mixer_ffn_kernel(q_ref, kt_ref, ktp_ref, v_ref, vp_ref, gz_ref, act_ref, dt_ref, bias_ref, sinks_ref, dtb_ref,
                      alog_ref, dskip_ref, an_ref, sn_ref, expand_ref, x_ref, mod_ref, wo_ref, n2_ref, wgu_ref,
                      wd_ref, fn_ref, o_ref, state_ref, ka_ref, vv_ref, xbd_ref, mix_ref, h_ref, a_ref):
    s = pl.program_id(1)
    last = pl.num_programs(1) - 1
    mixer = lambda first: _mixer_steps(first, q_ref, kt_ref, ktp_ref, v_ref, vp_ref, gz_ref, act_ref, dt_ref,
                                       bias_ref, sinks_ref, dtb_ref, alog_ref, dskip_ref, an_ref, sn_ref,
                                       expand_ref, mix_ref, state_ref, ka_ref, vv_ref, xbd_ref)
    ffn = lambda: _ffn_steps(mix_ref, x_ref, mod_ref, wo_ref, n2_ref, wgu_ref, wd_ref, fn_ref, o_ref, h_ref, a_ref)

    @pl.when(s == 0)
    def _():
        state_ref[...] = jnp.zeros_like(state_ref)
        vv_ref[:, :, LANES:] = jnp.ones((N_KV_HEADS, BLOCK + MIX_ROWS, LANES), BF16)
        for _ in mixer(True):
            pass

    @pl.when((s > 0) & (s < last))
    def _():
        ffn_steps, issued = ffn(), 0
        next(ffn_steps)
        for k, _ in enumerate(mixer(False)):
            while issued < ((k + 1) * (FFN_PHASES - 1)) // MIXER_YIELDS:
                next(ffn_steps)
                issued += 1
        assert k == MIXER_YIELDS - 1
        for _ in ffn_steps:
            pass

    @pl.when(s == last)
    def _():
        for _ in ffn():
            pass


def _mixer_ffn_call(q, kt, v, gz, act, dt, bias, sinks, dt_bias, a_log, d_skip_x, attn_norm, ssm_norm, expand,
                    x, mod, w_o, norm2, w_gu, w_down, final_norm):
    b, s, _ = q.shape
    rows = MIX_ROWS
    n_tiles = s // rows
    mix_tile = lambda i, c: jnp.minimum(c, n_tiles - 1)
    ffn_tile = lambda i, c: jnp.maximum(c - 1, 0)
    prev_blk = lambda i, c: jnp.maximum(mix_tile(i, c) * MIX_CHUNKS - 1, 0)
    mrow = lambda width: pl.BlockSpec((None, rows, width), lambda i, c: (i, mix_tile(i, c), 0))
    frow = lambda: pl.BlockSpec((None, rows, D_MODEL), lambda i, c: (i, ffn_tile(i, c), 0))
    const = lambda shape: pl.BlockSpec(shape, lambda i, c: (0,) * len(shape))
    resident = lambda shape: pl.BlockSpec(shape, lambda i, c: (0,) * len(shape), pipeline_mode=pl.Buffered(1))
    return pl.pallas_call(
        _mixer_ffn_kernel,
        grid=(b, n_tiles + 1),
        in_specs=[
            mrow(ATTN_WIDTH),
            pl.BlockSpec((None, KV_WIDTH, rows), lambda i, c: (i, 0, mix_tile(i, c))),
            pl.BlockSpec((None, KV_WIDTH, BLOCK), lambda i, c: (i, 0, prev_blk(i, c))),
            mrow(KV_WIDTH),
            pl.BlockSpec((None, BLOCK, KV_WIDTH), lambda i, c: (i, prev_blk(i, c), 0)),
            mrow(SSM_WIDTH), mrow(XBC_WIDTH), mrow(LANES),
            const((2, N_Q_HEADS, BLOCK, BLOCK)),
            pl.BlockSpec(memory_space=pltpu.SMEM),
            const((1, LANES)), const((1, LANES)),
            const((1, SSM_WIDTH)), const((1, ATTN_WIDTH)), const((1, SSM_WIDTH)),
            const((LANES, SSM_WIDTH)),
            frow(),
            pl.BlockSpec((None, 6, D_MODEL), lambda i, c: (i, 0, 0)),
            resident((D_MODEL, D_MODEL)), const((1, D_MODEL)),
            resident((D_MODEL, 2 * D_FF)), resident((D_FF, D_MODEL)), const((1, D_MODEL)),
        ],
        out_specs=frow(),
        out_shape=jax.ShapeDtypeStruct((b, s, D_MODEL), F32),
        scratch_shapes=[
            pltpu.VMEM((SSM_GROUPS, D_STATE, GROUP_WIDTH), F32),
            pltpu.VMEM((N_KV_HEADS, 2, KV_WIDTH, BLOCK + rows), BF16),
            pltpu.VMEM((N_KV_HEADS, BLOCK + rows, 2 * LANES), BF16),
            pltpu.VMEM((MIX_CHUNKS, SSM_GROUPS, HEADS_PER_GROUP * CHUNK, GROUP_WIDTH), BF16),
            pltpu.VMEM((rows, ATTN_WIDTH + SSM_WIDTH), BF16),
            pltpu.VMEM((rows, D_MODEL), BF16),
            pltpu.VMEM((rows, D_FF), BF16),
        ],
        compiler_params=pltpu.CompilerParams(
            dimension_semantics=("parallel", "arbitrary"), vmem_limit_bytes=VMEM_LIMIT_BYTES),
        name="mixer_ffn",
    )(q, kt, kt, v, v, gz, act, dt, bias, sinks, dt_bias, a_log, d_skip_x, attn_norm, ssm_norm, expand,
      x, mod, w_o, norm2, w_gu, w_down, final_norm)


def _head_expand_matrix():
    e = np.zeros((LANES, SSM_WIDTH), np.float32)
    for h in range(SSM_HEADS):
        e[h, h * SSM_HEAD_DIM:(h + 1) * SSM_HEAD_DIM] = 1.0
    return e


def kernel(x, c, ada_w, ada_b, norm1, w_in, conv_w, conv_b, dt_bias, A_log, D_skip, sinks, attn_out_norm,
           ssm_out_norm, w_o, norm2, w_gate_up, w_down, rel_bias, final_norm):
    assert ada_w.shape[0] == 1, "the final RMSNorm is fused into the (single) layer's FFN call"
    b = x.shape[0]
    bias = _bias_call(rel_bias)
    expand = jnp.asarray(_head_expand_matrix(), BF16)
    pad_heads = lambda v: jnp.pad(v, (0, LANES - SSM_HEADS))[None, :]
    c_pad = jnp.pad(c, ((0, SUBLANES - b), (0, 0)))
    mod = _ada_call(c_pad, ada_w[0], ada_b[0][None, :])[:b].reshape(b, 6, D_MODEL)
    w_main = w_in.astype(BF16)[0]
    w_dt = jnp.pad(w_main[:, XBC_END:], ((0, 0), (0, LANES - SSM_HEADS)))
    q, kt, v, gz, act, dt = _inproj_call(x, mod, norm1[0][None, :], w_main, w_dt, conv_w[0], conv_b[0][None, :])
    return _mixer_ffn_call(
        q, kt, v, gz, act, dt, bias, sinks[0], pad_heads(dt_bias[0]), pad_heads(A_log[0]),
        jnp.repeat(D_skip[0], SSM_HEAD_DIM)[None, :], attn_out_norm[0][None, :], ssm_out_norm[0][None, :], expand,
        x, mod, w_o[0].astype(BF16), norm2[0][None, :], w_gate_up[0].astype(BF16), w_down[0].astype(BF16),
        final_norm[None, :])
```

```python
import numpy as np
import jax
import jax.numpy as jnp
from jax import lax
from jax.experimental import pallas as pl
from jax.experimental.pallas import tpu as pltpu

F32 = jnp.float32
BF16 = jnp.bfloat16

D_MODEL = 1024
HEAD_DIM = 64
N_Q_HEADS = 8
N_KV_HEADS = 2
Q_PER_KV = N_Q_HEADS // N_KV_HEADS
ATTN_WIDTH = N_Q_HEADS * HEAD_DIM
KV_WIDTH = N_KV_HEADS * HEAD_DIM
WINDOW = 128
BLOCK = 128
N_BUCKETS = 32
MAX_DISTANCE = 128

SSM_HEAD_DIM = 64
SSM_HEADS = 8
SSM_GROUPS = 2
HEADS_PER_GROUP = SSM_HEADS // SSM_GROUPS
SSM_WIDTH = SSM_HEADS * SSM_HEAD_DIM
GROUP_WIDTH = SSM_WIDTH // SSM_GROUPS
D_STATE = 128
CONV_K = 4
CHUNK = 128
XBC_WIDTH = SSM_WIDTH + 2 * SSM_GROUPS * D_STATE
D_FF = 2816
EPS = 1e-6
LOG2E = float(np.log2(np.e))

LANES = 128
SUBLANES = 8
VMEM_LIMIT_BYTES = 56 * 1024 * 1024

ADA_TN = 1536
INPROJ_TM = 1024
FFN_FC = 256
MIX_CHUNKS = 4
MIXER_YIELDS = 10
MIX_ROWS = MIX_CHUNKS * CHUNK

Q_END = ATTN_WIDTH
K_END = Q_END + KV_WIDTH
V_END = K_END + KV_WIDTH
Z_END = V_END + SSM_WIDTH
XBC_END = Z_END + XBC_WIDTH

assert BLOCK == CHUNK == LANES and KV_WIDTH == LANES and 2 * HEAD_DIM == LANES


def _silu(x):
    h = 0.5 * x
    return h * jnp.tanh(h) + h


def _rms(x):
    return x * lax.rsqrt(jnp.mean(x * x, axis=-1, keepdims=True) + EPS)


def _split2(x):
    hi = x.astype(BF16)
    lo = (x - hi.astype(F32)).astype(BF16)
    return hi, lo


def _dot(a, b):
    return jnp.dot(a, b, preferred_element_type=F32)


def _dot_nt(a, b):
    return lax.dot_general(a, b, (((1,), (1,)), ((), ())), preferred_element_type=F32)


def _dot_tn(a, b):
    return lax.dot_general(a, b, (((0,), (0,)), ((), ())), preferred_element_type=F32)


def _ada_kernel(c_ref, w_ref, b_ref, o_ref):
    cond = _silu(c_ref[...])
    o_ref[...] = _dot(cond.astype(BF16), w_ref[...].astype(BF16)) + b_ref[...]


def _ada_call(c_pad, ada_w, ada_b):
    n = ada_w.shape[1]
    return pl.pallas_call(
        _ada_kernel,
        grid=(n // ADA_TN,),
        in_specs=[
            pl.BlockSpec((SUBLANES, D_MODEL), lambda j: (0, 0)),
            pl.BlockSpec((D_MODEL, ADA_TN), lambda j: (0, j)),
            pl.BlockSpec((1, ADA_TN), lambda j: (0, j)),
        ],
        out_specs=pl.BlockSpec((SUBLANES, ADA_TN), lambda j: (0, j)),
        out_shape=jax.ShapeDtypeStruct((SUBLANES, n), F32),
        compiler_params=pltpu.CompilerParams(dimension_semantics=("parallel",)),
        name="ada_mod",
    )(c_pad, ada_w, ada_b)


def _t5_bucket_table():
    i = np.arange(BLOCK)[:, None]
    j = np.arange(BLOCK)[None, :]
    n = np.where(j <= i, i - j, i + BLOCK - j)
    assert WINDOW == BLOCK and n.min() >= 0 and n.max() < WINDOW
    max_exact = N_BUCKETS // 2
    large = max_exact + (np.log(np.maximum(n, 1) / max_exact) / np.log(MAX_DISTANCE / max_exact)
                         * (N_BUCKETS - max_exact)).astype(np.int32)
    large = np.minimum(large, N_BUCKETS - 1)
    bucket = np.where(n < max_exact, n, large).astype(np.int32)
    first = np.where(j <= i, bucket, -1)
    return np.stack([first, bucket]).astype(np.int32)


def _bias_kernel(idx_ref, rb_ref, o_ref):
    for v in range(2):
        idx = idx_ref[v]
        hits = [idx == b for b in range(N_BUCKETS)]
        for h in range(N_Q_HEADS):
            acc = jnp.full((BLOCK, BLOCK), -jnp.inf, F32)
            for b in range(N_BUCKETS):
                acc = jnp.where(hits[b], rb_ref[b, h] * LOG2E, acc)
            o_ref[v, h] = acc


def _bias_call(rel_bias):
    idx = jnp.asarray(_t5_bucket_table())
    return pl.pallas_call(
        _bias_kernel,
        in_specs=[
            pl.BlockSpec(memory_space=pltpu.VMEM),
            pl.BlockSpec(memory_space=pltpu.SMEM),
        ],
        out_specs=pl.BlockSpec(memory_space=pltpu.VMEM),
        out_shape=jax.ShapeDtypeStruct((2, N_Q_HEADS, BLOCK, BLOCK), F32),
        name="t5_bias",
    )(idx, rel_bias)


def _inproj_norm(x_ref, mod_ref, n1_ref, hb_ref):
    gain = n1_ref[...] * (1.0 + mod_ref[1:2, :])
    hb_ref[...] = (_rms(x_ref[...]) * gain + mod_ref[0:1, :]).astype(BF16)


def _inproj_project(hb_ref, w_ref, wdt_ref, cw_ref, cb_ref, q_ref, kt_ref, v_ref, gz_ref, act_ref, dt_ref, xe_ref):
    tm = hb_ref.shape[0]
    proj = lambda lo, hi: _dot(hb_ref[...], w_ref[:, lo:hi])
    tile = 2 * LANES
    half = tm // 2
    for i, c0 in enumerate(range(0, XBC_WIDTH, tile)):
        cols = slice(c0, c0 + tile)
        carry = xe_ref[:, cols]
        for r0 in range(0, tm, half):
            xbc = _dot(hb_ref[r0:r0 + half, :], w_ref[:, Z_END + c0:Z_END + c0 + tile])
            xe = jnp.concatenate([carry, xbc], axis=0)
            xe = xe.reshape(half // SUBLANES + 1, SUBLANES, tile)
            slab_row = lax.broadcasted_iota(jnp.int32, xe.shape, 1)

            def shift_down(a, n):
                rot = pltpu.roll(a, n, axis=1)
                return jnp.where(slab_row < n, jnp.concatenate([rot[:1], rot[:-1]], axis=0), rot)

            w0, w1, w2, w3 = (cw_ref[k:k + 1, cols] for k in range(CONV_K))
            x1 = shift_down(xe, 1)
            taps = shift_down(w0 * x1 + w1 * xe, 2) + (w2 * x1 + w3 * xe)
            conv = taps[1:].reshape(half, tile) + cb_ref[:, cols]
            carry = xbc[half - SUBLANES:, :]
            act_ref[r0:r0 + half, cols] = _silu(conv).astype(BF16)
        xe_ref[:, cols] = carry
        if i < Q_END // tile:
            q_ref[:, i * tile:(i + 1) * tile] = proj(i * tile, (i + 1) * tile).astype(BF16)
        else:
            z0 = (i - Q_END // tile) * tile
            gz_ref[:, z0:z0 + tile] = _silu(proj(V_END + z0, V_END + z0 + tile)).astype(BF16)
    kv = proj(Q_END, V_END)
    kt_ref[...] = (kv[:, :KV_WIDTH] * (HEAD_DIM ** -0.5 * LOG2E)).T.astype(BF16)
    v_ref[...] = kv[:, KV_WIDTH:].astype(BF16)
    dt_ref[...] = _dot(hb_ref[...], wdt_ref[...])


def _inproj_kernel(x_ref, mod_ref, n1_ref, w_ref, wdt_ref, cw_ref, cb_ref, q_ref, kt_ref, v_ref, gz_ref, act_ref,
                   dt_ref, hb_ref, xe_ref):
    @pl.when(pl.program_id(1) == 0)
    def _():
        xe_ref[...] = jnp.zeros_like(xe_ref)

    _inproj_norm(x_ref, mod_ref, n1_ref, hb_ref)
    _inproj_project(hb_ref, w_ref, wdt_ref, cw_ref, cb_ref, q_ref, kt_ref, v_ref, gz_ref, act_ref, dt_ref, xe_ref)


def _inproj_call(x, mod, norm1, w_main, w_dt, conv_w, conv_b):
    b, s, _ = x.shape
    tm = INPROJ_TM
    row = lambda width: pl.BlockSpec((None, tm, width), lambda i, j: (i, j, 0))
    const = lambda shape: pl.BlockSpec(shape, lambda i, j: (0,) * len(shape))
    return pl.pallas_call(
        _inproj_kernel,
        grid=(b, s // tm),
        in_specs=[
            row(D_MODEL),
            pl.BlockSpec((None, 6, D_MODEL), lambda i, j: (i, 0, 0)),
            const((1, D_MODEL)),
            const((D_MODEL, XBC_END + SSM_HEADS)),
            const((D_MODEL, LANES)),
            const((CONV_K, XBC_WIDTH)), const((1, XBC_WIDTH)),
        ],
        out_specs=[row(ATTN_WIDTH), pl.BlockSpec((None, KV_WIDTH, tm), lambda i, j: (i, 0, j)), row(KV_WIDTH),
                   row(SSM_WIDTH), row(XBC_WIDTH), row(LANES)],
        out_shape=[
            jax.ShapeDtypeStruct((b, s, ATTN_WIDTH), BF16),
            jax.ShapeDtypeStruct((b, KV_WIDTH, s), BF16),
            jax.ShapeDtypeStruct((b, s, KV_WIDTH), BF16),
            jax.ShapeDtypeStruct((b, s, SSM_WIDTH), BF16),
            jax.ShapeDtypeStruct((b, s, XBC_WIDTH), BF16),
            jax.ShapeDtypeStruct((b, s, LANES), F32),
        ],
        scratch_shapes=[
            pltpu.VMEM((tm, D_MODEL), BF16),
            pltpu.VMEM((SUBLANES, XBC_WIDTH), F32),
        ],
        compiler_params=pltpu.CompilerParams(
            dimension_semantics=("parallel", "arbitrary"), vmem_limit_bytes=VMEM_LIMIT_BYTES),
        name="in_proj",
    )(x, mod, norm1, w_main, w_dt, conv_w, conv_b)


def _stage_keys_values(kt_ref, ktp_ref, v_ref, vp_ref, ka_ref, vv_ref):
    zeros = jnp.zeros((HEAD_DIM, BLOCK + MIX_ROWS), BF16)
    for g in range(N_KV_HEADS):
        rows = slice(g * HEAD_DIM, (g + 1) * HEAD_DIM)
        for par in range(2):
            live = slice(par * HEAD_DIM, (par + 1) * HEAD_DIM)
            dead = slice((1 - par) * HEAD_DIM, (2 - par) * HEAD_DIM)
            ka_ref[g, par, live, 0:BLOCK] = ktp_ref[rows, :]
            ka_ref[g, par, live, BLOCK:] = kt_ref[rows, :]
            ka_ref[g, par, dead, :] = zeros

    def duplicate_halves(v):
        v32 = pltpu.bitcast(v, jnp.uint32)
        swapped = pltpu.roll(v32, HEAD_DIM, axis=1)
        lane = lax.broadcasted_iota(jnp.int32, v32.shape, 1)
        first = lane < HEAD_DIM
        return (pltpu.bitcast(jnp.where(first, v32, swapped), BF16),
                pltpu.bitcast(jnp.where(first, swapped, v32), BF16))

    for src, r0, r1 in ((vp_ref, 0, BLOCK), (v_ref, BLOCK, BLOCK + MIX_ROWS)):
        for g, dup in enumerate(duplicate_halves(src[...])):
            vv_ref[g, r0:r1, 0:LANES] = dup


def _scores(q_ref, ka_ref, j):
    rows = pl.ds(j * BLOCK, BLOCK)
    win = pl.ds(j * BLOCK, 2 * BLOCK)
    out = []
    for h in range(N_Q_HEADS):
        t, par = h // 2, h % 2
        out.append(_dot(q_ref[rows, t * LANES:(t + 1) * LANES], ka_ref[t // 2, par, :, win]))
    return out


def _softmax(scores, bias_ref, sinks_ref, own):
    s = [jnp.where(own, sc[:, BLOCK:], sc[:, :BLOCK]) + bias_ref[h] for h, sc in enumerate(scores)]
    sink = [sinks_ref[h] * LOG2E for h in range(N_Q_HEADS)]
    m = [jnp.maximum(jnp.max(s[h], axis=-1, keepdims=True), sink[h]) for h in range(N_Q_HEADS)]
    p = [jnp.exp2(s[h] - m[h]) for h in range(N_Q_HEADS)]
    zero = jnp.zeros((BLOCK, BLOCK), BF16)
    pb = [ph.astype(BF16) for ph in p]
    pcat = [jnp.concatenate([jnp.where(own, zero, b), jnp.where(own, b, zero)], axis=1) for b in pb]
    sink_term = [jnp.exp2(sink[h] - m[h]) for h in range(N_Q_HEADS)]
    return pcat, sink_term


def _values(pcat, vv_ref, j):
    win = pl.ds(j * BLOCK, 2 * BLOCK)
    return [_dot(pcat[h], vv_ref[h // Q_PER_KV, win, :]) for h in range(N_Q_HEADS)]


def _finish_attention(o2, sink_term, an_ref, o_ref, j):
    rows = pl.ds(j * BLOCK, BLOCK)
    heads = [o2[h][:, :LANES] / (o2[h][:, LANES:] + sink_term[h]) for h in range(N_Q_HEADS)]
    lane = lax.broadcasted_iota(jnp.int32, (BLOCK, LANES), 1)
    tiles = [jnp.where(lane < HEAD_DIM, heads[2 * t], heads[2 * t + 1]) for t in range(N_Q_HEADS // 2)]
    ssq = sum(jnp.sum(tl * tl, axis=-1, keepdims=True) for tl in tiles)
    inv = lax.rsqrt(ssq * (1.0 / ATTN_WIDTH) + EPS)
    for t, tl in enumerate(tiles):
        cols = slice(t * LANES, (t + 1) * LANES)
        o_ref[rows, cols] = (tl * inv * an_ref[:, cols]).astype(BF16)


def _mixer_steps(first_tile, q_ref, kt_ref, ktp_ref, v_ref, vp_ref, gz_ref, act_ref, dt_ref, bias_ref, sinks_ref,
                 dtb_ref, alog_ref, dskip_ref, an_ref, sn_ref, expand_ref, o_ref, state_ref, ka_ref, vv_ref,
                 xbd_ref):
    nch = MIX_CHUNKS
    row_i = lax.broadcasted_iota(jnp.int32, (BLOCK, BLOCK), 0)
    col_i = lax.broadcasted_iota(jnp.int32, (BLOCK, BLOCK), 1)
    own = col_i <= row_i
    bias_of = lambda j: bias_ref.at[0 if (first_tile and j == 0) else 1]
    grp = lambda a, j, g: a[j * CHUNK:(j + 1) * CHUNK, g * GROUP_WIDTH:(g + 1) * GROUP_WIDTH]
    softmax = lambda j, sc: _softmax(sc, bias_of(j), sinks_ref, own)
    finish = lambda j, vals, soft: _finish_attention(vals, soft[1], an_ref, o_ref, j)

    _stage_keys_values(kt_ref, ktp_ref, v_ref, vp_ref, ka_ref, vv_ref)
    dt = jax.nn.softplus(dt_ref[...] + dtb_ref[...])
    dta = dt * (-LOG2E * jnp.exp(alog_ref[...]))
    tri = own.astype(BF16)
    dta_hi, dta_lo = _split2(dta)
    scores = {0: _scores(q_ref, ka_ref, 0), 1: _scores(q_ref, ka_ref, 1)}
    acs = [_dot(tri, dta_hi[j * CHUNK:(j + 1) * CHUNK]) + _dot(tri, dta_lo[j * CHUNK:(j + 1) * CHUNK])
           for j in range(nch)]
    yield

    soft = {0: softmax(0, scores.pop(0))}
    acs_dt_t = [(a - jnp.log2(dt[j * CHUNK:(j + 1) * CHUNK])).T for j, a in enumerate(acs)]
    dec_in = [jnp.exp2(a) for a in acs]
    dec_out = jnp.concatenate([jnp.exp2(a[CHUNK - 1:CHUNK, :] - a) for a in acs], axis=0)
    state_in = (dt * dec_out).astype(BF16)
    decay_in = jnp.concatenate(dec_in, axis=0).astype(BF16)
    decay_tail = [_split2(d[CHUNK - SUBLANES:, :]) for d in dec_in]
    yield

    vals = {0: _values(soft[0][0], vv_ref, 0)}
    expand = expand_ref[...]
    state_in_x = _dot(state_in, expand)
    decay_in_x = _dot(decay_in, expand)
    chunk_decay = [(_dot(hi, expand) + _dot(lo, expand))[SUBLANES - 1:SUBLANES, :] for hi, lo in decay_tail]
    scores[2] = _scores(q_ref, ka_ref, 2)
    soft[1] = softmax(1, scores.pop(1))
    finish(0, vals.pop(0), soft.pop(0))
    yield

    vals[1] = _values(soft[1][0], vv_ref, 1)
    scores[3] = _scores(q_ref, ka_ref, 3)
    xs = act_ref[:, :SSM_WIDTH].astype(F32)
    xdec_b = (xs * state_in_x).astype(BF16)
    lane_head = lax.broadcasted_iota(jnp.int32, (CHUNK, GROUP_WIDTH), 1) // SSM_HEAD_DIM
    for j in range(nch):
        for g in range(SSM_GROUPS):
            xg = act_ref[j * CHUNK:(j + 1) * CHUNK, g * GROUP_WIDTH:(g + 1) * GROUP_WIDTH]
            for r in range(HEADS_PER_GROUP):
                xbd_ref[j, g, r * CHUNK:(r + 1) * CHUNK, :] = jnp.where(lane_head == r, xg, jnp.zeros_like(xg))
    soft[2] = softmax(2, scores.pop(2))
    finish(1, vals.pop(1), soft.pop(1))
    yield

    bc_b = act_ref[:, SSM_WIDTH:]
    bm = [[bc_b[j * CHUNK:(j + 1) * CHUNK, g * D_STATE:(g + 1) * D_STATE] for g in range(SSM_GROUPS)]
          for j in range(nch)]
    cm = [[bc_b[j * CHUNK:(j + 1) * CHUNK, (SSM_GROUPS + g) * D_STATE:(SSM_GROUPS + g + 1) * D_STATE]
           for g in range(SSM_GROUPS)] for j in range(nch)]
    cbm = [[_dot_nt(cm[j][g], bm[j][g]) for g in range(SSM_GROUPS)] for j in range(nch)]
    d_state = [[_dot_tn(bm[j][g], grp(xdec_b, j, g)) for g in range(SSM_GROUPS)] for j in range(nch)]
    vals[2] = _values(soft[2][0], vv_ref, 2)
    soft[3] = softmax(3, scores.pop(3))

    def intra_weights(j):
        per_group = []
        for g in range(SSM_GROUPS):
            w = []
            for r in range(HEADS_PER_GROUP):
                h = g * HEADS_PER_GROUP + r
                seg = acs[j][:, h:h + 1] - acs_dt_t[j][h:h + 1, :]
                w.append(cbm[j][g] * jnp.exp2(jnp.where(own, seg, -jnp.inf)))
            per_group.append((jnp.concatenate(w[:2], axis=1).astype(BF16),
                              jnp.concatenate(w[2:], axis=1).astype(BF16)))
        return per_group

    def intra_product(j, w):
        return [_dot(w[g][0], xbd_ref[j, g, 0:2 * CHUNK, :]) + _dot(w[g][1], xbd_ref[j, g, 2 * CHUNK:, :])
                for g in range(SSM_GROUPS)]

    yield
    w = {0: intra_weights(0)}
    finish(2, vals.pop(2), soft.pop(2))
    yield

    vals[3] = _values(soft[3][0], vv_ref, 3)
    y_diag = [intra_product(0, w.pop(0))]
    w[1] = intra_weights(1)
    yield

    y_diag.append(intra_product(1, w.pop(1)))
    w[2] = intra_weights(2)
    finish(3, vals.pop(3), soft.pop(3))
    yield

    y_diag.append(intra_product(2, w.pop(2)))
    w[3] = intra_weights(3)
    yield

    y_diag.append(intra_product(3, w.pop(3)))
    entering = []
    for g in range(SSM_GROUPS):
        state = state_ref[g]
        per_chunk = []
        for j in range(nch):
            per_chunk.append(state.astype(BF16))
            state = state * chunk_decay[j][:, g * GROUP_WIDTH:(g + 1) * GROUP_WIDTH] + d_state[j][g]
        state_ref[g] = state
        entering.append(per_chunk)
    yield

    y_rows = [jnp.concatenate([y_diag[j][g] + _dot(cm[j][g], entering[g][j]) * grp(decay_in_x, j, g)
                               for g in range(SSM_GROUPS)], axis=-1) for j in range(nch)]
    y = jnp.concatenate(y_rows, axis=0) + dskip_ref[...] * xs
    y = y * gz_ref[...].astype(F32)
    y = jnp.concatenate([_rms(y[:, :GROUP_WIDTH]), _rms(y[:, GROUP_WIDTH:])], axis=-1)
    o_ref[:, ATTN_WIDTH:] = (y * sn_ref[...]).astype(BF16)


FFN_PHASES = D_FF // FFN_FC + D_MODEL // FFN_FC


def _ffn_steps(mix_ref, x_ref, mod_ref, wo_ref, n2_ref, wgu_ref, wd_ref, fn_ref, o_ref, h_ref, a_ref):
    gain2 = n2_ref[...] * (1.0 + mod_ref[4:5, :])
    x1 = x_ref[...] + mod_ref[2:3, :] * _dot(mix_ref[...], wo_ref[...])
    h_ref[...] = (_rms(x1) * gain2 + mod_ref[3:4, :]).astype(BF16)
    o_ref[...] = x1
    yield
    for c0 in range(0, D_FF, FFN_FC):
        hb = h_ref[...]
        g = _dot(hb, wgu_ref[:, c0:c0 + FFN_FC])
        u = _dot(hb, wgu_ref[:, D_FF + c0:D_FF + c0 + FFN_FC])
        a_ref[:, c0:c0 + FFN_FC] = (_silu(g) * u).astype(BF16)
        yield
    for c0 in range(0, D_MODEL, FFN_FC):
        cols = slice(c0, c0 + FFN_FC)
        o_ref[:, cols] = o_ref[:, cols] + mod_ref[5:6, cols] * _dot(a_ref[...], wd_ref[:, cols])
        if c0 + FFN_FC == D_MODEL:
            o_ref[...] = _rms(o_ref[...]) * fn_ref[...]
        yield


def _mixer_ffn_kernel(q_ref, kt_ref, ktp_ref, v_ref, vp_ref, gz_ref, act_ref, dt_ref, bias_ref, sinks_ref, dtb_ref,
                      alog_ref, dskip_ref, an_ref, sn_ref, expand_ref, x_ref, mod_ref, wo_ref, n2_ref, wgu_ref,
                      wd_ref, fn_ref, o_ref, state_ref, ka_ref, vv_ref, xbd_ref, mix_ref, h_ref, a_ref):
    s = pl.program_id(1)
    last = pl.num_programs(1) - 1
    mixer = lambda first: _mixer_steps(first, q_ref, kt_ref, ktp_ref, v_ref, vp_ref, gz_ref, act_ref, dt_ref,
                                       bias_ref, sinks_ref, dtb_ref, alog_ref, dskip_ref, an_ref, sn_ref,
                                       expand_ref, mix_ref, state_ref, ka_ref, vv_ref, xbd_ref)
    ffn = lambda: _ffn_steps(mix_ref, x_ref, mod_ref, wo_ref, n2_ref, wgu_ref, wd_ref, fn_ref, o_ref, h_ref, a_ref)

    @pl.when(s == 0)
    def _():
        state_ref[...] = jnp.zeros_like(state_ref)
        vv_ref[:, :, LANES:] = jnp.ones((N_KV_HEADS, BLOCK + MIX_ROWS, LANES), BF16)
        for _ in mixer(True):
            pass

    @pl.when((s > 0) & (s < last))
    def _():
        ffn_steps, issued = ffn(), 0
        next(ffn_steps)
        for k, _ in enumerate(mixer(False)):
            while issued < ((k + 1) * (FFN_PHASES - 1)) // MIXER_YIELDS:
                next(ffn_steps)
                issued += 1
        assert k == MIXER_YIELDS - 1
        for _ in ffn_steps:
            pass

    @pl.when(s == last)
    def _():
        for _ in ffn():
            pass


def _mixer_ffn_call(q, kt, v, gz, act, dt, bias, sinks, dt_bias, a_log, d_skip_x, attn_norm, ssm_norm, expand,
                    x, mod, w_o, norm2, w_gu, w_down, final_norm):
    b, s, _ = q.shape
    rows = MIX_ROWS
    n_tiles = s // rows
    mix_tile = lambda i, c: jnp.minimum(c, n_tiles - 1)
    ffn_tile = lambda i, c: jnp.maximum(c - 1, 0)
    prev_blk = lambda i, c: jnp.maximum(mix_tile(i, c) * MIX_CHUNKS - 1, 0)
    mrow = lambda width: pl.BlockSpec((None, rows, width), lambda i, c: (i, mix_tile(i, c), 0))
    frow = lambda: pl.BlockSpec((None, rows, D_MODEL), lambda i, c: (i, ffn_tile(i, c), 0))
    const = lambda shape: pl.BlockSpec(shape, lambda i, c: (0,) * len(shape))
    resident = lambda shape: pl.BlockSpec(shape, lambda i, c: (0,) * len(shape), pipeline_mode=pl.Buffered(1))
    return pl.pallas_call(
        _mixer_ffn_kernel,
        grid=(b, n_tiles + 1),
        in_specs=[
            mrow(ATTN_WIDTH),
            pl.BlockSpec((None, KV_WIDTH, rows), lambda i, c: (i, 0, mix_tile(i, c))),
            pl.BlockSpec((None, KV_WIDTH, BLOCK), lambda i, c: (i, 0, prev_blk(i, c))),
            mrow(KV_WIDTH),
            pl.BlockSpec((None, BLOCK, KV_WIDTH), lambda i, c: (i, prev_blk(i, c), 0)),
            mrow(SSM_WIDTH), mrow(XBC_WIDTH), mrow(LANES),
            const((2, N_Q_HEADS, BLOCK, BLOCK)),
            pl.BlockSpec(memory_space=pltpu.SMEM),
            const((1, LANES)), const((1, LANES)),
            const((1, SSM_WIDTH)), const((1, ATTN_WIDTH)), const((1, SSM_WIDTH)),
            const((LANES, SSM_WIDTH)),
            frow(),
            pl.BlockSpec((None, 6, D_MODEL), lambda i, c: (i, 0, 0)),
            resident((D_MODEL, D_MODEL)), const((1, D_MODEL)),
            resident((D_MODEL, 2 * D_FF)), resident((D_FF, D_MODEL)), const((1, D_MODEL)),
        ],
        out_specs=frow(),
        out_shape=jax.ShapeDtypeStruct((b, s, D_MODEL), F32),
        scratch_shapes=[
            pltpu.VMEM((SSM_GROUPS, D_STATE, GROUP_WIDTH), F32),
            pltpu.VMEM((N_KV_HEADS, 2, KV_WIDTH, BLOCK + rows), BF16),
            pltpu.VMEM((N_KV_HEADS, BLOCK + rows, 2 * LANES), BF16),
            pltpu.VMEM((MIX_CHUNKS, SSM_GROUPS, HEADS_PER_GROUP * CHUNK, GROUP_WIDTH), BF16),
            pltpu.VMEM((rows, ATTN_WIDTH + SSM_WIDTH), BF16),
            pltpu.VMEM((rows, D_MODEL), BF16),
            pltpu.VMEM((rows, D_FF), BF16),
        ],
        compiler_params=pltpu.CompilerParams(
            dimension_semantics=("parallel", "arbitrary"), vmem_limit_bytes=VMEM_LIMIT_BYTES),
        name="mixer_ffn",
    )(q, kt, kt, v, v, gz, act, dt, bias, sinks, dt_bias, a_log, d_skip_x, attn_norm, ssm_norm, expand,
      x, mod, w_o, norm2, w_gu, w_down, final_norm)


def _head_expand_matrix():
    e = np.zeros((LANES, SSM_WIDTH), np.float32)
    for h in range(SSM_HEADS):
        e[h, h * SSM_HEAD_DIM:(h + 1) * SSM_HEAD_DIM] = 1.0
    return e


def kernel(x, c, ada_w, ada_b, norm1, w_in, conv_w, conv_b, dt_bias, A_log, D_skip, sinks, attn_out_norm,
           ssm_out_norm, w_o, norm2, w_gate_up, w_down, rel_bias, final_norm):
    assert ada_w.shape[0] == 1, "the final RMSNorm is fused into the (single) layer's FFN call"
    b = x.shape[0]
    bias = _bias_call(rel_bias)
    expand = jnp.asarray(_head_expand_matrix(), BF16)
    pad_heads = lambda v: jnp.pad(v, (0, LANES - SSM_HEADS))[None, :]
    c_pad = jnp.pad(c, ((0, SUBLANES - b), (0, 0)))
    mod = _ada_call(c_pad, ada_w[0], ada_b[0][None, :])[:b].reshape(b, 6, D_MODEL)
    w_main = w_in.astype(BF16)[0]
    w_dt = jnp.pad(w_main[:, XBC_END:], ((0, 0), (0, LANES - SSM_HEADS)))
    q, kt, v, gz, act, dt = _inproj_call(x, mod, norm1[0][None, :], w_main, w_dt, conv_w[0], conv_b[0][None, :])
    return _mixer_ffn_call(
        q, kt, v, gz, act, dt, bias, sinks[0], pad_heads(dt_bias[0]), pad_heads(A_log[0]),
        jnp.repeat(D_skip[0], SSM_HEAD_DIM)[None, :], attn_out_norm[0][None, :], ssm_out_norm[0][None, :], expand,
        x, mod, w_o[0].astype(BF16), norm2[0][None, :], w_gate_up[0].astype(BF16), w_down[0].astype(BF16),
        final_norm[None, :])
```

```python
import numpy as np
import jax
import jax.numpy as jnp
from jax import lax
from jax.experimental import pallas as pl
from jax.experimental.pallas import tpu as pltpu

F32 = jnp.float32
BF16 = jnp.bfloat16

D_MODEL = 1024
HEAD_DIM = 64
N_Q_HEADS = 8
N_KV_HEADS = 2
Q_PER_KV = N_Q_HEADS // N_KV_HEADS
ATTN_WIDTH = N_Q_HEADS * HEAD_DIM
KV_WIDTH = N_KV_HEADS * HEAD_DIM
WINDOW = 128
BLOCK = 128
N_BUCKETS = 32
MAX_DISTANCE = 128

SSM_HEAD_DIM = 64
SSM_HEADS = 8
SSM_GROUPS = 2
HEADS_PER_GROUP = SSM_HEADS // SSM_GROUPS
SSM_WIDTH = SSM_HEADS * SSM_HEAD_DIM
GROUP_WIDTH = SSM_WIDTH // SSM_GROUPS
D_STATE = 128
CONV_K = 4
CHUNK = 128
XBC_WIDTH = SSM_WIDTH + 2 * SSM_GROUPS * D_STATE
D_FF = 2816
EPS = 1e-6
LOG2E = float(np.log2(np.e))

LANES = 128
SUBLANES = 8
VMEM_LIMIT_BYTES = 56 * 1024 * 1024

ADA_TN = 1536
INPROJ_TM = 1024
FFN_FC = 256
MIX_CHUNKS = 4
MIXER_YIELDS = 10
MIX_ROWS = MIX_CHUNKS * CHUNK

Q_END = ATTN_WIDTH
K_END = Q_END + KV_WIDTH
V_END = K_END + KV_WIDTH
Z_END = V_END + SSM_WIDTH
XBC_END = Z_END + XBC_WIDTH

assert BLOCK == CHUNK == LANES and KV_WIDTH == LANES and 2 * HEAD_DIM == LANES


def _silu_of_half(h):
    return h * jnp.tanh(h) + h


def _silu(x):
    return _silu_of_half(0.5 * x)


def _rms(x):
    return x * lax.rsqrt(jnp.mean(x * x, axis=-1, keepdims=True) + EPS)


def _split2(x):
    hi = x.astype(BF16)
    lo = (x - hi.astype(F32)).astype(BF16)
    return hi, lo


def _dot(a, b):
    return jnp.dot(a, b, preferred_element_type=F32)


def _dot_nt(a, b):
    return lax.dot_general(a, b, (((1,), (1,)), ((), ())), preferred_element_type=F32)


def _dot_tn(a, b):
    return lax.dot_general(a, b, (((0,), (0,)), ((), ())), preferred_element_type=F32)


def _ada_kernel(c_ref, w_ref, b_ref, o_ref):
    cond = _silu(c_ref[...])
    o_ref[...] = _dot(cond.astype(BF16), w_ref[...].astype(BF16)) + b_ref[...]


def _ada_call(c_pad, ada_w, ada_b):
    n = ada_w.shape[1]
    return pl.pallas_call(
        _ada_kernel,
        grid=(n // ADA_TN,),
        in_specs=[
            pl.BlockSpec((SUBLANES, D_MODEL), lambda j: (0, 0)),
            pl.BlockSpec((D_MODEL, ADA_TN), lambda j: (0, j)),
            pl.BlockSpec((1, ADA_TN), lambda j: (0, j)),
        ],
        out_specs=pl.BlockSpec((SUBLANES, ADA_TN), lambda j: (0, j)),
        out_shape=jax.ShapeDtypeStruct((SUBLANES, n), F32),
        compiler_params=pltpu.CompilerParams(dimension_semantics=("parallel",)),
        name="ada_mod",
    )(c_pad, ada_w, ada_b)


def _t5_bucket_table():
    i = np.arange(BLOCK)[:, None]
    j = np.arange(BLOCK)[None, :]
    n = np.where(j <= i, i - j, i + BLOCK - j)
    assert WINDOW == BLOCK and n.min() >= 0 and n.max() < WINDOW
    max_exact = N_BUCKETS // 2
    large = max_exact + (np.log(np.maximum(n, 1) / max_exact) / np.log(MAX_DISTANCE / max_exact)
                         * (N_BUCKETS - max_exact)).astype(np.int32)
    large = np.minimum(large, N_BUCKETS - 1)
    bucket = np.where(n < max_exact, n, large).astype(np.int32)
    first = np.where(j <= i, bucket, -1)
    return np.stack([first, bucket]).astype(np.int32)


def _bias_kernel(idx_ref, rb_ref, o_ref):
    for v in range(2):
        idx = idx_ref[v]
        hits = [idx == b for b in range(N_BUCKETS)]
        for h in range(N_Q_HEADS):
            acc = jnp.full((BLOCK, BLOCK), -jnp.inf, F32)
            for b in range(N_BUCKETS):
                acc = jnp.where(hits[b], rb_ref[b, h] * LOG2E, acc)
            o_ref[v, h] = acc


def _bias_call(rel_bias):
    idx = jnp.asarray(_t5_bucket_table())
    return pl.pallas_call(
        _bias_kernel,
        in_specs=[
            pl.BlockSpec(memory_space=pltpu.VMEM),
            pl.BlockSpec(memory_space=pltpu.SMEM),
        ],
        out_specs=pl.BlockSpec(memory_space=pltpu.VMEM),
        out_shape=jax.ShapeDtypeStruct((2, N_Q_HEADS, BLOCK, BLOCK), F32),
        name="t5_bias",
    )(idx, rel_bias)


def _inproj_norm(x_ref, mod_ref, n1_ref, hb_ref):
    gain = n1_ref[...] * (1.0 + mod_ref[1:2, :])
    hb_ref[...] = (_rms(x_ref[...]) * gain + mod_ref[0:1, :]).astype(BF16)


def _inproj_project(hb_ref, w_ref, wdt_ref, cw_ref, cb_ref, q_ref, kt_ref, v_ref, gz_ref, act_ref, dt_ref, xe_ref):
    tm = hb_ref.shape[0]
    proj = lambda lo, hi: _dot(hb_ref[...], w_ref[:, lo:hi])
    tile = 2 * LANES
    half = tm // 2
    for i, c0 in enumerate(range(0, XBC_WIDTH, tile)):
        cols = slice(c0, c0 + tile)
        carry = xe_ref[:, cols]
        for r0 in range(0, tm, half):
            xbc = _dot(hb_ref[r0:r0 + half, :], w_ref[:, Z_END + c0:Z_END + c0 + tile])
            xe = jnp.concatenate([carry, xbc], axis=0)
            xe = xe.reshape(half // SUBLANES + 1, SUBLANES, tile)
            slab_row = lax.broadcasted_iota(jnp.int32, xe.shape, 1)

            def shift_down(a, n):
                rot = pltpu.roll(a, n, axis=1)
                return jnp.where(slab_row < n, jnp.concatenate([rot[:1], rot[:-1]], axis=0), rot)

            w0, w1, w2, w3 = (0.5 * cw_ref[k:k + 1, cols] for k in range(CONV_K))
            x1 = shift_down(xe, 1)
            taps = shift_down(w0 * x1 + w1 * xe, 2) + (w2 * x1 + w3 * xe)
            conv_half = taps[1:].reshape(half, tile) + 0.5 * cb_ref[:, cols]
            carry = xbc[half - SUBLANES:, :]
            act_ref[r0:r0 + half, cols] = _silu_of_half(conv_half).astype(BF16)
        xe_ref[:, cols] = carry
        if i < Q_END // tile:
            q_ref[:, i * tile:(i + 1) * tile] = proj(i * tile, (i + 1) * tile).astype(BF16)
        else:
            z0 = (i - Q_END // tile) * tile
            gz_ref[:, z0:z0 + tile] = _silu(proj(V_END + z0, V_END + z0 + tile)).astype(BF16)
    kv = proj(Q_END, V_END)
    kt_ref[...] = (kv[:, :KV_WIDTH] * (HEAD_DIM ** -0.5 * LOG2E)).T.astype(BF16)
    v_ref[...] = kv[:, KV_WIDTH:].astype(BF16)
    dt_ref[...] = _dot(hb_ref[...], wdt_ref[...])


def _inproj_kernel(x_ref, mod_ref, n1_ref, w_ref, wdt_ref, cw_ref, cb_ref, q_ref, kt_ref, v_ref, gz_ref, act_ref,
                   dt_ref, hb_ref, xe_ref):
    @pl.when(pl.program_id(1) == 0)
    def _():
        xe_ref[...] = jnp.zeros_like(xe_ref)

    _inproj_norm(x_ref, mod_ref, n1_ref, hb_ref)
    _inproj_project(hb_ref, w_ref, wdt_ref, cw_ref, cb_ref, q_ref, kt_ref, v_ref, gz_ref, act_ref, dt_ref, xe_ref)


def _inproj_call(x, mod, norm1, w_main, w_dt, conv_w, conv_b):
    b, s, _ = x.shape
    tm = INPROJ_TM
    row = lambda width: pl.BlockSpec((None, tm, width), lambda i, j: (i, j, 0))
    const = lambda shape: pl.BlockSpec(shape, lambda i, j: (0,) * len(shape))
    return pl.pallas_call(
        _inproj_kernel,
        grid=(b, s // tm),
        in_specs=[
            row(D_MODEL),
            pl.BlockSpec((None, 6, D_MODEL), lambda i, j: (i, 0, 0)),
            const((1, D_MODEL)),
            const((D_MODEL, XBC_END + SSM_HEADS)),
            const((D_MODEL, LANES)),
            const((CONV_K, XBC_WIDTH)), const((1, XBC_WIDTH)),
        ],
        out_specs=[row(ATTN_WIDTH), pl.BlockSpec((None, KV_WIDTH, tm), lambda i, j: (i, 0, j)), row(KV_WIDTH),
                   row(SSM_WIDTH), row(XBC_WIDTH), row(LANES)],
        out_shape=[
            jax.ShapeDtypeStruct((b, s, ATTN_WIDTH), BF16),
            jax.ShapeDtypeStruct((b, KV_WIDTH, s), BF16),
            jax.ShapeDtypeStruct((b, s, KV_WIDTH), BF16),
            jax.ShapeDtypeStruct((b, s, SSM_WIDTH), BF16),
            jax.ShapeDtypeStruct((b, s, XBC_WIDTH), BF16),
            jax.ShapeDtypeStruct((b, s, LANES), F32),
        ],
        scratch_shapes=[
            pltpu.VMEM((tm, D_MODEL), BF16),
            pltpu.VMEM((SUBLANES, XBC_WIDTH), F32),
        ],
        compiler_params=pltpu.CompilerParams(
            dimension_semantics=("parallel", "arbitrary"), vmem_limit_bytes=VMEM_LIMIT_BYTES),
        name="in_proj",
    )(x, mod, norm1, w_main, w_dt, conv_w, conv_b)


def _stage_keys_values(kt_ref, ktp_ref, v_ref, vp_ref, ka_ref, vv_ref):
    zeros = jnp.zeros((HEAD_DIM, BLOCK + MIX_ROWS), BF16)
    for g in range(N_KV_HEADS):
        rows = slice(g * HEAD_DIM, (g + 1) * HEAD_DIM)
        for par in range(2):
            live = slice(par * HEAD_DIM, (par + 1) * HEAD_DIM)
            dead = slice((1 - par) * HEAD_DIM, (2 - par) * HEAD_DIM)
            ka_ref[g, par, live, 0:BLOCK] = ktp_ref[rows, :]
            ka_ref[g, par, live, BLOCK:] = kt_ref[rows, :]
            ka_ref[g, par, dead, :] = zeros

    def duplicate_halves(v):
        v32 = pltpu.bitcast(v, jnp.uint32)
        swapped = pltpu.roll(v32, HEAD_DIM, axis=1)
        lane = lax.broadcasted_iota(jnp.int32, v32.shape, 1)
        first = lane < HEAD_DIM
        return (pltpu.bitcast(jnp.where(first, v32, swapped), BF16),
                pltpu.bitcast(jnp.where(first, swapped, v32), BF16))

    for src, r0, r1 in ((vp_ref, 0, BLOCK), (v_ref, BLOCK, BLOCK + MIX_ROWS)):
        for g, dup in enumerate(duplicate_halves(src[...])):
            vv_ref[g, r0:r1, 0:LANES] = dup


def _scores(q_ref, ka_ref, j):
    rows = pl.ds(j * BLOCK, BLOCK)
    win = pl.ds(j * BLOCK, 2 * BLOCK)
    out = []
    for h in range(N_Q_HEADS):
        t, par = h // 2, h % 2
        out.append(_dot(q_ref[rows, t * LANES:(t + 1) * LANES], ka_ref[t // 2, par, :, win]))
    return out


def _softmax(scores, bias_ref, sinks_ref, own):
    s = [jnp.where(own, sc[:, BLOCK:], sc[:, :BLOCK]) + bias_ref[h] for h, sc in enumerate(scores)]
    sink = [sinks_ref[h] * LOG2E for h in range(N_Q_HEADS)]
    m = [jnp.maximum(jnp.max(s[h], axis=-1, keepdims=True), sink[h]) for h in range(N_Q_HEADS)]
    p = [jnp.exp2(s[h] - m[h]) for h in range(N_Q_HEADS)]
    zero = jnp.zeros((BLOCK, BLOCK), BF16)
    pb = [ph.astype(BF16) for ph in p]
    pcat = [jnp.concatenate([jnp.where(own, zero, b), jnp.where(own, b, zero)], axis=1) for b in pb]
    sink_term = [jnp.exp2(sink[h] - m[h]) for h in range(N_Q_HEADS)]
    return pcat, sink_term


def _values(pcat, vv_ref, j):
    win = pl.ds(j * BLOCK, 2 * BLOCK)
    return [_dot(pcat[h], vv_ref[h // Q_PER_KV, win, :]) for h in range(N_Q_HEADS)]


def _finish_attention(o2, sink_term, an_ref, o_ref, j):
    rows = pl.ds(j * BLOCK, BLOCK)
    heads = [o2[h][:, :LANES] / (o2[h][:, LANES:] + sink_term[h]) for h in range(N_Q_HEADS)]
    lane = lax.broadcasted_iota(jnp.int32, (BLOCK, LANES), 1)
    tiles = [jnp.where(lane < HEAD_DIM, heads[2 * t], heads[2 * t + 1]) for t in range(N_Q_HEADS // 2)]
    ssq = sum(jnp.sum(tl * tl, axis=-1, keepdims=True) for tl in tiles)
    inv = lax.rsqrt(ssq * (1.0 / ATTN_WIDTH) + EPS)
    for t, tl in enumerate(tiles):
        cols = slice(t * LANES, (t + 1) * LANES)
        o_ref[rows, cols] = (tl * inv * an_ref[:, cols]).astype(BF16)


def _mixer_steps(first_tile, q_ref, kt_ref, ktp_ref, v_ref, vp_ref, gz_ref, act_ref, dt_ref, bias_ref, sinks_ref,
                 dtb_ref, alog_ref, dskip_ref, an_ref, sn_ref, expand_ref, o_ref, state_ref, ka_ref, vv_ref,
                 xbd_ref):
    nch = MIX_CHUNKS
    row_i = lax.broadcasted_iota(jnp.int32, (BLOCK, BLOCK), 0)
    col_i = lax.broadcasted_iota(jnp.int32, (BLOCK, BLOCK), 1)
    own = col_i <= row_i
    bias_of = lambda j: bias_ref.at[0 if (first_tile and j == 0) else 1]
    grp = lambda a, j, g: a[j * CHUNK:(j + 1) * CHUNK, g * GROUP_WIDTH:(g + 1) * GROUP_WIDTH]
    softmax = lambda j, sc: _softmax(sc, bias_of(j), sinks_ref, own)
    finish = lambda j, vals, soft: _finish_attention(vals, soft[1], an_ref, o_ref, j)

    _stage_keys_values(kt_ref, ktp_ref, v_ref, vp_ref, ka_ref, vv_ref)
    dt = jax.nn.softplus(dt_ref[...] + dtb_ref[...])
    dta = dt * (-LOG2E * jnp.exp(alog_ref[...]))
    tri = own.astype(BF16)
    dta_hi, dta_lo = _split2(dta)
    scores = {0: _scores(q_ref, ka_ref, 0), 1: _scores(q_ref, ka_ref, 1)}
    acs = [_dot(tri, dta_hi[j * CHUNK:(j + 1) * CHUNK]) + _dot(tri, dta_lo[j * CHUNK:(j + 1) * CHUNK])
           for j in range(nch)]
    yield

    soft = {0: softmax(0, scores.pop(0))}
    acs_dt_t = [(a - jnp.log2(dt[j * CHUNK:(j + 1) * CHUNK])).T for j, a in enumerate(acs)]
    dec_in = [jnp.exp2(a) for a in acs]
    dec_out = jnp.concatenate([jnp.exp2(a[CHUNK - 1:CHUNK, :] - a) for a in acs], axis=0)
    state_in = (dt * dec_out).astype(BF16)
    decay_in = jnp.concatenate(dec_in, axis=0).astype(BF16)
    decay_tail = [_split2(d[CHUNK - SUBLANES:, :]) for d in dec_in]
    yield

    vals = {0: _values(soft[0][0], vv_ref, 0)}
    expand = expand_ref[...]
    state_in_x = _dot(state_in, expand)
    decay_in_x = _dot(decay_in, expand)
    chunk_decay = [(_dot(hi, expand) + _dot(lo, expand))[SUBLANES - 1:SUBLANES, :] for hi, lo in decay_tail]
    scores[2] = _scores(q_ref, ka_ref, 2)
    soft[1] = softmax(1, scores.pop(1))
    finish(0, vals.pop(0), soft.pop(0))
    yield

    vals[1] = _values(soft[1][0], vv_ref, 1)
    scores[3] = _scores(q_ref, ka_ref, 3)
    xs = act_ref[:, :SSM_WIDTH].astype(F32)
    xdec_b = (xs * state_in_x).astype(BF16)
    lane_head = lax.broadcasted_iota(jnp.int32, (CHUNK, GROUP_WIDTH), 1) // SSM_HEAD_DIM
    for j in range(nch):
        for g in range(SSM_GROUPS):
            xg = act_ref[j * CHUNK:(j + 1) * CHUNK, g * GROUP_WIDTH:(g + 1) * GROUP_WIDTH]
            for r in range(HEADS_PER_GROUP):
                xbd_ref[j, g, r * CHUNK:(r + 1) * CHUNK, :] = jnp.where(lane_head == r, xg, jnp.zeros_like(xg))
    soft[2] = softmax(2, scores.pop(2))
    finish(1, vals.pop(1), soft.pop(1))
    yield

    bc_b = act_ref[:, SSM_WIDTH:]
    bm = [[bc_b[j * CHUNK:(j + 1) * CHUNK, g * D_STATE:(g + 1) * D_STATE] for g in range(SSM_GROUPS)]
          for j in range(nch)]
    cm = [[bc_b[j * CHUNK:(j + 1) * CHUNK, (SSM_GROUPS + g) * D_STATE:(SSM_GROUPS + g + 1) * D_STATE]
           for g in range(SSM_GROUPS)] for j in range(nch)]
    cbm = [[_dot_nt(cm[j][g], bm[j][g]) for g in range(SSM_GROUPS)] for j in range(nch)]
    d_state = [[_dot_tn(bm[j][g], grp(xdec_b, j, g)) for g in range(SSM_GROUPS)] for j in range(nch)]
    vals[2] = _values(soft[2][0], vv_ref, 2)
    soft[3] = softmax(3, scores.pop(3))

    def intra_weights(j):
        per_group = []
        for g in range(SSM_GROUPS):
            w = []
            for r in range(HEADS_PER_GROUP):
                h = g * HEADS_PER_GROUP + r
                seg = acs[j][:, h:h + 1] - acs_dt_t[j][h:h + 1, :]
                w.append(cbm[j][g] * jnp.exp2(jnp.where(own, seg, -jnp.inf)))
            per_group.append((jnp.concatenate(w[:2], axis=1).astype(BF16),
                              jnp.concatenate(w[2:], axis=1).astype(BF16)))
        return per_group

    def intra_product(j, w):
        return [_dot(w[g][0], xbd_ref[j, g, 0:2 * CHUNK, :]) + _dot(w[g][1], xbd_ref[j, g, 2 * CHUNK:, :])
                for g in range(SSM_GROUPS)]

    yield
    w = {0: intra_weights(0)}
    finish(2, vals.pop(2), soft.pop(2))
    yield

    vals[3] = _values(soft[3][0], vv_ref, 3)
    y_diag = [intra_product(0, w.pop(0))]
    w[1] = intra_weights(1)
    yield

    y_diag.append(intra_product(1, w.pop(1)))
    w[2] = intra_weights(2)
    finish(3, vals.pop(3), soft.pop(3))
    yield

    y_diag.append(intra_product(2, w.pop(2)))
    w[3] = intra_weights(3)
    yield

    y_diag.append(intra_product(3, w.pop(3)))
    entering = []
    for g in range(SSM_GROUPS):
        state = state_ref[g]
        per_chunk = []
        for j in range(nch):
            per_chunk.append(state.astype(BF16))
            state = state * chunk_decay[j][:, g * GROUP_WIDTH:(g + 1) * GROUP_WIDTH] + d_state[j][g]
        state_ref[g] = state
        entering.append(per_chunk)
    yield

    y_rows = [jnp.concatenate([y_diag[j][g] + _dot(cm[j][g], entering[g][j]) * grp(decay_in_x, j, g)
                               for g in range(SSM_GROUPS)], axis=-1) for j in range(nch)]
    y = jnp.concatenate(y_rows, axis=0) + dskip_ref[...] * xs
    y = y * gz_ref[...].astype(F32)
    y = jnp.concatenate([_rms(y[:, :GROUP_WIDTH]), _rms(y[:, GROUP_WIDTH:])], axis=-1)
    o_ref[:, ATTN_WIDTH:] = (y * sn_ref[...]).astype(BF16)


FFN_PHASES = D_FF // FFN_FC + D_MODEL // FFN_FC


def _ffn_steps(mix_ref, x_ref, mod_ref, wo_ref, n2_ref, wgu_ref, wd_ref, fn_ref, o_ref, h_ref, a_ref):
    gain2 = n2_ref[...] * (1.0 + mod_ref[4:5, :])
    x1 = x_ref[...] + mod_ref[2:3, :] * _dot(mix_ref[...], wo_ref[...])
    h_ref[...] = (_rms(x1) * gain2 + mod_ref[3:4, :]).astype(BF16)
    o_ref[...] = x1
    yield
    for c0 in range(0, D_FF, FFN_FC):
        hb = h_ref[...]
        g_half = _dot(hb, wgu_ref[:, c0:c0 + FFN_FC])
        u = _dot(hb, wgu_ref[:, D_FF + c0:D_FF + c0 + FFN_FC])
        a_ref[:, c0:c0 + FFN_FC] = (_silu_of_half(g_half) * u).astype(BF16)
        yield
    for c0 in range(0, D_MODEL, FFN_FC):
        cols = slice(c0, c0 + FFN_FC)
        o_ref[:, cols] = o_ref[:, cols] + mod_ref[5:6, cols] * _dot(a_ref[...], wd_ref[:, cols])
        if c0 + FFN_FC == D_MODEL:
            o_ref[...] = _rms(o_ref[...]) * fn_ref[...]
        yield


def _mixer_ffn_kernel(q_ref, kt_ref, ktp_ref, v_ref, vp_ref, gz_ref, act_ref, dt_ref, bias_ref, sinks_ref, dtb_ref,
                      alog_ref, dskip_ref, an_ref, sn_ref, expand_ref, x_ref, mod_ref, wo_ref, n2_ref, wgu_ref,
                      wd_ref, fn_ref, o_ref, state_ref, ka_ref, vv_ref, xbd_ref, mix_ref, h_ref, a_ref):
    s = pl.program_id(1)
    last = pl.num_programs(1) - 1
    mixer = lambda first: _mixer_steps(first, q_ref, kt_ref, ktp_ref, v_ref, vp_ref, gz_ref, act_ref, dt_ref,
                                       bias_ref, sinks_ref, dtb_ref, alog_ref, dskip_ref, an_ref, sn_ref,
                                       expand_ref, mix_ref, state_ref, ka_ref, vv_ref, xbd_ref)
    ffn = lambda: _ffn_steps(mix_ref, x_ref, mod_ref, wo_ref, n2_ref, wgu_ref, wd_ref, fn_ref, o_ref, h_ref, a_ref)

    @pl.when(s == 0)
    def _():
        state_ref[...] = jnp.zeros_like(state_ref)
        vv_ref[:, :, LANES:] = jnp.ones((N_KV_HEADS, BLOCK + MIX_ROWS, LANES), BF16)
        for _ in mixer(True):
            pass

    @pl.when((s > 0) & (s < last))
    def _():
        ffn_steps, issued = ffn(), 0
        next(ffn_steps)
        for k, _ in enumerate(mixer(False)):
            while issued < ((k + 1) * (FFN_PHASES - 1)) // MIXER_YIELDS:
                next(ffn_steps)
                issued += 1
        assert k == MIXER_YIELDS - 1
        for _ in ffn_steps:
            pass

    @pl.when(s == last)
    def _():
        for _ in ffn():
            pass


def _mixer_ffn_call(q, kt, v, gz, act, dt, bias, sinks, dt_bias, a_log, d_skip_x, attn_norm, ssm_norm, expand,
                    x, mod, w_o, norm2, w_gu, w_down, final_norm):
    b, s, _ = q.shape
    rows = MIX_ROWS
    n_tiles = s // rows
    mix_tile = lambda i, c: jnp.minimum(c, n_tiles - 1)
    ffn_tile = lambda i, c: jnp.maximum(c - 1, 0)
    prev_blk = lambda i, c: jnp.maximum(mix_tile(i, c) * MIX_CHUNKS - 1, 0)
    mrow = lambda width: pl.BlockSpec((None, rows, width), lambda i, c: (i, mix_tile(i, c), 0))
    frow = lambda: pl.BlockSpec((None, rows, D_MODEL), lambda i, c: (i, ffn_tile(i, c), 0))
    const = lambda shape: pl.BlockSpec(shape, lambda i, c: (0,) * len(shape))
    resident = lambda shape: pl.BlockSpec(shape, lambda i, c: (0,) * len(shape), pipeline_mode=pl.Buffered(1))
    return pl.pallas_call(
        _mixer_ffn_kernel,
        grid=(b, n_tiles + 1),
        in_specs=[
            mrow(ATTN_WIDTH),
            pl.BlockSpec((None, KV_WIDTH, rows), lambda i, c: (i, 0, mix_tile(i, c))),
            pl.BlockSpec((None, KV_WIDTH, BLOCK), lambda i, c: (i, 0, prev_blk(i, c))),
            mrow(KV_WIDTH),
            pl.BlockSpec((None, BLOCK, KV_WIDTH), lambda i, c: (i, prev_blk(i, c), 0)),
            mrow(SSM_WIDTH), mrow(XBC_WIDTH), mrow(LANES),
            const((2, N_Q_HEADS, BLOCK, BLOCK)),
            pl.BlockSpec(memory_space=pltpu.SMEM),
            const((1, LANES)), const((1, LANES)),
            const((1, SSM_WIDTH)), const((1, ATTN_WIDTH)), const((1, SSM_WIDTH)),
            const((LANES, SSM_WIDTH)),
            frow(),
            pl.BlockSpec((None, 6, D_MODEL), lambda i, c: (i, 0, 0)),
            resident((D_MODEL, D_MODEL)), const((1, D_MODEL)),
            resident((D_MODEL, 2 * D_FF)), resident((D_FF, D_MODEL)), const((1, D_MODEL)),
        ],
        out_specs=frow(),
        out_shape=jax.ShapeDtypeStruct((b, s, D_MODEL), F32),
        scratch_shapes=[
            pltpu.VMEM((SSM_GROUPS, D_STATE, GROUP_WIDTH), F32),
            pltpu.VMEM((N_KV_HEADS, 2, KV_WIDTH, BLOCK + rows), BF16),
            pltpu.VMEM((N_KV_HEADS, BLOCK + rows, 2 * LANES), BF16),
            pltpu.VMEM((MIX_CHUNKS, SSM_GROUPS, HEADS_PER_GROUP * CHUNK, GROUP_WIDTH), BF16),
            pltpu.VMEM((rows, ATTN_WIDTH + SSM_WIDTH), BF16),
            pltpu.VMEM((rows, D_MODEL), BF16),
            pltpu.VMEM((rows, D_FF), BF16),
        ],
        compiler_params=pltpu.CompilerParams(
            dimension_semantics=("parallel", "arbitrary"), vmem_limit_bytes=VMEM_LIMIT_BYTES),
        name="mixer_ffn",
    )(q, kt, kt, v, v, gz, act, dt, bias, sinks, dt_bias, a_log, d_skip_x, attn_norm, ssm_norm, expand,
      x, mod, w_o, norm2, w_gu, w_down, final_norm)


def _head_expand_matrix():
    e = np.zeros((LANES, SSM_WIDTH), np.float32)
    for h in range(SSM_HEADS):
        e[h, h * SSM_HEAD_DIM:(h + 1) * SSM_HEAD_DIM] = 1.0
    return e


def kernel(x, c, ada_w, ada_b, norm1, w_in, conv_w, conv_b, dt_bias, A_log, D_skip, sinks, attn_out_norm,
           ssm_out_norm, w_o, norm2, w_gate_up, w_down, rel_bias, final_norm):
    assert ada_w.shape[0] == 1, "the final RMSNorm is fused into the (single) layer's FFN call"
    b = x.shape[0]
    bias = _bias_call(rel_bias)
    expand = jnp.asarray(_head_expand_matrix(), BF16)
    pad_heads = lambda v: jnp.pad(v, (0, LANES - SSM_HEADS))[None, :]
    c_pad = jnp.pad(c, ((0, SUBLANES - b), (0, 0)))
    mod = _ada_call(c_pad, ada_w[0], ada_b[0][None, :])[:b].reshape(b, 6, D_MODEL)
    w_main = w_in.astype(BF16)[0]
    w_dt = jnp.pad(w_main[:, XBC_END:], ((0, 0), (0, LANES - SSM_HEADS)))
    q, kt, v, gz, act, dt = _inproj_call(x, mod, norm1[0][None, :], w_main, w_dt, conv_w[0], conv_b[0][None, :])
    gate_half = jnp.where(jnp.arange(2 * D_FF) < D_FF, 0.5, 1.0).astype(F32)
    w_gu = (w_gate_up[0] * gate_half).astype(BF16)
    return _mixer_ffn_call(
        q, kt, v, gz, act, dt, bias, sinks[0], pad_heads(dt_bias[0]), pad_heads(A_log[0]),
        jnp.repeat(D_skip[0], SSM_HEAD_DIM)[None, :], attn_out_norm[0][None, :], ssm_out_norm[0][None, :], expand,
        x, mod, w_o[0].astype(BF16), norm2[0][None, :], w_gu, w_down[0].astype(BF16), final_norm[None, :])
```

```python
import numpy as np
import jax
import jax.numpy as jnp
from jax import lax
from jax.experimental import pallas as pl
from jax.experimental.pallas import tpu as pltpu

F32 = jnp.float32
BF16 = jnp.bfloat16

D_MODEL = 1024
HEAD_DIM = 64
N_Q_HEADS = 8
N_KV_HEADS = 2
Q_PER_KV = N_Q_HEADS // N_KV_HEADS
ATTN_WIDTH = N_Q_HEADS * HEAD_DIM
KV_WIDTH = N_KV_HEADS * HEAD_DIM
WINDOW = 128
BLOCK = 128
N_BUCKETS = 32
MAX_DISTANCE = 128

SSM_HEAD_DIM = 64
SSM_HEADS = 8
SSM_GROUPS = 2
HEADS_PER_GROUP = SSM_HEADS // SSM_GROUPS
SSM_WIDTH = SSM_HEADS * SSM_HEAD_DIM
GROUP_WIDTH = SSM_WIDTH // SSM_GROUPS
D_STATE = 128
CONV_K = 4
CHUNK = 128
XBC_WIDTH = SSM_WIDTH + 2 * SSM_GROUPS * D_STATE
D_FF = 2816
EPS = 1e-6
LOG2E = float(np.log2(np.e))

LANES = 128
SUBLANES = 8
VMEM_LIMIT_BYTES = 56 * 1024 * 1024

ADA_TN = 1536
INPROJ_TM = 1024
FFN_FC = 256
MIX_CHUNKS = 4
MIXER_YIELDS = 10
MIX_ROWS = MIX_CHUNKS * CHUNK

Q_END = ATTN_WIDTH
K_END = Q_END + KV_WIDTH
V_END = K_END + KV_WIDTH
Z_END = V_END + SSM_WIDTH
XBC_END = Z_END + XBC_WIDTH

assert BLOCK == CHUNK == LANES and KV_WIDTH == LANES and 2 * HEAD_DIM == LANES


def _silu_of_half(h):
    return h * jnp.tanh(h) + h


def _silu(x):
    return _silu_of_half(0.5 * x)


def _rms(x):
    return x * lax.rsqrt(jnp.mean(x * x, axis=-1, keepdims=True) + EPS)


def _split2(x):
    hi = x.astype(BF16)
    lo = (x - hi.astype(F32)).astype(BF16)
    return hi, lo


def _dot(a, b):
    return jnp.dot(a, b, preferred_element_type=F32)


def _dot_nt(a, b):
    return lax.dot_general(a, b, (((1,), (1,)), ((), ())), preferred_element_type=F32)


def _dot_tn(a, b):
    return lax.dot_general(a, b, (((0,), (0,)), ((), ())), preferred_element_type=F32)


def _ada_kernel(c_ref, w_ref, b_ref, o_ref):
    cond = _silu(c_ref[...])
    o_ref[...] = _dot(cond.astype(BF16), w_ref[...].astype(BF16)) + b_ref[...]


def _ada_call(c_pad, ada_w, ada_b):
    n = ada_w.shape[1]
    return pl.pallas_call(
        _ada_kernel,
        grid=(n // ADA_TN,),
        in_specs=[
            pl.BlockSpec((SUBLANES, D_MODEL), lambda j: (0, 0)),
            pl.BlockSpec((D_MODEL, ADA_TN), lambda j: (0, j)),
            pl.BlockSpec((1, ADA_TN), lambda j: (0, j)),
        ],
        out_specs=pl.BlockSpec((SUBLANES, ADA_TN), lambda j: (0, j)),
        out_shape=jax.ShapeDtypeStruct((SUBLANES, n), F32),
        compiler_params=pltpu.CompilerParams(dimension_semantics=("parallel",)),
        name="ada_mod",
    )(c_pad, ada_w, ada_b)


def _t5_bucket_table():
    i = np.arange(BLOCK)[:, None]
    j = np.arange(BLOCK)[None, :]
    n = np.where(j <= i, i - j, i + BLOCK - j)
    assert WINDOW == BLOCK and n.min() >= 0 and n.max() < WINDOW
    max_exact = N_BUCKETS // 2
    large = max_exact + (np.log(np.maximum(n, 1) / max_exact) / np.log(MAX_DISTANCE / max_exact)
                         * (N_BUCKETS - max_exact)).astype(np.int32)
    large = np.minimum(large, N_BUCKETS - 1)
    bucket = np.where(n < max_exact, n, large).astype(np.int32)
    first = np.where(j <= i, bucket, -1)
    return np.stack([first, bucket]).astype(np.int32)


def _bias_kernel(idx_ref, rb_ref, o_ref):
    for v in range(2):
        idx = idx_ref[v]
        hits = [idx == b for b in range(N_BUCKETS)]
        for h in range(N_Q_HEADS):
            acc = jnp.full((BLOCK, BLOCK), -jnp.inf, F32)
            for b in range(N_BUCKETS):
                acc = jnp.where(hits[b], rb_ref[b, h] * LOG2E, acc)
            o_ref[v, h] = acc


def _bias_call(rel_bias):
    idx = jnp.asarray(_t5_bucket_table())
    return pl.pallas_call(
        _bias_kernel,
        in_specs=[
            pl.BlockSpec(memory_space=pltpu.VMEM),
            pl.BlockSpec(memory_space=pltpu.SMEM),
        ],
        out_specs=pl.BlockSpec(memory_space=pltpu.VMEM),
        out_shape=jax.ShapeDtypeStruct((2, N_Q_HEADS, BLOCK, BLOCK), F32),
        name="t5_bias",
    )(idx, rel_bias)


def _inproj_norm(x_ref, mod_ref, n1_ref, hb_ref):
    gain = n1_ref[...] * (1.0 + mod_ref[1:2, :])
    hb_ref[...] = (_rms(x_ref[...]) * gain + mod_ref[0:1, :]).astype(BF16)


def _inproj_project(hb_ref, w_ref, wdt_ref, cw_ref, cb_ref, q_ref, kt_ref, v_ref, gz_ref, act_ref, dt_ref, xe_ref):
    tm = hb_ref.shape[0]
    proj = lambda lo, hi: _dot(hb_ref[...], w_ref[:, lo:hi])
    tile = 2 * LANES
    half = tm // 2
    for i, c0 in enumerate(range(0, XBC_WIDTH, tile)):
        cols = slice(c0, c0 + tile)
        carry = xe_ref[:, cols]
        for r0 in range(0, tm, half):
            xbc = _dot(hb_ref[r0:r0 + half, :], w_ref[:, Z_END + c0:Z_END + c0 + tile])
            xe = jnp.concatenate([carry, xbc], axis=0)
            xe = xe.reshape(half // SUBLANES + 1, SUBLANES, tile)
            slab_row = lax.broadcasted_iota(jnp.int32, xe.shape, 1)

            def shift_down(a, n):
                rot = pltpu.roll(a, n, axis=1)
                return jnp.where(slab_row < n, jnp.concatenate([rot[:1], rot[:-1]], axis=0), rot)

            w0, w1, w2, w3 = (0.5 * cw_ref[k:k + 1, cols] for k in range(CONV_K))
            x1 = shift_down(xe, 1)
            taps = shift_down(w0 * x1 + w1 * xe, 2) + (w2 * x1 + w3 * xe)
            conv_half = taps[1:].reshape(half, tile) + 0.5 * cb_ref[:, cols]
            carry = xbc[half - SUBLANES:, :]
            act_ref[r0:r0 + half, cols] = _silu_of_half(conv_half).astype(BF16)
        xe_ref[:, cols] = carry
        if i < Q_END // tile:
            q_ref[:, i * tile:(i + 1) * tile] = proj(i * tile, (i + 1) * tile).astype(BF16)
        else:
            z0 = (i - Q_END // tile) * tile
            gz_ref[:, z0:z0 + tile] = _silu(proj(V_END + z0, V_END + z0 + tile)).astype(BF16)
    kv = proj(Q_END, V_END)
    kt_ref[...] = (kv[:, :KV_WIDTH] * (HEAD_DIM ** -0.5 * LOG2E)).T.astype(BF16)
    v_ref[...] = kv[:, KV_WIDTH:].astype(BF16)
    dt_ref[...] = _dot(hb_ref[...], wdt_ref[...])


def _inproj_kernel(x_ref, mod_ref, n1_ref, w_ref, wdt_ref, cw_ref, cb_ref, q_ref, kt_ref, v_ref, gz_ref, act_ref,
                   dt_ref, hb_ref, xe_ref):
    @pl.when(pl.program_id(1) == 0)
    def _():
        xe_ref[...] = jnp.zeros_like(xe_ref)

    _inproj_norm(x_ref, mod_ref, n1_ref, hb_ref)
    _inproj_project(hb_ref, w_ref, wdt_ref, cw_ref, cb_ref, q_ref, kt_ref, v_ref, gz_ref, act_ref, dt_ref, xe_ref)


def _inproj_call(x, mod, norm1, w_main, w_dt, conv_w, conv_b):
    b, s, _ = x.shape
    tm = INPROJ_TM
    row = lambda width: pl.BlockSpec((None, tm, width), lambda i, j: (i, j, 0))
    const = lambda shape: pl.BlockSpec(shape, lambda i, j: (0,) * len(shape))
    return pl.pallas_call(
        _inproj_kernel,
        grid=(b, s // tm),
        in_specs=[
            row(D_MODEL),
            pl.BlockSpec((None, 6, D_MODEL), lambda i, j: (i, 0, 0)),
            const((1, D_MODEL)),
            const((D_MODEL, XBC_END + SSM_HEADS)),
            const((D_MODEL, LANES)),
            const((CONV_K, XBC_WIDTH)), const((1, XBC_WIDTH)),
        ],
        out_specs=[row(ATTN_WIDTH), pl.BlockSpec((None, KV_WIDTH, tm), lambda i, j: (i, 0, j)), row(KV_WIDTH),
                   row(SSM_WIDTH), row(XBC_WIDTH), row(LANES)],
        out_shape=[
            jax.ShapeDtypeStruct((b, s, ATTN_WIDTH), BF16),
            jax.ShapeDtypeStruct((b, KV_WIDTH, s), BF16),
            jax.ShapeDtypeStruct((b, s, KV_WIDTH), BF16),
            jax.ShapeDtypeStruct((b, s, SSM_WIDTH), BF16),
            jax.ShapeDtypeStruct((b, s, XBC_WIDTH), BF16),
            jax.ShapeDtypeStruct((b, s, LANES), F32),
        ],
        scratch_shapes=[
            pltpu.VMEM((tm, D_MODEL), BF16),
            pltpu.VMEM((SUBLANES, XBC_WIDTH), F32),
        ],
        compiler_params=pltpu.CompilerParams(
            dimension_semantics=("parallel", "arbitrary"), vmem_limit_bytes=VMEM_LIMIT_BYTES),
        name="in_proj",
    )(x, mod, norm1, w_main, w_dt, conv_w, conv_b)


def _stage_keys_values(kt_ref, ktp_ref, v_ref, vp_ref, ka_ref, vv_ref):
    zeros = jnp.zeros((HEAD_DIM, BLOCK + MIX_ROWS), BF16)
    for g in range(N_KV_HEADS):
        rows = slice(g * HEAD_DIM, (g + 1) * HEAD_DIM)
        for par in range(2):
            live = slice(par * HEAD_DIM, (par + 1) * HEAD_DIM)
            dead = slice((1 - par) * HEAD_DIM, (2 - par) * HEAD_DIM)
            ka_ref[g, par, live, 0:BLOCK] = ktp_ref[rows, :]
            ka_ref[g, par, live, BLOCK:] = kt_ref[rows, :]
            ka_ref[g, par, dead, :] = zeros

    def duplicate_halves(v):
        v32 = pltpu.bitcast(v, jnp.uint32)
        swapped = pltpu.roll(v32, HEAD_DIM, axis=1)
        lane = lax.broadcasted_iota(jnp.int32, v32.shape, 1)
        first = lane < HEAD_DIM
        return (pltpu.bitcast(jnp.where(first, v32, swapped), BF16),
                pltpu.bitcast(jnp.where(first, swapped, v32), BF16))

    for src, r0, r1 in ((vp_ref, 0, BLOCK), (v_ref, BLOCK, BLOCK + MIX_ROWS)):
        for g, dup in enumerate(duplicate_halves(src[...])):
            vv_ref[g, r0:r1, 0:LANES] = dup


def _scores(q_ref, ka_ref, j):
    rows = pl.ds(j * BLOCK, BLOCK)
    win = pl.ds(j * BLOCK, 2 * BLOCK)
    out = []
    for h in range(N_Q_HEADS):
        t, par = h // 2, h % 2
        out.append(_dot(q_ref[rows, t * LANES:(t + 1) * LANES], ka_ref[t // 2, par, :, win]))
    return out


def _softmax(scores, bias_ref, sinks_ref, own):
    s = [jnp.where(own, sc[:, BLOCK:], sc[:, :BLOCK]) + bias_ref[h] for h, sc in enumerate(scores)]
    sink = [sinks_ref[h] * LOG2E for h in range(N_Q_HEADS)]
    m = [jnp.maximum(jnp.max(s[h], axis=-1, keepdims=True), sink[h]) for h in range(N_Q_HEADS)]
    p = [jnp.exp2(s[h] - m[h]) for h in range(N_Q_HEADS)]
    zero = jnp.zeros((BLOCK, BLOCK), BF16)
    pb = [ph.astype(BF16) for ph in p]
    pcat = [jnp.concatenate([jnp.where(own, zero, b), jnp.where(own, b, zero)], axis=1) for b in pb]
    sink_term = [jnp.exp2(sink[h] - m[h]) for h in range(N_Q_HEADS)]
    return pcat, sink_term


def _values(pcat, vv_ref, j):
    win = pl.ds(j * BLOCK, 2 * BLOCK)
    return [_dot(pcat[h], vv_ref[h // Q_PER_KV, win, :]) for h in range(N_Q_HEADS)]


def _finish_attention(o2, sink_term, an_ref, o_ref, j):
    rows = pl.ds(j * BLOCK, BLOCK)
    heads = [o2[h][:, :LANES] / (o2[h][:, LANES:] + sink_term[h]) for h in range(N_Q_HEADS)]
    lane = lax.broadcasted_iota(jnp.int32, (BLOCK, LANES), 1)
    tiles = [jnp.where(lane < HEAD_DIM, heads[2 * t], heads[2 * t + 1]) for t in range(N_Q_HEADS // 2)]
    ssq = sum(jnp.sum(tl * tl, axis=-1, keepdims=True) for tl in tiles)
    inv = lax.rsqrt(ssq * (1.0 / ATTN_WIDTH) + EPS)
    for t, tl in enumerate(tiles):
        cols = slice(t * LANES, (t + 1) * LANES)
        o_ref[rows, cols] = (tl * inv * an_ref[:, cols]).astype(BF16)


def _mixer_steps(first_tile, q_ref, kt_ref, ktp_ref, v_ref, vp_ref, gz_ref, act_ref, dt_ref, bias_ref, sinks_ref,
                 dtb_ref, alog_ref, dskip_ref, an_ref, sn_ref, expand_ref, o_ref, state_ref, ka_ref, vv_ref,
                 xbd_ref):
    nch = MIX_CHUNKS
    row_i = lax.broadcasted_iota(jnp.int32, (BLOCK, BLOCK), 0)
    col_i = lax.broadcasted_iota(jnp.int32, (BLOCK, BLOCK), 1)
    own = col_i <= row_i
    bias_of = lambda j: bias_ref.at[0 if (first_tile and j == 0) else 1]
    grp = lambda a, j, g: a[j * CHUNK:(j + 1) * CHUNK, g * GROUP_WIDTH:(g + 1) * GROUP_WIDTH]
    softmax = lambda j, sc: _softmax(sc, bias_of(j), sinks_ref, own)
    finish = lambda j, vals, soft: _finish_attention(vals, soft[1], an_ref, o_ref, j)

    _stage_keys_values(kt_ref, ktp_ref, v_ref, vp_ref, ka_ref, vv_ref)
    dt = jax.nn.softplus(dt_ref[...] + dtb_ref[...])
    dta = dt * (-LOG2E * jnp.exp(alog_ref[...]))
    tri = own.astype(BF16)
    dta_hi, dta_lo = _split2(dta)
    scores = {0: _scores(q_ref, ka_ref, 0), 1: _scores(q_ref, ka_ref, 1)}
    acs = [_dot(tri, dta_hi[j * CHUNK:(j + 1) * CHUNK]) + _dot(tri, dta_lo[j * CHUNK:(j + 1) * CHUNK])
           for j in range(nch)]
    yield

    soft = {0: softmax(0, scores.pop(0))}
    acs_dt_t = [(a - jnp.log2(dt[j * CHUNK:(j + 1) * CHUNK])).T for j, a in enumerate(acs)]
    dec_in = [jnp.exp2(a) for a in acs]
    dec_out = jnp.concatenate([jnp.exp2(a[CHUNK - 1:CHUNK, :] - a) for a in acs], axis=0)
    state_in = (dt * dec_out).astype(BF16)
    decay_in = jnp.concatenate(dec_in, axis=0).astype(BF16)
    decay_tail = [_split2(d[CHUNK - SUBLANES:, :]) for d in dec_in]
    yield

    vals = {0: _values(soft[0][0], vv_ref, 0)}
    expand = expand_ref[...]
    state_in_x = _dot(state_in, expand)
    decay_in_x = _dot(decay_in, expand)
    chunk_decay = [(_dot(hi, expand) + _dot(lo, expand))[SUBLANES - 1:SUBLANES, :] for hi, lo in decay_tail]
    scores[2] = _scores(q_ref, ka_ref, 2)
    soft[1] = softmax(1, scores.pop(1))
    finish(0, vals.pop(0), soft.pop(0))
    yield

    vals[1] = _values(soft[1][0], vv_ref, 1)
    scores[3] = _scores(q_ref, ka_ref, 3)
    xs = act_ref[:, :SSM_WIDTH].astype(F32)
    xdec_b = (xs * state_in_x).astype(BF16)
    lane_head = lax.broadcasted_iota(jnp.int32, (CHUNK, GROUP_WIDTH), 1) // SSM_HEAD_DIM
    for j in range(nch):
        for g in range(SSM_GROUPS):
            xg = act_ref[j * CHUNK:(j + 1) * CHUNK, g * GROUP_WIDTH:(g + 1) * GROUP_WIDTH]
            for r in range(HEADS_PER_GROUP):
                xbd_ref[j, g, r * CHUNK:(r + 1) * CHUNK, :] = jnp.where(lane_head == r, xg, jnp.zeros_like(xg))
    soft[2] = softmax(2, scores.pop(2))
    finish(1, vals.pop(1), soft.pop(1))
    yield

    bc_b = act_ref[:, SSM_WIDTH:]
    bm = [[bc_b[j * CHUNK:(j + 1) * CHUNK, g * D_STATE:(g + 1) * D_STATE] for g in range(SSM_GROUPS)]
          for j in range(nch)]
    cm = [[bc_b[j * CHUNK:(j + 1) * CHUNK, (SSM_GROUPS + g) * D_STATE:(SSM_GROUPS + g + 1) * D_STATE]
           for g in range(SSM_GROUPS)] for j in range(nch)]
    cbm = [[_dot_nt(cm[j][g], bm[j][g]) for g in range(SSM_GROUPS)] for j in range(nch)]
    d_state = [[_dot_tn(bm[j][g], grp(xdec_b, j, g)) for g in range(SSM_GROUPS)] for j in range(nch)]
    vals[2] = _values(soft[2][0], vv_ref, 2)
    soft[3] = softmax(3, scores.pop(3))

    def intra_weights(j):
        per_group = []
        for g in range(SSM_GROUPS):
            w = []
            for r in range(HEADS_PER_GROUP):
                h = g * HEADS_PER_GROUP + r
                seg = acs[j][:, h:h + 1] - acs_dt_t[j][h:h + 1, :]
                w.append(cbm[j][g] * jnp.exp2(jnp.where(own, seg, -jnp.inf)))
            per_group.append((jnp.concatenate(w[:2], axis=1).astype(BF16),
                              jnp.concatenate(w[2:], axis=1).astype(BF16)))
        return per_group

    def intra_product(j, w):
        return [_dot(w[g][0], xbd_ref[j, g, 0:2 * CHUNK, :]) + _dot(w[g][1], xbd_ref[j, g, 2 * CHUNK:, :])
                for g in range(SSM_GROUPS)]

    yield
    w = {0: intra_weights(0)}
    finish(2, vals.pop(2), soft.pop(2))
    yield

    vals[3] = _values(soft[3][0], vv_ref, 3)
    y_diag = [intra_product(0, w.pop(0))]
    w[1] = intra_weights(1)
    yield

    y_diag.append(intra_product(1, w.pop(1)))
    w[2] = intra_weights(2)
    finish(3, vals.pop(3), soft.pop(3))
    yield

    y_diag.append(intra_product(2, w.pop(2)))
    w[3] = intra_weights(3)
    yield

    y_diag.append(intra_product(3, w.pop(3)))
    entering = []
    for g in range(SSM_GROUPS):
        state = state_ref[g]
        per_chunk = []
        for j in range(nch):
            per_chunk.append(state.astype(BF16))
            state = state * chunk_decay[j][:, g * GROUP_WIDTH:(g + 1) * GROUP_WIDTH] + d_state[j][g]
        state_ref[g] = state
        entering.append(per_chunk)
    yield

    y_rows = [jnp.concatenate([y_diag[j][g] + _dot(cm[j][g], entering[g][j]) * grp(decay_in_x, j, g)
                               for g in range(SSM_GROUPS)], axis=-1) for j in range(nch)]
    y = jnp.concatenate(y_rows, axis=0) + dskip_ref[...] * xs
    y = y * gz_ref[...].astype(F32)
    y = jnp.concatenate([_rms(y[:, :GROUP_WIDTH]), _rms(y[:, GROUP_WIDTH:])], axis=-1)
    o_ref[:, ATTN_WIDTH:] = (y * sn_ref[...]).astype(BF16)


FFN_PHASES = D_FF // FFN_FC + D_MODEL // FFN_FC


def _ffn_steps(mix_ref, x_ref, mod_ref, wo_ref, n2_ref, wgu_ref, wd_ref, fn_ref, o_ref, h_ref, a_ref):
    gain2 = n2_ref[...] * (1.0 + mod_ref[4:5, :])
    x1 = x_ref[...] + mod_ref[2:3, :] * _dot(mix_ref[...], wo_ref[...])
    h_ref[...] = (_rms(x1) * gain2 + mod_ref[3:4, :]).astype(BF16)
    o_ref[...] = x1
    yield
    for c0 in range(0, D_FF, FFN_FC):
        hb = h_ref[...]
        g = _dot(hb, wgu_ref[:, c0:c0 + FFN_FC])
        u = _dot(hb, wgu_ref[:, D_FF + c0:D_FF + c0 + FFN_FC])
        a_ref[:, c0:c0 + FFN_FC] = (_silu(g) * u).astype(BF16)
        yield
    for c0 in range(0, D_MODEL, FFN_FC):
        cols = slice(c0, c0 + FFN_FC)
        o_ref[:, cols] = o_ref[:, cols] + mod_ref[5:6, cols] * _dot(a_ref[...], wd_ref[:, cols])
        if c0 + FFN_FC == D_MODEL:
            o_ref[...] = _rms(o_ref[...]) * fn_ref[...]
        yield


def _mixer_ffn_kernel(q_ref, kt_ref, ktp_ref, v_ref, vp_ref, gz_ref, act_ref, dt_ref, bias_ref, sinks_ref, dtb_ref,
                      alog_ref, dskip_ref, an_ref, sn_ref, expand_ref, x_ref, mod_ref, wo_ref, n2_ref, wgu_ref,
                      wd_ref, fn_ref, o_ref, state_ref, ka_ref, vv_ref, xbd_ref, mix_ref, h_ref, a_ref):
    s = pl.program_id(1)
    last = pl.num_programs(1) - 1
    mixer = lambda first: _mixer_steps(first, q_ref, kt_ref, ktp_ref, v_ref, vp_ref, gz_ref, act_ref, dt_ref,
                                       bias_ref, sinks_ref, dtb_ref, alog_ref, dskip_ref, an_ref, sn_ref,
                                       expand_ref, mix_ref, state_ref, ka_ref, vv_ref, xbd_ref)
    ffn = lambda: _ffn_steps(mix_ref, x_ref, mod_ref, wo_ref, n2_ref, wgu_ref, wd_ref, fn_ref, o_ref, h_ref, a_ref)

    @pl.when(s == 0)
    def _():
        state_ref[...] = jnp.zeros_like(state_ref)
        vv_ref[:, :, LANES:] = jnp.ones((N_KV_HEADS, BLOCK + MIX_ROWS, LANES), BF16)
        for _ in mixer(True):
            pass

    @pl.when((s > 0) & (s < last))
    def _():
        ffn_steps, issued = ffn(), 0
        next(ffn_steps)
        for k, _ in enumerate(mixer(False)):
            while issued < ((k + 1) * (FFN_PHASES - 1)) // MIXER_YIELDS:
                next(ffn_steps)
                issued += 1
        assert k == MIXER_YIELDS - 1
        for _ in ffn_steps:
            pass

    @pl.when(s == last)
    def _():
        for _ in ffn():
            pass


def _mixer_ffn_call(q, kt, v, gz, act, dt, bias, sinks, dt_bias, a_log, d_skip_x, attn_norm, ssm_norm, expand,
                    x, mod, w_o, norm2, w_gu, w_down, final_norm):
    b, s, _ = q.shape
    rows = MIX_ROWS
    n_tiles = s // rows
    mix_tile = lambda i, c: jnp.minimum(c, n_tiles - 1)
    ffn_tile = lambda i, c: jnp.maximum(c - 1, 0)
    prev_blk = lambda i, c: jnp.maximum(mix_tile(i, c) * MIX_CHUNKS - 1, 0)
    mrow = lambda width: pl.BlockSpec((None, rows, width), lambda i, c: (i, mix_tile(i, c), 0))
    frow = lambda: pl.BlockSpec((None, rows, D_MODEL), lambda i, c: (i, ffn_tile(i, c), 0))
    const = lambda shape: pl.BlockSpec(shape, lambda i, c: (0,) * len(shape))
    resident = lambda shape: pl.BlockSpec(shape, lambda i, c: (0,) * len(shape), pipeline_mode=pl.Buffered(1))
    return pl.pallas_call(
        _mixer_ffn_kernel,
        grid=(b, n_tiles + 1),
        in_specs=[
            mrow(ATTN_WIDTH),
            pl.BlockSpec((None, KV_WIDTH, rows), lambda i, c: (i, 0, mix_tile(i, c))),
            pl.BlockSpec((None, KV_WIDTH, BLOCK), lambda i, c: (i, 0, prev_blk(i, c))),
            mrow(KV_WIDTH),
            pl.BlockSpec((None, BLOCK, KV_WIDTH), lambda i, c: (i, prev_blk(i, c), 0)),
            mrow(SSM_WIDTH), mrow(XBC_WIDTH), mrow(LANES),
            const((2, N_Q_HEADS, BLOCK, BLOCK)),
            pl.BlockSpec(memory_space=pltpu.SMEM),
            const((1, LANES)), const((1, LANES)),
            const((1, SSM_WIDTH)), const((1, ATTN_WIDTH)), const((1, SSM_WIDTH)),
            const((LANES, SSM_WIDTH)),
            frow(),
            pl.BlockSpec((None, 6, D_MODEL), lambda i, c: (i, 0, 0)),
            resident((D_MODEL, D_MODEL)), const((1, D_MODEL)),
            resident((D_MODEL, 2 * D_FF)), resident((D_FF, D_MODEL)), const((1, D_MODEL)),
        ],
        out_specs=frow(),
        out_shape=jax.ShapeDtypeStruct((b, s, D_MODEL), F32),
        scratch_shapes=[
            pltpu.VMEM((SSM_GROUPS, D_STATE, GROUP_WIDTH), F32),
            pltpu.VMEM((N_KV_HEADS, 2, KV_WIDTH, BLOCK + rows), BF16),
            pltpu.VMEM((N_KV_HEADS, BLOCK + rows, 2 * LANES), BF16),
            pltpu.VMEM((MIX_CHUNKS, SSM_GROUPS, HEADS_PER_GROUP * CHUNK, GROUP_WIDTH), BF16),
            pltpu.VMEM((rows, ATTN_WIDTH + SSM_WIDTH), BF16),
            pltpu.VMEM((rows, D_MODEL), BF16),
            pltpu.VMEM((rows, D_FF), BF16),
        ],
        compiler_params=pltpu.CompilerParams(
            dimension_semantics=("parallel", "arbitrary"), vmem_limit_bytes=VMEM_LIMIT_BYTES),
        name="mixer_ffn",
    )(q, kt, kt, v, v, gz, act, dt, bias, sinks, dt_bias, a_log, d_skip_x, attn_norm, ssm_norm, expand,
      x, mod, w_o, norm2, w_gu, w_down, final_norm)


def _head_expand_matrix():
    e = np.zeros((LANES, SSM_WIDTH), np.float32)
    for h in range(SSM_HEADS):
        e[h, h * SSM_HEAD_DIM:(h + 1) * SSM_HEAD_DIM] = 1.0
    return e


def kernel(x, c, ada_w, ada_b, norm1, w_in, conv_w, conv_b, dt_bias, A_log, D_skip, sinks, attn_out_norm,
           ssm_out_norm, w_o, norm2, w_gate_up, w_down, rel_bias, final_norm):
    assert ada_w.shape[0] == 1, "the final RMSNorm is fused into the (single) layer's FFN call"
    b = x.shape[0]
    bias = _bias_call(rel_bias)
    expand = jnp.asarray(_head_expand_matrix(), BF16)
    pad_heads = lambda v: jnp.pad(v, (0, LANES - SSM_HEADS))[None, :]
    c_pad = jnp.pad(c, ((0, SUBLANES - b), (0, 0)))
    mod = _ada_call(c_pad, ada_w[0], ada_b[0][None, :])[:b].reshape(b, 6, D_MODEL)
    w_main = w_in.astype(BF16)[0]
    w_dt = jnp.pad(w_main[:, XBC_END:], ((0, 0), (0, LANES - SSM_HEADS)))
    q, kt, v, gz, act, dt = _inproj_call(x, mod, norm1[0][None, :], w_main, w_dt, conv_w[0], conv_b[0][None, :])
    return _mixer_ffn_call(
        q, kt, v, gz, act, dt, bias, sinks[0], pad_heads(dt_bias[0]), pad_heads(A_log[0]),
        jnp.repeat(D_skip[0], SSM_HEAD_DIM)[None, :], attn_out_norm[0][None, :], ssm_out_norm[0][None, :], expand,
        x, mod, w_o[0].astype(BF16), norm2[0][None, :], w_gate_up[0].astype(BF16), w_down[0].astype(BF16),
        final_norm[None, :])
```

```python
import numpy as np
import jax
import jax.numpy as jnp
from jax import lax
from jax.experimental import pallas as pl
from jax.experimental.pallas import tpu as pltpu

F32 = jnp.float32
BF16 = jnp.bfloat16

D_MODEL = 1024
HEAD_DIM = 64
N_Q_HEADS = 8
N_KV_HEADS = 2
Q_PER_KV = N_Q_HEADS // N_KV_HEADS
ATTN_WIDTH = N_Q_HEADS * HEAD_DIM
KV_WIDTH = N_KV_HEADS * HEAD_DIM
WINDOW = 128
BLOCK = 128
N_BUCKETS = 32
MAX_DISTANCE = 128

SSM_HEAD_DIM = 64
SSM_HEADS = 8
SSM_GROUPS = 2
HEADS_PER_GROUP = SSM_HEADS // SSM_GROUPS
SSM_WIDTH = SSM_HEADS * SSM_HEAD_DIM
GROUP_WIDTH = SSM_WIDTH // SSM_GROUPS
D_STATE = 128
CONV_K = 4
CHUNK = 128
XBC_WIDTH = SSM_WIDTH + 2 * SSM_GROUPS * D_STATE
D_FF = 2816
EPS = 1e-6
LOG2E = float(np.log2(np.e))

LANES = 128
SUBLANES = 8
VMEM_LIMIT_BYTES = 56 * 1024 * 1024

ADA_TN = 1536
INPROJ_TM = 1024
FFN_FC = 256
MIX_CHUNKS = 4
MIXER_YIELDS = 10
MIX_ROWS = MIX_CHUNKS * CHUNK

Q_END = ATTN_WIDTH
K_END = Q_END + KV_WIDTH
V_END = K_END + KV_WIDTH
Z_END = V_END + SSM_WIDTH
XBC_END = Z_END + XBC_WIDTH

assert BLOCK == CHUNK == LANES and KV_WIDTH == LANES and 2 * HEAD_DIM == LANES


def _silu_of_half(h):
    return h * jnp.tanh(h) + h


def _silu(x):
    return _silu_of_half(0.5 * x)


def _rms(x):
    return x * lax.rsqrt(jnp.mean(x * x, axis=-1, keepdims=True) + EPS)


def _split2(x):
    hi = x.astype(BF16)
    lo = (x - hi.astype(F32)).astype(BF16)
    return hi, lo


def _dot(a, b):
    return jnp.dot(a, b, preferred_element_type=F32)


def _dot_nt(a, b):
    return lax.dot_general(a, b, (((1,), (1,)), ((), ())), preferred_element_type=F32)


def _dot_tn(a, b):
    return lax.dot_general(a, b, (((0,), (0,)), ((), ())), preferred_element_type=F32)


def _ada_kernel(c_ref, w_ref, b_ref, o_ref):
    cond = _silu(c_ref[...])
    o_ref[...] = _dot(cond.astype(BF16), w_ref[...].astype(BF16)) + b_ref[...]


def _ada_call(c_pad, ada_w, ada_b):
    n = ada_w.shape[1]
    return pl.pallas_call(
        _ada_kernel,
        grid=(n // ADA_TN,),
        in_specs=[
            pl.BlockSpec((SUBLANES, D_MODEL), lambda j: (0, 0)),
            pl.BlockSpec((D_MODEL, ADA_TN), lambda j: (0, j)),
            pl.BlockSpec((1, ADA_TN), lambda j: (0, j)),
        ],
        out_specs=pl.BlockSpec((SUBLANES, ADA_TN), lambda j: (0, j)),
        out_shape=jax.ShapeDtypeStruct((SUBLANES, n), F32),
        compiler_params=pltpu.CompilerParams(dimension_semantics=("parallel",)),
        name="ada_mod",
    )(c_pad, ada_w, ada_b)


def _t5_bucket_table():
    i = np.arange(BLOCK)[:, None]
    j = np.arange(BLOCK)[None, :]
    n = np.where(j <= i, i - j, i + BLOCK - j)
    assert WINDOW == BLOCK and n.min() >= 0 and n.max() < WINDOW
    max_exact = N_BUCKETS // 2
    large = max_exact + (np.log(np.maximum(n, 1) / max_exact) / np.log(MAX_DISTANCE / max_exact)
                         * (N_BUCKETS - max_exact)).astype(np.int32)
    large = np.minimum(large, N_BUCKETS - 1)
    bucket = np.where(n < max_exact, n, large).astype(np.int32)
    first = np.where(j <= i, bucket, -1)
    return np.stack([first, bucket]).astype(np.int32)


def _bias_kernel(idx_ref, rb_ref, o_ref):
    for v in range(2):
        idx = idx_ref[v]
        hits = [idx == b for b in range(N_BUCKETS)]
        for h in range(N_Q_HEADS):
            acc = jnp.full((BLOCK, BLOCK), -jnp.inf, F32)
            for b in range(N_BUCKETS):
                acc = jnp.where(hits[b], rb_ref[b, h] * LOG2E, acc)
            o_ref[v, h] = acc


def _bias_call(rel_bias):
    idx = jnp.asarray(_t5_bucket_table())
    return pl.pallas_call(
        _bias_kernel,
        in_specs=[
            pl.BlockSpec(memory_space=pltpu.VMEM),
            pl.BlockSpec(memory_space=pltpu.SMEM),
        ],
        out_specs=pl.BlockSpec(memory_space=pltpu.VMEM),
        out_shape=jax.ShapeDtypeStruct((2, N_Q_HEADS, BLOCK, BLOCK), F32),
        name="t5_bias",
    )(idx, rel_bias)


def _inproj_norm(x_ref, mod_ref, n1_ref, hb_ref):
    gain = n1_ref[...] * (1.0 + mod_ref[1:2, :])
    hb_ref[...] = (_rms(x_ref[...]) * gain + mod_ref[0:1, :]).astype(BF16)


def _inproj_project(hb_ref, w_ref, wdt_ref, cw_ref, cb_ref, q_ref, kt_ref, v_ref, gz_ref, act_ref, dt_ref, xe_ref):
    tm = hb_ref.shape[0]
    proj = lambda lo, hi: _dot(hb_ref[...], w_ref[:, lo:hi])
    tile = 2 * LANES
    half = tm // 2
    for i, c0 in enumerate(range(0, XBC_WIDTH, tile)):
        cols = slice(c0, c0 + tile)
        carry = xe_ref[:, cols]
        for r0 in range(0, tm, half):
            xbc = _dot(hb_ref[r0:r0 + half, :], w_ref[:, Z_END + c0:Z_END + c0 + tile])
            xe = jnp.concatenate([carry, xbc], axis=0)
            xe = xe.reshape(half // SUBLANES + 1, SUBLANES, tile)
            slab_row = lax.broadcasted_iota(jnp.int32, xe.shape, 1)

            def shift_down(a, n):
                rot = pltpu.roll(a, n, axis=1)
                return jnp.where(slab_row < n, jnp.concatenate([rot[:1], rot[:-1]], axis=0), rot)

            w0, w1, w2, w3 = (0.5 * cw_ref[k:k + 1, cols] for k in range(CONV_K))
            x1 = shift_down(xe, 1)
            taps = shift_down(w0 * x1 + w1 * xe, 2) + (w2 * x1 + w3 * xe)
            conv_half = taps[1:].reshape(half, tile) + 0.5 * cb_ref[:, cols]
            carry = xbc[half - SUBLANES:, :]
            act_ref[r0:r0 + half, cols] = _silu_of_half(conv_half).astype(BF16)
        xe_ref[:, cols] = carry
        if i < Q_END // tile:
            q_ref[:, i * tile:(i + 1) * tile] = proj(i * tile, (i + 1) * tile).astype(BF16)
        else:
            z0 = (i - Q_END // tile) * tile
            gz_ref[:, z0:z0 + tile] = _silu(proj(V_END + z0, V_END + z0 + tile)).astype(BF16)
    kv = proj(Q_END, V_END)
    kt_ref[...] = (kv[:, :KV_WIDTH] * (HEAD_DIM ** -0.5 * LOG2E)).T.astype(BF16)
    v_ref[...] = kv[:, KV_WIDTH:].astype(BF16)
    dt_ref[...] = _dot(hb_ref[...], wdt_ref[...])


def _inproj_kernel(x_ref, mod_ref, n1_ref, w_ref, wdt_ref, cw_ref, cb_ref, q_ref, kt_ref, v_ref, gz_ref, act_ref,
                   dt_ref, hb_ref, xe_ref):
    @pl.when(pl.program_id(1) == 0)
    def _():
        xe_ref[...] = jnp.zeros_like(xe_ref)

    _inproj_norm(x_ref, mod_ref, n1_ref, hb_ref)
    _inproj_project(hb_ref, w_ref, wdt_ref, cw_ref, cb_ref, q_ref, kt_ref, v_ref, gz_ref, act_ref, dt_ref, xe_ref)


def _inproj_call(x, mod, norm1, w_main, w_dt, conv_w, conv_b):
    b, s, _ = x.shape
    tm = INPROJ_TM
    row = lambda width: pl.BlockSpec((None, tm, width), lambda i, j: (i, j, 0))
    const = lambda shape: pl.BlockSpec(shape, lambda i, j: (0,) * len(shape))
    return pl.pallas_call(
        _inproj_kernel,
        grid=(b, s // tm),
        in_specs=[
            row(D_MODEL),
            pl.BlockSpec((None, 6, D_MODEL), lambda i, j: (i, 0, 0)),
            const((1, D_MODEL)),
            const((D_MODEL, XBC_END + SSM_HEADS)),
            const((D_MODEL, LANES)),
            const((CONV_K, XBC_WIDTH)), const((1, XBC_WIDTH)),
        ],
        out_specs=[row(ATTN_WIDTH), pl.BlockSpec((None, KV_WIDTH, tm), lambda i, j: (i, 0, j)), row(KV_WIDTH),
                   row(SSM_WIDTH), row(XBC_WIDTH), row(LANES)],
        out_shape=[
            jax.ShapeDtypeStruct((b, s, ATTN_WIDTH), BF16),
            jax.ShapeDtypeStruct((b, KV_WIDTH, s), BF16),
            jax.ShapeDtypeStruct((b, s, KV_WIDTH), BF16),
            jax.ShapeDtypeStruct((b, s, SSM_WIDTH), BF16),
            jax.ShapeDtypeStruct((b, s, XBC_WIDTH), BF16),
            jax.ShapeDtypeStruct((b, s, LANES), F32),
        ],
        scratch_shapes=[
            pltpu.VMEM((tm, D_MODEL), BF16),
            pltpu.VMEM((SUBLANES, XBC_WIDTH), F32),
        ],
        compiler_params=pltpu.CompilerParams(
            dimension_semantics=("parallel", "arbitrary"), vmem_limit_bytes=VMEM_LIMIT_BYTES),
        name="in_proj",
    )(x, mod, norm1, w_main, w_dt, conv_w, conv_b)


def _stage_keys_values(kt_ref, ktp_ref, v_ref, vp_ref, ka_ref, vv_ref):
    zeros = jnp.zeros((HEAD_DIM, BLOCK + MIX_ROWS), BF16)
    for g in range(N_KV_HEADS):
        rows = slice(g * HEAD_DIM, (g + 1) * HEAD_DIM)
        for par in range(2):
            live = slice(par * HEAD_DIM, (par + 1) * HEAD_DIM)
            dead = slice((1 - par) * HEAD_DIM, (2 - par) * HEAD_DIM)
            ka_ref[g, par, live, 0:BLOCK] = ktp_ref[rows, :]
            ka_ref[g, par, live, BLOCK:] = kt_ref[rows, :]
            ka_ref[g, par, dead, :] = zeros

    def duplicate_halves(v):
        v32 = pltpu.bitcast(v, jnp.uint32)
        swapped = pltpu.roll(v32, HEAD_DIM, axis=1)
        lane = lax.broadcasted_iota(jnp.int32, v32.shape, 1)
        first = lane < HEAD_DIM
        return (pltpu.bitcast(jnp.where(first, v32, swapped), BF16),
                pltpu.bitcast(jnp.where(first, swapped, v32), BF16))

    for src, r0, r1 in ((vp_ref, 0, BLOCK), (v_ref, BLOCK, BLOCK + MIX_ROWS)):
        for g, dup in enumerate(duplicate_halves(src[...])):
            vv_ref[g, r0:r1, 0:LANES] = dup


def _scores(q_ref, ka_ref, j):
    rows = pl.ds(j * BLOCK, BLOCK)
    win = pl.ds(j * BLOCK, 2 * BLOCK)
    out = []
    for h in range(N_Q_HEADS):
        t, par = h // 2, h % 2
        out.append(_dot(q_ref[rows, t * LANES:(t + 1) * LANES], ka_ref[t // 2, par, :, win]))
    return out


def _softmax(scores, bias_ref, sinks_ref, own):
    s = [jnp.where(own, sc[:, BLOCK:], sc[:, :BLOCK]) + bias_ref[h] for h, sc in enumerate(scores)]
    sink = [sinks_ref[h] * LOG2E for h in range(N_Q_HEADS)]
    m = [jnp.maximum(jnp.max(s[h], axis=-1, keepdims=True), sink[h]) for h in range(N_Q_HEADS)]
    p = [jnp.exp2(s[h] - m[h]) for h in range(N_Q_HEADS)]
    zero = jnp.zeros((BLOCK, BLOCK), BF16)
    pb = [ph.astype(BF16) for ph in p]
    pcat = [jnp.concatenate([jnp.where(own, zero, b), jnp.where(own, b, zero)], axis=1) for b in pb]
    sink_term = [jnp.exp2(sink[h] - m[h]) for h in range(N_Q_HEADS)]
    return pcat, sink_term


def _values(pcat, vv_ref, j):
    win = pl.ds(j * BLOCK, 2 * BLOCK)
    return [_dot(pcat[h], vv_ref[h // Q_PER_KV, win, :]) for h in range(N_Q_HEADS)]


def _finish_attention(o2, sink_term, an_ref, o_ref, j):
    rows = pl.ds(j * BLOCK, BLOCK)
    heads = [o2[h][:, :LANES] / (o2[h][:, LANES:] + sink_term[h]) for h in range(N_Q_HEADS)]
    lane = lax.broadcasted_iota(jnp.int32, (BLOCK, LANES), 1)
    tiles = [jnp.where(lane < HEAD_DIM, heads[2 * t], heads[2 * t + 1]) for t in range(N_Q_HEADS // 2)]
    ssq = sum(jnp.sum(tl * tl, axis=-1, keepdims=True) for tl in tiles)
    inv = lax.rsqrt(ssq * (1.0 / ATTN_WIDTH) + EPS)
    for t, tl in enumerate(tiles):
        cols = slice(t * LANES, (t + 1) * LANES)
        o_ref[rows, cols] = (tl * inv * an_ref[:, cols]).astype(BF16)


def _mixer_steps(first_tile, q_ref, kt_ref, ktp_ref, v_ref, vp_ref, gz_ref, act_ref, dt_ref, bias_ref, sinks_ref,
                 dtb_ref, alog_ref, dskip_ref, an_ref, sn_ref, expand_ref, o_ref, state_ref, ka_ref, vv_ref,
                 xbd_ref):
    nch = MIX_CHUNKS
    row_i = lax.broadcasted_iota(jnp.int32, (BLOCK, BLOCK), 0)
    col_i = lax.broadcasted_iota(jnp.int32, (BLOCK, BLOCK), 1)
    own = col_i <= row_i
    bias_of = lambda j: bias_ref.at[0 if (first_tile and j == 0) else 1]
    grp = lambda a, j, g: a[j * CHUNK:(j + 1) * CHUNK, g * GROUP_WIDTH:(g + 1) * GROUP_WIDTH]
    softmax = lambda j, sc: _softmax(sc, bias_of(j), sinks_ref, own)
    finish = lambda j, vals, soft: _finish_attention(vals, soft[1], an_ref, o_ref, j)

    _stage_keys_values(kt_ref, ktp_ref, v_ref, vp_ref, ka_ref, vv_ref)
    dt = jax.nn.softplus(dt_ref[...] + dtb_ref[...])
    dta = dt * (-LOG2E * jnp.exp(alog_ref[...]))
    tri = own.astype(BF16)
    dta_hi, dta_lo = _split2(dta)
    scores = {0: _scores(q_ref, ka_ref, 0), 1: _scores(q_ref, ka_ref, 1)}
    acs = [_dot(tri, dta_hi[j * CHUNK:(j + 1) * CHUNK]) + _dot(tri, dta_lo[j * CHUNK:(j + 1) * CHUNK])
           for j in range(nch)]
    yield

    soft = {0: softmax(0, scores.pop(0))}
    acs_dt_t = [(a - jnp.log2(dt[j * CHUNK:(j + 1) * CHUNK])).T for j, a in enumerate(acs)]
    dec_in = [jnp.exp2(a) for a in acs]
    dec_out = jnp.concatenate([jnp.exp2(a[CHUNK - 1:CHUNK, :] - a) for a in acs], axis=0)
    state_in = (dt * dec_out).astype(BF16)
    decay_in = jnp.concatenate(dec_in, axis=0).astype(BF16)
    decay_tail = [_split2(d[CHUNK - SUBLANES:, :]) for d in dec_in]
    yield

    vals = {0: _values(soft[0][0], vv_ref, 0)}
    expand = expand_ref[...]
    state_in_x = _dot(state_in, expand)
    chunk_decay = [(_dot(hi, expand) + _dot(lo, expand))[SUBLANES - 1:SUBLANES, :] for hi, lo in decay_tail]
    scores[2] = _scores(q_ref, ka_ref, 2)
    soft[1] = softmax(1, scores.pop(1))
    finish(0, vals.pop(0), soft.pop(0))
    yield

    vals[1] = _values(soft[1][0], vv_ref, 1)
    scores[3] = _scores(q_ref, ka_ref, 3)
    xs = act_ref[:, :SSM_WIDTH].astype(F32)
    xdec_b = (xs * state_in_x).astype(BF16)
    lane_head = lax.broadcasted_iota(jnp.int32, (CHUNK, GROUP_WIDTH), 1) // SSM_HEAD_DIM
    for j in range(nch):
        for g in range(SSM_GROUPS):
            xg = act_ref[j * CHUNK:(j + 1) * CHUNK, g * GROUP_WIDTH:(g + 1) * GROUP_WIDTH]
            for r in range(HEADS_PER_GROUP):
                xbd_ref[j, g, r * CHUNK:(r + 1) * CHUNK, :] = jnp.where(lane_head == r, xg, jnp.zeros_like(xg))
    soft[2] = softmax(2, scores.pop(2))
    finish(1, vals.pop(1), soft.pop(1))
    yield

    bc_b = act_ref[:, SSM_WIDTH:]
    bm = [[bc_b[j * CHUNK:(j + 1) * CHUNK, g * D_STATE:(g + 1) * D_STATE] for g in range(SSM_GROUPS)]
          for j in range(nch)]
    cm = [[bc_b[j * CHUNK:(j + 1) * CHUNK, (SSM_GROUPS + g) * D_STATE:(SSM_GROUPS + g + 1) * D_STATE]
           for g in range(SSM_GROUPS)] for j in range(nch)]
    cbm = [[_dot_nt(cm[j][g], bm[j][g]) for g in range(SSM_GROUPS)] for j in range(nch)]
    d_state = [[_dot_tn(bm[j][g], grp(xdec_b, j, g)) for g in range(SSM_GROUPS)] for j in range(nch)]
    vals[2] = _values(soft[2][0], vv_ref, 2)
    soft[3] = softmax(3, scores.pop(3))

    def intra_weights(j):
        per_group = []
        for g in range(SSM_GROUPS):
            w = []
            for r in range(HEADS_PER_GROUP):
                h = g * HEADS_PER_GROUP + r
                seg = acs[j][:, h:h + 1] - acs_dt_t[j][h:h + 1, :]
                w.append(cbm[j][g] * jnp.exp2(jnp.where(own, seg, -jnp.inf)))
            per_group.append((jnp.concatenate(w[:2], axis=1).astype(BF16),
                              jnp.concatenate(w[2:], axis=1).astype(BF16)))
        return per_group

    def intra_product(j, w):
        return [_dot(w[g][0], xbd_ref[j, g, 0:2 * CHUNK, :]) + _dot(w[g][1], xbd_ref[j, g, 2 * CHUNK:, :])
                for g in range(SSM_GROUPS)]

    yield
    w = {0: intra_weights(0)}
    finish(2, vals.pop(2), soft.pop(2))
    yield

    vals[3] = _values(soft[3][0], vv_ref, 3)
    y_diag = [intra_product(0, w.pop(0))]
    w[1] = intra_weights(1)
    yield

    y_diag.append(intra_product(1, w.pop(1)))
    w[2] = intra_weights(2)
    finish(3, vals.pop(3), soft.pop(3))
    yield

    y_diag.append(intra_product(2, w.pop(2)))
    w[3] = intra_weights(3)
    yield

    y_diag.append(intra_product(3, w.pop(3)))
    entering = []
    for g in range(SSM_GROUPS):
        state = state_ref[g]
        per_chunk = []
        for j in range(nch):
            per_chunk.append(state.astype(BF16))
            state = state * chunk_decay[j][:, g * GROUP_WIDTH:(g + 1) * GROUP_WIDTH] + d_state[j][g]
        state_ref[g] = state
        entering.append(per_chunk)
    decay_in_x = _dot(decay_in, expand_ref[...])
    yield

    y_rows = [jnp.concatenate([y_diag[j][g] + _dot(cm[j][g], entering[g][j]) * grp(decay_in_x, j, g)
                               for g in range(SSM_GROUPS)], axis=-1) for j in range(nch)]
    y = jnp.concatenate(y_rows, axis=0) + dskip_ref[...] * act_ref[:, :SSM_WIDTH].astype(F32)
    y = y * gz_ref[...].astype(F32)
    y = jnp.concatenate([_rms(y[:, :GROUP_WIDTH]), _rms(y[:, GROUP_WIDTH:])], axis=-1)
    o_ref[:, ATTN_WIDTH:] = (y * sn_ref[...]).astype(BF16)


FFN_PHASES = D_FF // FFN_FC + D_MODEL // FFN_FC


def _ffn_steps(mix_ref, x_ref, mod_ref, wo_ref, n2_ref, wgu_ref, wd_ref, fn_ref, o_ref, h_ref, a_ref):
    gain2 = n2_ref[...] * (1.0 + mod_ref[4:5, :])
    x1 = x_ref[...] + mod_ref[2:3, :] * _dot(mix_ref[...], wo_ref[...])
    h_ref[...] = (_rms(x1) * gain2 + mod_ref[3:4, :]).astype(BF16)
    o_ref[...] = x1
    yield
    for c0 in range(0, D_FF, FFN_FC):
        hb = h_ref[...]
        g = _dot(hb, wgu_ref[:, c0:c0 + FFN_FC])
        u = _dot(hb, wgu_ref[:, D_FF + c0:D_FF + c0 + FFN_FC])
        a_ref[:, c0:c0 + FFN_FC] = (_silu(g) * u).astype(BF16)
        yield
    for c0 in range(0, D_MODEL, FFN_FC):
        cols = slice(c0, c0 + FFN_FC)
        o_ref[:, cols] = o_ref[:, cols] + mod_ref[5:6, cols] * _dot(a_ref[...], wd_ref[:, cols])
        if c0 + FFN_FC == D_MODEL:
            o_ref[...] = _rms(o_ref[...]) * fn_ref[...]
        yield


def _mixer_ffn_kernel(q_ref, kt_ref, ktp_ref, v_ref, vp_ref, gz_ref, act_ref, dt_ref, bias_ref, sinks_ref, dtb_ref,
                      alog_ref, dskip_ref, an_ref, sn_ref, expand_ref, x_ref, mod_ref, wo_ref, n2_ref, wgu_ref,
                      wd_ref, fn_ref, o_ref, state_ref, ka_ref, vv_ref, xbd_ref, mix_ref, h_ref, a_ref):
    s = pl.program_id(1)
    last = pl.num_programs(1) - 1
    mixer = lambda first: _mixer_steps(first, q_ref, kt_ref, ktp_ref, v_ref, vp_ref, gz_ref, act_ref, dt_ref,
                                       bias_ref, sinks_ref, dtb_ref, alog_ref, dskip_ref, an_ref, sn_ref,
                                       expand_ref, mix_ref, state_ref, ka_ref, vv_ref, xbd_ref)
    ffn = lambda: _ffn_steps(mix_ref, x_ref, mod_ref, wo_ref, n2_ref, wgu_ref, wd_ref, fn_ref, o_ref, h_ref, a_ref)

    @pl.when(s == 0)
    def _():
        state_ref[...] = jnp.zeros_like(state_ref)
        vv_ref[:, :, LANES:] = jnp.ones((N_KV_HEADS, BLOCK + MIX_ROWS, LANES), BF16)
        for _ in mixer(True):
            pass

    @pl.when((s > 0) & (s < last))
    def _():
        ffn_steps, issued = ffn(), 0
        next(ffn_steps)
        for k, _ in enumerate(mixer(False)):
            while issued < ((k + 1) * (FFN_PHASES - 1)) // MIXER_YIELDS:
                next(ffn_steps)
                issued += 1
        assert k == MIXER_YIELDS - 1
        for _ in ffn_steps:
            pass

    @pl.when(s == last)
    def _():
        for _ in ffn():
            pass


def _mixer_ffn_call(q, kt, v, gz, act, dt, bias, sinks, dt_bias, a_log, d_skip_x, attn_norm, ssm_norm, expand,
                    x, mod, w_o, norm2, w_gu, w_down, final_norm):
    b, s, _ = q.shape
    rows = MIX_ROWS
    n_tiles = s // rows
    mix_tile = lambda i, c: jnp.minimum(c, n_tiles - 1)
    ffn_tile = lambda i, c: jnp.maximum(c - 1, 0)
    prev_blk = lambda i, c: jnp.maximum(mix_tile(i, c) * MIX_CHUNKS - 1, 0)
    mrow = lambda width: pl.BlockSpec((None, rows, width), lambda i, c: (i, mix_tile(i, c), 0))
    frow = lambda: pl.BlockSpec((None, rows, D_MODEL), lambda i, c: (i, ffn_tile(i, c), 0))
    const = lambda shape: pl.BlockSpec(shape, lambda i, c: (0,) * len(shape))
    resident = lambda shape: pl.BlockSpec(shape, lambda i, c: (0,) * len(shape), pipeline_mode=pl.Buffered(1))
    return pl.pallas_call(
        _mixer_ffn_kernel,
        grid=(b, n_tiles + 1),
        in_specs=[
            mrow(ATTN_WIDTH),
            pl.BlockSpec((None, KV_WIDTH, rows), lambda i, c: (i, 0, mix_tile(i, c))),
            pl.BlockSpec((None, KV_WIDTH, BLOCK), lambda i, c: (i, 0, prev_blk(i, c))),
            mrow(KV_WIDTH),
            pl.BlockSpec((None, BLOCK, KV_WIDTH), lambda i, c: (i, prev_blk(i, c), 0)),
            mrow(SSM_WIDTH), mrow(XBC_WIDTH), mrow(LANES),
            const((2, N_Q_HEADS, BLOCK, BLOCK)),
            pl.BlockSpec(memory_space=pltpu.SMEM),
            const((1, LANES)), const((1, LANES)),
            const((1, SSM_WIDTH)), const((1, ATTN_WIDTH)), const((1, SSM_WIDTH)),
            const((LANES, SSM_WIDTH)),
            frow(),
            pl.BlockSpec((None, 6, D_MODEL), lambda i, c: (i, 0, 0)),
            resident((D_MODEL, D_MODEL)), const((1, D_MODEL)),
            resident((D_MODEL, 2 * D_FF)), resident((D_FF, D_MODEL)), const((1, D_MODEL)),
        ],
        out_specs=frow(),
        out_shape=jax.ShapeDtypeStruct((b, s, D_MODEL), F32),
        scratch_shapes=[
            pltpu.VMEM((SSM_GROUPS, D_STATE, GROUP_WIDTH), F32),
            pltpu.VMEM((N_KV_HEADS, 2, KV_WIDTH, BLOCK + rows), BF16),
            pltpu.VMEM((N_KV_HEADS, BLOCK + rows, 2 * LANES), BF16),
            pltpu.VMEM((MIX_CHUNKS, SSM_GROUPS, HEADS_PER_GROUP * CHUNK, GROUP_WIDTH), BF16),
            pltpu.VMEM((rows, ATTN_WIDTH + SSM_WIDTH), BF16),
            pltpu.VMEM((rows, D_MODEL), BF16),
            pltpu.VMEM((rows, D_FF), BF16),
        ],
        compiler_params=pltpu.CompilerParams(
            dimension_semantics=("parallel", "arbitrary"), vmem_limit_bytes=VMEM_LIMIT_BYTES),
        name="mixer_ffn",
    )(q, kt, kt, v, v, gz, act, dt, bias, sinks, dt_bias, a_log, d_skip_x, attn_norm, ssm_norm, expand,
      x, mod, w_o, norm2, w_gu, w_down, final_norm)


def _head_expand_matrix():
    e = np.zeros((LANES, SSM_WIDTH), np.float32)
    for h in range(SSM_HEADS):
        e[h, h * SSM_HEAD_DIM:(h + 1) * SSM_HEAD_DIM] = 1.0
    return e


def kernel(x, c, ada_w, ada_b, norm1, w_in, conv_w, conv_b, dt_bias, A_log, D_skip, sinks, attn_out_norm,
           ssm_out_norm, w_o, norm2, w_gate_up, w_down, rel_bias, final_norm):
    assert ada_w.shape[0] == 1, "the final RMSNorm is fused into the (single) layer's FFN call"
    b = x.shape[0]
    bias = _bias_call(rel_bias)
    expand = jnp.asarray(_head_expand_matrix(), BF16)
    pad_heads = lambda v: jnp.pad(v, (0, LANES - SSM_HEADS))[None, :]
    c_pad = jnp.pad(c, ((0, SUBLANES - b), (0, 0)))
    mod = _ada_call(c_pad, ada_w[0], ada_b[0][None, :])[:b].reshape(b, 6, D_MODEL)
    w_main = w_in.astype(BF16)[0]
    w_dt = jnp.pad(w_main[:, XBC_END:], ((0, 0), (0, LANES - SSM_HEADS)))
    q, kt, v, gz, act, dt = _inproj_call(x, mod, norm1[0][None, :], w_main, w_dt, conv_w[0], conv_b[0][None, :])
    return _mixer_ffn_call(
        q, kt, v, gz, act, dt, bias, sinks[0], pad_heads(dt_bias[0]), pad_heads(A_log[0]),
        jnp.repeat(D_skip[0], SSM_HEAD_DIM)[None, :], attn_out_norm[0][None, :], ssm_out_norm[0][None, :], expand,
        x, mod, w_o[0].astype(BF16), norm2[0][None, :], w_gate_up[0].astype(BF16), w_down[0].astype(BF16),
        final_norm[None, :])
```

```python
import numpy as np
import jax
import jax.numpy as jnp
from jax import lax
from jax.experimental import pallas as pl
from jax.experimental.pallas import tpu as pltpu

F32 = jnp.float32
BF16 = jnp.bfloat16

D_MODEL = 1024
HEAD_DIM = 64
N_Q_HEADS = 8
N_KV_HEADS = 2
Q_PER_KV = N_Q_HEADS // N_KV_HEADS
ATTN_WIDTH = N_Q_HEADS * HEAD_DIM
KV_WIDTH = N_KV_HEADS * HEAD_DIM
WINDOW = 128
BLOCK = 128
N_BUCKETS = 32
MAX_DISTANCE = 128

SSM_HEAD_DIM = 64
SSM_HEADS = 8
SSM_GROUPS = 2
HEADS_PER_GROUP = SSM_HEADS // SSM_GROUPS
SSM_WIDTH = SSM_HEADS * SSM_HEAD_DIM
GROUP_WIDTH = SSM_WIDTH // SSM_GROUPS
D_STATE = 128
CONV_K = 4
CHUNK = 128
XBC_WIDTH = SSM_WIDTH + 2 * SSM_GROUPS * D_STATE
D_FF = 2816
EPS = 1e-6
LOG2E = float(np.log2(np.e))

LANES = 128
SUBLANES = 8
VMEM_LIMIT_BYTES = 56 * 1024 * 1024

ADA_TN = 1536
INPROJ_TM = 1024
FFN_FC = 256
MIX_CHUNKS = 4
MIXER_YIELDS = 10
MIX_ROWS = MIX_CHUNKS * CHUNK

Q_END = ATTN_WIDTH
K_END = Q_END + KV_WIDTH
V_END = K_END + KV_WIDTH
Z_END = V_END + SSM_WIDTH
XBC_END = Z_END + XBC_WIDTH

assert BLOCK == CHUNK == LANES and KV_WIDTH == LANES and 2 * HEAD_DIM == LANES


def _silu_of_half(h):
    return h * jnp.tanh(h) + h


def _silu(x):
    return _silu_of_half(0.5 * x)


def _rms(x):
    return x * lax.rsqrt(jnp.mean(x * x, axis=-1, keepdims=True) + EPS)


def _split2(x):
    hi = x.astype(BF16)
    lo = (x - hi.astype(F32)).astype(BF16)
    return hi, lo


def _dot(a, b):
    return jnp.dot(a, b, preferred_element_type=F32)


def _dot_nt(a, b):
    return lax.dot_general(a, b, (((1,), (1,)), ((), ())), preferred_element_type=F32)


def _dot_tn(a, b):
    return lax.dot_general(a, b, (((0,), (0,)), ((), ())), preferred_element_type=F32)


def _ada_kernel(c_ref, w_ref, b_ref, o_ref):
    cond = _silu(c_ref[...])
    o_ref[...] = _dot(cond.astype(BF16), w_ref[...].astype(BF16)) + b_ref[...]


def _ada_call(c_pad, ada_w, ada_b):
    n = ada_w.shape[1]
    return pl.pallas_call(
        _ada_kernel,
        grid=(n // ADA_TN,),
        in_specs=[
            pl.BlockSpec((SUBLANES, D_MODEL), lambda j: (0, 0)),
            pl.BlockSpec((D_MODEL, ADA_TN), lambda j: (0, j)),
            pl.BlockSpec((1, ADA_TN), lambda j: (0, j)),
        ],
        out_specs=pl.BlockSpec((SUBLANES, ADA_TN), lambda j: (0, j)),
        out_shape=jax.ShapeDtypeStruct((SUBLANES, n), F32),
        compiler_params=pltpu.CompilerParams(dimension_semantics=("parallel",)),
        name="ada_mod",
    )(c_pad, ada_w, ada_b)


def _t5_bucket_table():
    i = np.arange(BLOCK)[:, None]
    j = np.arange(BLOCK)[None, :]
    n = np.where(j <= i, i - j, i + BLOCK - j)
    assert WINDOW == BLOCK and n.min() >= 0 and n.max() < WINDOW
    max_exact = N_BUCKETS // 2
    large = max_exact + (np.log(np.maximum(n, 1) / max_exact) / np.log(MAX_DISTANCE / max_exact)
                         * (N_BUCKETS - max_exact)).astype(np.int32)
    large = np.minimum(large, N_BUCKETS - 1)
    bucket = np.where(n < max_exact, n, large).astype(np.int32)
    first = np.where(j <= i, bucket, -1)
    return np.stack([first, bucket]).astype(np.int32)


def _bias_kernel(idx_ref, rb_ref, o_ref):
    for v in range(2):
        idx = idx_ref[v]
        hits = [idx == b for b in range(N_BUCKETS)]
        for h in range(N_Q_HEADS):
            acc = jnp.full((BLOCK, BLOCK), -jnp.inf, F32)
            for b in range(N_BUCKETS):
                acc = jnp.where(hits[b], rb_ref[b, h] * LOG2E, acc)
            o_ref[v, h] = acc


def _bias_call(rel_bias):
    idx = jnp.asarray(_t5_bucket_table())
    return pl.pallas_call(
        _bias_kernel,
        in_specs=[
            pl.BlockSpec(memory_space=pltpu.VMEM),
            pl.BlockSpec(memory_space=pltpu.SMEM),
        ],
        out_specs=pl.BlockSpec(memory_space=pltpu.VMEM),
        out_shape=jax.ShapeDtypeStruct((2, N_Q_HEADS, BLOCK, BLOCK), F32),
        name="t5_bias",
    )(idx, rel_bias)


def _inproj_norm(x_ref, mod_ref, n1_ref, hb_ref):
    gain = n1_ref[...] * (1.0 + mod_ref[1:2, :])
    hb_ref[...] = (_rms(x_ref[...]) * gain + mod_ref[0:1, :]).astype(BF16)


def _inproj_project(hb_ref, w_ref, wdt_ref, cw_ref, cb_ref, q_ref, kt_ref, v_ref, gz_ref, act_ref, dt_ref, xe_ref):
    tm = hb_ref.shape[0]
    proj = lambda lo, hi: _dot(hb_ref[...], w_ref[:, lo:hi])
    tile = 2 * LANES
    half = tm // 2
    for i, c0 in enumerate(range(0, XBC_WIDTH, tile)):
        cols = slice(c0, c0 + tile)
        carry = xe_ref[:, cols]
        for r0 in range(0, tm, half):
            xbc = _dot(hb_ref[r0:r0 + half, :], w_ref[:, Z_END + c0:Z_END + c0 + tile])
            xe = jnp.concatenate([carry, xbc], axis=0)
            xe = xe.reshape(half // SUBLANES + 1, SUBLANES, tile)
            slab_row = lax.broadcasted_iota(jnp.int32, xe.shape, 1)

            def shift_down(a, n):
                rot = pltpu.roll(a, n, axis=1)
                return jnp.where(slab_row < n, jnp.concatenate([rot[:1], rot[:-1]], axis=0), rot)

            w0, w1, w2, w3 = (0.5 * cw_ref[k:k + 1, cols] for k in range(CONV_K))
            x1 = shift_down(xe, 1)
            taps = shift_down(w0 * x1 + w1 * xe, 2) + (w2 * x1 + w3 * xe)
            conv_half = taps[1:].reshape(half, tile) + 0.5 * cb_ref[:, cols]
            carry = xbc[half - SUBLANES:, :]
            act_ref[r0:r0 + half, cols] = _silu_of_half(conv_half).astype(BF16)
        xe_ref[:, cols] = carry
        if i < Q_END // tile:
            q_ref[:, i * tile:(i + 1) * tile] = proj(i * tile, (i + 1) * tile).astype(BF16)
        else:
            z0 = (i - Q_END // tile) * tile
            gz_ref[:, z0:z0 + tile] = _silu(proj(V_END + z0, V_END + z0 + tile)).astype(BF16)
    kv = proj(Q_END, V_END)
    kt_ref[...] = (kv[:, :KV_WIDTH] * (HEAD_DIM ** -0.5 * LOG2E)).T.astype(BF16)
    v_ref[...] = kv[:, KV_WIDTH:].astype(BF16)
    dt_ref[...] = _dot(hb_ref[...], wdt_ref[...])


def _inproj_kernel(x_ref, mod_ref, n1_ref, w_ref, wdt_ref, cw_ref, cb_ref, q_ref, kt_ref, v_ref, gz_ref, act_ref,
                   dt_ref, hb_ref, xe_ref):
    @pl.when(pl.program_id(1) == 0)
    def _():
        xe_ref[...] = jnp.zeros_like(xe_ref)

    _inproj_norm(x_ref, mod_ref, n1_ref, hb_ref)
    _inproj_project(hb_ref, w_ref, wdt_ref, cw_ref, cb_ref, q_ref, kt_ref, v_ref, gz_ref, act_ref, dt_ref, xe_ref)


def _inproj_call(x, mod, norm1, w_main, w_dt, conv_w, conv_b):
    b, s, _ = x.shape
    tm = INPROJ_TM
    row = lambda width: pl.BlockSpec((None, tm, width), lambda i, j: (i, j, 0))
    const = lambda shape: pl.BlockSpec(shape, lambda i, j: (0,) * len(shape))
    return pl.pallas_call(
        _inproj_kernel,
        grid=(b, s // tm),
        in_specs=[
            row(D_MODEL),
            pl.BlockSpec((None, 6, D_MODEL), lambda i, j: (i, 0, 0)),
            const((1, D_MODEL)),
            const((D_MODEL, XBC_END + SSM_HEADS)),
            const((D_MODEL, LANES)),
            const((CONV_K, XBC_WIDTH)), const((1, XBC_WIDTH)),
        ],
        out_specs=[row(ATTN_WIDTH), pl.BlockSpec((None, KV_WIDTH, tm), lambda i, j: (i, 0, j)), row(KV_WIDTH),
                   row(SSM_WIDTH), row(XBC_WIDTH), row(LANES)],
        out_shape=[
            jax.ShapeDtypeStruct((b, s, ATTN_WIDTH), BF16),
            jax.ShapeDtypeStruct((b, KV_WIDTH, s), BF16),
            jax.ShapeDtypeStruct((b, s, KV_WIDTH), BF16),
            jax.ShapeDtypeStruct((b, s, SSM_WIDTH), BF16),
            jax.ShapeDtypeStruct((b, s, XBC_WIDTH), BF16),
            jax.ShapeDtypeStruct((b, s, LANES), F32),
        ],
        scratch_shapes=[
            pltpu.VMEM((tm, D_MODEL), BF16),
            pltpu.VMEM((SUBLANES, XBC_WIDTH), F32),
        ],
        compiler_params=pltpu.CompilerParams(
            dimension_semantics=("parallel", "arbitrary"), vmem_limit_bytes=VMEM_LIMIT_BYTES),
        name="in_proj",
    )(x, mod, norm1, w_main, w_dt, conv_w, conv_b)


def _stage_keys_values(kt_ref, ktp_ref, v_ref, vp_ref, ka_ref, vv_ref):
    zeros = jnp.zeros((HEAD_DIM, BLOCK + MIX_ROWS), BF16)
    for g in range(N_KV_HEADS):
        rows = slice(g * HEAD_DIM, (g + 1) * HEAD_DIM)
        for par in range(2):
            live = slice(par * HEAD_DIM, (par + 1) * HEAD_DIM)
            dead = slice((1 - par) * HEAD_DIM, (2 - par) * HEAD_DIM)
            ka_ref[g, par, live, 0:BLOCK] = ktp_ref[rows, :]
            ka_ref[g, par, live, BLOCK:] = kt_ref[rows, :]
            ka_ref[g, par, dead, :] = zeros

    def duplicate_halves(v):
        v32 = pltpu.bitcast(v, jnp.uint32)
        swapped = pltpu.roll(v32, HEAD_DIM, axis=1)
        lane = lax.broadcasted_iota(jnp.int32, v32.shape, 1)
        first = lane < HEAD_DIM
        return (pltpu.bitcast(jnp.where(first, v32, swapped), BF16),
                pltpu.bitcast(jnp.where(first, swapped, v32), BF16))

    for src, r0, r1 in ((vp_ref, 0, BLOCK), (v_ref, BLOCK, BLOCK + MIX_ROWS)):
        for g, dup in enumerate(duplicate_halves(src[...])):
            vv_ref[g, r0:r1, 0:LANES] = dup


def _scores(q_ref, ka_ref, j):
    rows = pl.ds(j * BLOCK, BLOCK)
    win = pl.ds(j * BLOCK, 2 * BLOCK)
    out = []
    for h in range(N_Q_HEADS):
        t, par = h // 2, h % 2
        out.append(_dot(q_ref[rows, t * LANES:(t + 1) * LANES], ka_ref[t // 2, par, :, win]))
    return out


def _softmax(scores, bias_ref, sinks_ref, own):
    s = [jnp.where(own, sc[:, BLOCK:], sc[:, :BLOCK]) + bias_ref[h] for h, sc in enumerate(scores)]
    sink = [sinks_ref[h] * LOG2E for h in range(N_Q_HEADS)]
    m = [jnp.maximum(jnp.max(s[h], axis=-1, keepdims=True), sink[h]) for h in range(N_Q_HEADS)]
    p = [jnp.exp2(s[h] - m[h]) for h in range(N_Q_HEADS)]
    zero = jnp.zeros((BLOCK, BLOCK), BF16)
    pb = [ph.astype(BF16) for ph in p]
    pcat = [jnp.concatenate([jnp.where(own, zero, b), jnp.where(own, b, zero)], axis=1) for b in pb]
    sink_term = [jnp.exp2(sink[h] - m[h]) for h in range(N_Q_HEADS)]
    return pcat, sink_term


def _values(pcat, vv_ref, j):
    win = pl.ds(j * BLOCK, 2 * BLOCK)
    return [_dot(pcat[h], vv_ref[h // Q_PER_KV, win, :]) for h in range(N_Q_HEADS)]


def _finish_attention(o2, sink_term, an_ref, o_ref, j):
    rows = pl.ds(j * BLOCK, BLOCK)
    heads = [o2[h][:, :LANES] / (o2[h][:, LANES:] + sink_term[h]) for h in range(N_Q_HEADS)]
    lane = lax.broadcasted_iota(jnp.int32, (BLOCK, LANES), 1)
    tiles = [jnp.where(lane < HEAD_DIM, heads[2 * t], heads[2 * t + 1]) for t in range(N_Q_HEADS // 2)]
    ssq = sum(jnp.sum(tl * tl, axis=-1, keepdims=True) for tl in tiles)
    inv = lax.rsqrt(ssq * (1.0 / ATTN_WIDTH) + EPS)
    for t, tl in enumerate(tiles):
        cols = slice(t * LANES, (t + 1) * LANES)
        o_ref[rows, cols] = (tl * inv * an_ref[:, cols]).astype(BF16)


def _mixer_steps(first_tile, q_ref, kt_ref, ktp_ref, v_ref, vp_ref, gz_ref, act_ref, dt_ref, bias_ref, sinks_ref,
                 dtb_ref, alog_ref, dskip_ref, an_ref, sn_ref, expand_ref, o_ref, state_ref, ka_ref, vv_ref,
                 xbd_ref):
    nch = MIX_CHUNKS
    row_i = lax.broadcasted_iota(jnp.int32, (BLOCK, BLOCK), 0)
    col_i = lax.broadcasted_iota(jnp.int32, (BLOCK, BLOCK), 1)
    own = col_i <= row_i
    bias_of = lambda j: bias_ref.at[0 if (first_tile and j == 0) else 1]
    grp = lambda a, j, g: a[j * CHUNK:(j + 1) * CHUNK, g * GROUP_WIDTH:(g + 1) * GROUP_WIDTH]
    softmax = lambda j, sc: _softmax(sc, bias_of(j), sinks_ref, own)
    finish = lambda j, vals, soft: _finish_attention(vals, soft[1], an_ref, o_ref, j)

    _stage_keys_values(kt_ref, ktp_ref, v_ref, vp_ref, ka_ref, vv_ref)
    dt = jax.nn.softplus(dt_ref[...] + dtb_ref[...])
    dta = dt * (-LOG2E * jnp.exp(alog_ref[...]))
    tri = own.astype(BF16)
    dta_hi, dta_lo = _split2(dta)
    scores = {0: _scores(q_ref, ka_ref, 0), 1: _scores(q_ref, ka_ref, 1)}
    acs = [_dot(tri, dta_hi[j * CHUNK:(j + 1) * CHUNK]) + _dot(tri, dta_lo[j * CHUNK:(j + 1) * CHUNK])
           for j in range(nch)]
    yield

    soft = {0: softmax(0, scores.pop(0))}
    acs_dt_t = [(a - jnp.log2(dt[j * CHUNK:(j + 1) * CHUNK])).T for j, a in enumerate(acs)]
    dec_in = [jnp.exp2(a) for a in acs]
    dec_out = jnp.concatenate([jnp.exp2(a[CHUNK - 1:CHUNK, :] - a) for a in acs], axis=0)
    state_in = (dt * dec_out).astype(BF16)
    decay_in = jnp.concatenate(dec_in, axis=0).astype(BF16)
    decay_tail = [_split2(d[CHUNK - SUBLANES:, :]) for d in dec_in]
    yield

    vals = {0: _values(soft[0][0], vv_ref, 0)}
    expand = expand_ref[...]
    state_in_x = _dot(state_in, expand)
    decay_in_x = _dot(decay_in, expand)
    chunk_decay = [(_dot(hi, expand) + _dot(lo, expand))[SUBLANES - 1:SUBLANES, :] for hi, lo in decay_tail]
    scores[2] = _scores(q_ref, ka_ref, 2)
    soft[1] = softmax(1, scores.pop(1))
    finish(0, vals.pop(0), soft.pop(0))
    yield

    vals[1] = _values(soft[1][0], vv_ref, 1)
    scores[3] = _scores(q_ref, ka_ref, 3)
    xs = act_ref[:, :SSM_WIDTH].astype(F32)
    xdec_b = (xs * state_in_x).astype(BF16)
    lane_head = lax.broadcasted_iota(jnp.int32, (CHUNK, GROUP_WIDTH), 1) // SSM_HEAD_DIM
    for j in range(nch):
        for g in range(SSM_GROUPS):
            xg = act_ref[j * CHUNK:(j + 1) * CHUNK, g * GROUP_WIDTH:(g + 1) * GROUP_WIDTH]
            for r in range(HEADS_PER_GROUP):
                xbd_ref[j, g, r * CHUNK:(r + 1) * CHUNK, :] = jnp.where(lane_head == r, xg, jnp.zeros_like(xg))
    soft[2] = softmax(2, scores.pop(2))
    finish(1, vals.pop(1), soft.pop(1))
    yield

    bc_b = act_ref[:, SSM_WIDTH:]
    bm = [[bc_b[j * CHUNK:(j + 1) * CHUNK, g * D_STATE:(g + 1) * D_STATE] for g in range(SSM_GROUPS)]
          for j in range(nch)]
    cm = [[bc_b[j * CHUNK:(j + 1) * CHUNK, (SSM_GROUPS + g) * D_STATE:(SSM_GROUPS + g + 1) * D_STATE]
           for g in range(SSM_GROUPS)] for j in range(nch)]
    cbm = [[_dot_nt(cm[j][g], bm[j][g]) for g in range(SSM_GROUPS)] for j in range(nch)]
    d_state = [[_dot_tn(bm[j][g], grp(xdec_b, j, g)) for g in range(SSM_GROUPS)] for j in range(nch)]
    vals[2] = _values(soft[2][0], vv_ref, 2)
    soft[3] = softmax(3, scores.pop(3))

    def intra_weights(j):
        per_group = []
        for g in range(SSM_GROUPS):
            w = []
            for r in range(HEADS_PER_GROUP):
                h = g * HEADS_PER_GROUP + r
                seg = acs[j][:, h:h + 1] - acs_dt_t[j][h:h + 1, :]
                w.append(cbm[j][g] * jnp.exp2(jnp.where(own, seg, -jnp.inf)))
            per_group.append((jnp.concatenate(w[:2], axis=1).astype(BF16),
                              jnp.concatenate(w[2:], axis=1).astype(BF16)))
        return per_group

    def intra_product(j, w):
        return [_dot(w[g][0], xbd_ref[j, g, 0:2 * CHUNK, :]) + _dot(w[g][1], xbd_ref[j, g, 2 * CHUNK:, :])
                for g in range(SSM_GROUPS)]

    yield
    w = {0: intra_weights(0)}
    finish(2, vals.pop(2), soft.pop(2))
    yield

    vals[3] = _values(soft[3][0], vv_ref, 3)
    y_diag = [intra_product(0, w.pop(0))]
    w[1] = intra_weights(1)
    yield

    y_diag.append(intra_product(1, w.pop(1)))
    w[2] = intra_weights(2)
    finish(3, vals.pop(3), soft.pop(3))
    yield

    y_diag.append(intra_product(2, w.pop(2)))
    w[3] = intra_weights(3)
    yield

    y_diag.append(intra_product(3, w.pop(3)))
    entering = []
    for g in range(SSM_GROUPS):
        state = state_ref[g]
        per_chunk = []
        for j in range(nch):
            per_chunk.append(state.astype(BF16))
            state = state * chunk_decay[j][:, g * GROUP_WIDTH:(g + 1) * GROUP_WIDTH] + d_state[j][g]
        state_ref[g] = state
        entering.append(per_chunk)
    yield

    y_rows = [jnp.concatenate([y_diag[j][g] + _dot(cm[j][g], entering[g][j]) * grp(decay_in_x, j, g)
                               for g in range(SSM_GROUPS)], axis=-1) for j in range(nch)]
    y = jnp.concatenate(y_rows, axis=0) + dskip_ref[...] * act_ref[:, :SSM_WIDTH].astype(F32)
    y = y * gz_ref[...].astype(F32)
    y = jnp.concatenate([_rms(y[:, :GROUP_WIDTH]), _rms(y[:, GROUP_WIDTH:])], axis=-1)
    o_ref[:, ATTN_WIDTH:] = (y * sn_ref[...]).astype(BF16)


FFN_PHASES = D_FF // FFN_FC + D_MODEL // FFN_FC


def _ffn_steps(mix_ref, x_ref, mod_ref, wo_ref, n2_ref, wgu_ref, wd_ref, fn_ref, o_ref, h_ref, a_ref):
    gain2 = n2_ref[...] * (1.0 + mod_ref[4:5, :])
    x1 = x_ref[...] + mod_ref[2:3, :] * _dot(mix_ref[...], wo_ref[...])
    h_ref[...] = (_rms(x1) * gain2 + mod_ref[3:4, :]).astype(BF16)
    o_ref[...] = x1
    yield
    for c0 in range(0, D_FF, FFN_FC):
        hb = h_ref[...]
        g = _dot(hb, wgu_ref[:, c0:c0 + FFN_FC])
        u = _dot(hb, wgu_ref[:, D_FF + c0:D_FF + c0 + FFN_FC])
        a_ref[:, c0:c0 + FFN_FC] = (_silu(g) * u).astype(BF16)
        yield
    for c0 in range(0, D_MODEL, FFN_FC):
        cols = slice(c0, c0 + FFN_FC)
        o_ref[:, cols] = o_ref[:, cols] + mod_ref[5:6, cols] * _dot(a_ref[...], wd_ref[:, cols])
        if c0 + FFN_FC == D_MODEL:
            o_ref[...] = _rms(o_ref[...]) * fn_ref[...]
        yield


def _mixer_ffn_kernel(q_ref, kt_ref, ktp_ref, v_ref, vp_ref, gz_ref, act_ref, dt_ref, bias_ref, sinks_ref, dtb_ref,
                      alog_ref, dskip_ref, an_ref, sn_ref, expand_ref, x_ref, mod_ref, wo_ref, n2_ref, wgu_ref,
                      wd_ref, fn_ref, o_ref, state_ref, ka_ref, vv_ref, xbd_ref, mix_ref, h_ref, a_ref):
    s = pl.program_id(1)
    last = pl.num_programs(1) - 1
    mixer = lambda first: _mixer_steps(first, q_ref, kt_ref, ktp_ref, v_ref, vp_ref, gz_ref, act_ref, dt_ref,
                                       bias_ref, sinks_ref, dtb_ref, alog_ref, dskip_ref, an_ref, sn_ref,
                                       expand_ref, mix_ref, state_ref, ka_ref, vv_ref, xbd_ref)
    ffn = lambda: _ffn_steps(mix_ref, x_ref, mod_ref, wo_ref, n2_ref, wgu_ref, wd_ref, fn_ref, o_ref, h_ref, a_ref)

    @pl.when(s == 0)
    def _():
        state_ref[...] = jnp.zeros_like(state_ref)
        vv_ref[:, :, LANES:] = jnp.ones((N_KV_HEADS, BLOCK + MIX_ROWS, LANES), BF16)
        for _ in mixer(True):
            pass

    @pl.when((s > 0) & (s < last))
    def _():
        ffn_steps, issued = ffn(), 0
        next(ffn_steps)
        for k, _ in enumerate(mixer(False)):
            while issued < ((k + 1) * (FFN_PHASES - 1)) // MIXER_YIELDS:
                next(ffn_steps)
                issued += 1
        assert k == MIXER_YIELDS - 1
        for _ in ffn_steps:
            pass

    @pl.when(s == last)
    def _():
        for _ in ffn():
            pass


def _mixer_ffn_call(q, kt, v, gz, act, dt, bias, sinks, dt_bias, a_log, d_skip_x, attn_norm, ssm_norm, expand,
                    x, mod, w_o, norm2, w_gu, w_down, final_norm):
    b, s, _ = q.shape
    rows = MIX_ROWS
    n_tiles = s // rows
    mix_tile = lambda i, c: jnp.minimum(c, n_tiles - 1)
    ffn_tile = lambda i, c: jnp.maximum(c - 1, 0)
    prev_blk = lambda i, c: jnp.maximum(mix_tile(i, c) * MIX_CHUNKS - 1, 0)
    mrow = lambda width: pl.BlockSpec((None, rows, width), lambda i, c: (i, mix_tile(i, c), 0))
    frow = lambda: pl.BlockSpec((None, rows, D_MODEL), lambda i, c: (i, ffn_tile(i, c), 0))
    const = lambda shape: pl.BlockSpec(shape, lambda i, c: (0,) * len(shape))
    resident = lambda shape: pl.BlockSpec(shape, lambda i, c: (0,) * len(shape), pipeline_mode=pl.Buffered(1))
    return pl.pallas_call(
        _mixer_ffn_kernel,
        grid=(b, n_tiles + 1),
        in_specs=[
            mrow(ATTN_WIDTH),
            pl.BlockSpec((None, KV_WIDTH, rows), lambda i, c: (i, 0, mix_tile(i, c))),
            pl.BlockSpec((None, KV_WIDTH, BLOCK), lambda i, c: (i, 0, prev_blk(i, c))),
            mrow(KV_WIDTH),
            pl.BlockSpec((None, BLOCK, KV_WIDTH), lambda i, c: (i, prev_blk(i, c), 0)),
            mrow(SSM_WIDTH), mrow(XBC_WIDTH), mrow(LANES),
            const((2, N_Q_HEADS, BLOCK, BLOCK)),
            pl.BlockSpec(memory_space=pltpu.SMEM),
            const((1, LANES)), const((1, LANES)),
            const((1, SSM_WIDTH)), const((1, ATTN_WIDTH)), const((1, SSM_WIDTH)),
            const((LANES, SSM_WIDTH)),
            frow(),
            pl.BlockSpec((None, 6, D_MODEL), lambda i, c: (i, 0, 0)),
            resident((D_MODEL, D_MODEL)), const((1, D_MODEL)),
            resident((D_MODEL, 2 * D_FF)), resident((D_FF, D_MODEL)), const((1, D_MODEL)),
        ],
        out_specs=frow(),
        out_shape=jax.ShapeDtypeStruct((b, s, D_MODEL), F32),
        scratch_shapes=[
            pltpu.VMEM((SSM_GROUPS, D_STATE, GROUP_WIDTH), F32),
            pltpu.VMEM((N_KV_HEADS, 2, KV_WIDTH, BLOCK + rows), BF16),
            pltpu.VMEM((N_KV_HEADS, BLOCK + rows, 2 * LANES), BF16),
            pltpu.VMEM((MIX_CHUNKS, SSM_GROUPS, HEADS_PER_GROUP * CHUNK, GROUP_WIDTH), BF16),
            pltpu.VMEM((rows, ATTN_WIDTH + SSM_WIDTH), BF16),
            pltpu.VMEM((rows, D_MODEL), BF16),
            pltpu.VMEM((rows, D_FF), BF16),
        ],
        compiler_params=pltpu.CompilerParams(
            dimension_semantics=("parallel", "arbitrary"), vmem_limit_bytes=VMEM_LIMIT_BYTES),
        name="mixer_ffn",
    )(q, kt, kt, v, v, gz, act, dt, bias, sinks, dt_bias, a_log, d_skip_x, attn_norm, ssm_norm, expand,
      x, mod, w_o, norm2, w_gu, w_down, final_norm)


def _head_expand_matrix():
    e = np.zeros((LANES, SSM_WIDTH), np.float32)
    for h in range(SSM_HEADS):
        e[h, h * SSM_HEAD_DIM:(h + 1) * SSM_HEAD_DIM] = 1.0
    return e


def kernel(x, c, ada_w, ada_b, norm1, w_in, conv_w, conv_b, dt_bias, A_log, D_skip, sinks, attn_out_norm,
           ssm_out_norm, w_o, norm2, w_gate_up, w_down, rel_bias, final_norm):
    assert ada_w.shape[0] == 1, "the final RMSNorm is fused into the (single) layer's FFN call"
    b = x.shape[0]
    bias = _bias_call(rel_bias)
    expand = jnp.asarray(_head_expand_matrix(), BF16)
    pad_heads = lambda v: jnp.pad(v, (0, LANES - SSM_HEADS))[None, :]
    c_pad = jnp.pad(c, ((0, SUBLANES - b), (0, 0)))
    mod = _ada_call(c_pad, ada_w[0], ada_b[0][None, :])[:b].reshape(b, 6, D_MODEL)
    w_main = w_in.astype(BF16)[0]
    w_dt = jnp.pad(w_main[:, XBC_END:], ((0, 0), (0, LANES - SSM_HEADS)))
    q, kt, v, gz, act, dt = _inproj_call(x, mod, norm1[0][None, :], w_main, w_dt, conv_w[0], conv_b[0][None, :])
    return _mixer_ffn_call(
        q, kt, v, gz, act, dt, bias, sinks[0], pad_heads(dt_bias[0]), pad_heads(A_log[0]),
        jnp.repeat(D_skip[0], SSM_HEAD_DIM)[None, :], attn_out_norm[0][None, :], ssm_out_norm[0][None, :], expand,
        x, mod, w_o[0].astype(BF16), norm2[0][None, :], w_gate_up[0].astype(BF16), w_down[0].astype(BF16),
        final_norm[None, :])
```
